```python
import math
import jax
import jax.numpy as jnp
from jax import lax
import numpy as np

D_MODEL = 2048
BATCH = 2
SEQ = 16384
DEPTH = 4

GRID_W = 64
CTX_LEN = 256
EPS = 1e-6
ROPE_THETA = 10000.0
Q_BLOCK = 128
N_BRANCH = 4
N_MOD = 6

S5_GROUP = 16
S5_WIDTH = 768
S5_GROUPS = S5_WIDTH // S5_GROUP
S5_STATE = 64

GQA_HEADS = 8
GQA_KV_HEADS = 2
GQA_HEAD_DIM = 128
GQA_OUT = GQA_HEADS * GQA_HEAD_DIM

MLA_HEADS = 8
MLA_Q_RANK = 512
MLA_KV_RANK = 256
MLA_NOPE = 128
MLA_ROPE = 64
MLA_V = 128
MLA_OUT = MLA_HEADS * MLA_V

MB_HEADS = 16
MB_HEAD_DIM = 64
MB_INNER = MB_HEADS * MB_HEAD_DIM
MB_GROUPS = 2
MB_HEADS_PER_GROUP = MB_HEADS // MB_GROUPS
MB_STATE = 128
MB_CONV = 5
MB_XBC = MB_INNER + 2 * MB_GROUPS * MB_STATE
SSD_CHUNK = 128

MOE_GROUPS = 4
MOE_PER_GROUP = 8
N_EXPERTS = MOE_GROUPS * MOE_PER_GROUP
MOE_TOPK = 2
MOE_FF = 512

IN_WIDTHS = (S5_WIDTH, GQA_OUT, GQA_KV_HEADS * GQA_HEAD_DIM, GQA_KV_HEADS * GQA_HEAD_DIM,
             MLA_Q_RANK, MLA_KV_RANK, MLA_ROPE, MB_INNER, MB_XBC, 2 * MB_HEADS, N_BRANCH * D_MODEL)
IN_WIDTH = sum(IN_WIDTHS)

kernel_name = 'hybrid_s5_gqa_mla_ssd_hmoe_dit'

F32 = jnp.float32


def rms_norm(x, g):
    xf = x.astype(F32)
    return xf * lax.rsqrt(jnp.mean(xf * xf, axis=-1, keepdims=True) + EPS) * g.astype(F32)


def modulate(h, shift, scale):
    return h * (1.0 + scale) + shift


def split_cols(t, widths):
    cuts = np.cumsum(np.array(widths))[:-1].tolist()
    return jnp.split(t, cuts, axis=-1)


def axial_rope_tables(n_tok, dim):
    rows = n_tok // GRID_W
    row = jnp.repeat(jnp.arange(rows, dtype=F32), GRID_W)
    col = jnp.tile(jnp.arange(GRID_W, dtype=F32), rows)
    quarter = dim // 4
    inv_freq = ROPE_THETA ** (-jnp.arange(quarter, dtype=F32) / quarter)
    ang = jnp.concatenate([row[:, None] * inv_freq, col[:, None] * inv_freq], axis=-1)
    return jnp.cos(ang), jnp.sin(ang)


def apply_axial_rope(x, cos, sin):
    b, l, h, dim = x.shape
    q4 = dim // 4
    xr = x.astype(F32).reshape(b, l, h, 2, 2, q4)
    x1, x2 = xr[..., 0, :], xr[..., 1, :]
    cs = cos.reshape(1, l, 1, 2, q4)
    sn = sin.reshape(1, l, 1, 2, q4)
    out = jnp.stack([x1 * cs - x2 * sn, x1 * sn + x2 * cs], axis=-2)
    return out.reshape(b, l, h, dim)


def attend(q, k, v):
    s = jnp.einsum('bqhgd,bkhd->bhgqk', q.astype(F32), k.astype(F32)) * (q.shape[-1] ** -0.5)
    w = jax.nn.softmax(s, axis=-1)
    return jnp.einsum('bhgqk,bkhe->bqhge', w, v.astype(F32))


def blocked_attend(q, k, v):
    b, l, hk, g, d = q.shape
    nb = l // Q_BLOCK
    q_blocks = jnp.moveaxis(q.reshape(b, nb, Q_BLOCK, hk, g, d), 1, 0)
    o = lax.map(lambda qb: attend(qb, k, v), q_blocks)
    return jnp.moveaxis(o, 0, 1).reshape(b, l, hk * g * o.shape[-1])


def dwconv_centred(u, w, bias):
    k = w.shape[1]
    out = lax.conv_general_dilated(u.astype(F32), w.astype(F32).T[:, None, :], window_strides=(1,),
                                   padding=[(k // 2, k // 2)], dimension_numbers=('NWC', 'WIO', 'NWC'),
                                   feature_group_count=u.shape[-1])
    return out + bias.astype(F32)


def _linear_recurrence(e1, e2):
    a1, b1 = e1
    a2, b2 = e2
    return a1 * a2, a2 * b1 + b2


def s5_mixer(u_c, u_l, p, need_ctx):
    lam = lax.complex(p['s5_lam_re'].astype(F32), p['s5_lam_im'].astype(F32))
    step = jnp.exp(p['s5_log_step'].astype(F32))[..., None]
    lam_bar = jnp.exp(lam * step)
    b_bar = ((lam_bar - 1.0) / lam)[..., None] * lax.complex(p['s5_b_re'].astype(F32), p['s5_b_im'].astype(F32))
    c_mat = lax.complex(p['s5_c_re'].astype(F32), p['s5_c_im'].astype(F32))

    def scan_dir(u, d, h0, reverse):
        b, l, _ = u.shape
        ug = u.astype(F32).reshape(b, l, S5_GROUPS, S5_GROUP).astype(jnp.complex64)
        bu = jnp.einsum('gnc,blgc->blgn', b_bar[d], ug)
        if h0 is not None:
            edge = l - 1 if reverse else 0
            bu = bu.at[:, edge].add(lam_bar[d] * h0)
        a = jnp.broadcast_to(lam_bar[d], bu.shape)
        _, h = lax.associative_scan(_linear_recurrence, (a, bu), axis=1, reverse=reverse)
        return h

    def readout(h, d):
        b, l = h.shape[0], h.shape[1]
        return jnp.real(jnp.einsum('gcn,blgn->blgc', c_mat[d], h)).reshape(b, l, S5_WIDTH)

    def finish(u, hf, hb):
        y = readout(hf, 0) + readout(hb, 1) + p['s5_d'].astype(F32) * u.astype(F32)
        g = jax.nn.gelu(y)
        return g * jax.nn.sigmoid(g @ p['s5_w_glu'])

    h_cf = scan_dir(u_c, 0, None, False)
    h_cb = scan_dir(u_c, 1, None, True)
    h_lf = scan_dir(u_l, 0, h_cf[:, -1], False)
    h_lb = scan_dir(u_l, 1, h_cb[:, 0], True)
    y_l = finish(u_l, h_lf, h_lb)
    y_c = finish(u_c, h_cf, h_cb) if need_ctx else None
    return y_c, y_l


def gqa_mixer(q_c, k_c, v_c, q_l, k_l, v_l, p, rope, need_ctx):
    def heads(t, n):
        return t.reshape(t.shape[0], t.shape[1], n, GQA_HEAD_DIM)

    def grouped(q):
        return q.reshape(q.shape[0], q.shape[1], GQA_KV_HEADS, GQA_HEADS // GQA_KV_HEADS, GQA_HEAD_DIM)

    kc = rms_norm(heads(k_c, GQA_KV_HEADS), p['gqa_k_norm'])
    vc = heads(v_c, GQA_KV_HEADS)
    ql = apply_axial_rope(rms_norm(heads(q_l, GQA_HEADS), p['gqa_q_norm']), *rope)
    kl = apply_axial_rope(rms_norm(heads(k_l, GQA_KV_HEADS), p['gqa_k_norm']), *rope)
    vl = heads(v_l, GQA_KV_HEADS)
    k_all = jnp.concatenate([kc, kl.astype(kc.dtype)], axis=1)
    v_all = jnp.concatenate([vc, vl], axis=1)
    y_l = blocked_attend(grouped(ql), k_all, v_all)
    y_c = None
    if need_ctx:
        qc = rms_norm(heads(q_c, GQA_HEADS), p['gqa_q_norm'])
        o = attend(grouped(qc), kc, vc)
        y_c = o.reshape(o.shape[0], o.shape[1], GQA_OUT)
    return y_c, y_l


def mla_mixer(cq_c, ckv_c, kr_c, cq_l, ckv_l, kr_l, p, rope, need_ctx):
    def keys_values(ckv, kr, rope_tabs):
        b, l, _ = ckv.shape
        lat = rms_norm(ckv, p['mla_kv_norm'])
        k_nope = rms_norm((lat @ p['mla_w_uk']).reshape(b, l, MLA_HEADS, MLA_NOPE), p['mla_kn_norm'])
        v = (lat @ p['mla_w_uv']).reshape(b, l, MLA_HEADS, MLA_V)
        k_rope = rms_norm(kr.reshape(b, l, 1, MLA_ROPE), p['mla_kr_norm'])
        if rope_tabs is not None:
            k_rope = apply_axial_rope(k_rope, *rope_tabs)
        k = jnp.concatenate([k_nope, jnp.broadcast_to(k_rope, (b, l, MLA_HEADS, MLA_ROPE))], axis=-1)
        return k, v

    def queries(cq, rope_tabs):
        b, l, _ = cq.shape
        q = (rms_norm(cq, p['mla_q_norm']) @ p['mla_w_uq']).reshape(b, l, MLA_HEADS, MLA_NOPE + MLA_ROPE)
        q_nope = rms_norm(q[..., :MLA_NOPE], p['mla_qn_norm'])
        q_rope = rms_norm(q[..., MLA_NOPE:], p['mla_qr_norm'])
        if rope_tabs is not None:
            q_rope = apply_axial_rope(q_rope, *rope_tabs)
        return jnp.concatenate([q_nope, q_rope], axis=-1)[:, :, :, None, :]

    kc, vc = keys_values(ckv_c, kr_c, None)
    kl, vl = keys_values(ckv_l, kr_l, rope)
    k_all = jnp.concatenate([kc, kl], axis=1)
    v_all = jnp.concatenate([vc, vl], axis=1)
    y_l = blocked_attend(queries(cq_l, rope), k_all, v_all)
    y_c = None
    if need_ctx:
        o = attend(queries(cq_c, None), kc, vc)
        y_c = o.reshape(o.shape[0], o.shape[1], MLA_OUT)
    return y_c, y_l


def ssd_scan(xdt, da, bm, cm, h0):
    b, l, g, j, pd = xdt.shape
    n = bm.shape[-1]
    nc = l // SSD_CHUNK
    x = xdt.astype(F32).reshape(b, nc, SSD_CHUNK, g, j, pd)
    a = da.astype(F32).reshape(b, nc, SSD_CHUNK, g, j)
    bc = bm.astype(F32).reshape(b, nc, SSD_CHUNK, g, n)
    cc = cm.astype(F32).reshape(b, nc, SSD_CHUNK, g, n)
    a_cum = jnp.cumsum(a, axis=2)
    scan_order = jnp.tril(jnp.ones((SSD_CHUNK, SSD_CHUNK), dtype=bool))[:, :, None, None]
    seg = a_cum[:, :, :, None] - a_cum[:, :, None, :]
    decay = jnp.where(scan_order, jnp.exp(jnp.where(scan_order, seg, 0.0)), 0.0)
    scores = jnp.einsum('bcqgn,bcsgn->bcqsg', cc, bc)
    y_diag = jnp.einsum('bcqsg,bcqsgj,bcsgjp->bcqgjp', scores, decay, x)
    decay_end = jnp.exp(a_cum[:, :, -1:] - a_cum)
    states = jnp.einsum('bcsgn,bcsgj,bcsgjp->bcgjpn', bc, decay_end, x)
    chunk_decay = jnp.exp(a_cum[:, :, -1])

    def carry(h, inp):
        s, dcy = inp
        return dcy[..., None, None] * h + s, h

    h_last, h_in = lax.scan(carry, h0, (jnp.moveaxis(states, 1, 0), jnp.moveaxis(chunk_decay, 1, 0)))
    h_in = jnp.moveaxis(h_in, 0, 1)
    y_off = jnp.einsum('bcqgn,bcgjpn,bcqgj->bcqgjp', cc, h_in, jnp.exp(a_cum))
    return (y_diag + y_off).reshape(b, l, g, j, pd), h_last


def mamba_mixer(z_c, xbc_c, dt_c, z_l, xbc_l, dt_l, p, need_ctx):
    a_dec = -jnp.exp(p['mb_a_log'].astype(F32))

    def prep(xbc, dt_raw):
        b, l, _ = xbc.shape
        xbc = jax.nn.silu(dwconv_centred(xbc, p['mb_conv_w'], p['mb_conv_b']))
        xs, bm, cm = split_cols(xbc, (MB_INNER, MB_GROUPS * MB_STATE, MB_GROUPS * MB_STATE))
        xs = xs.reshape(b, l, MB_GROUPS, MB_HEADS_PER_GROUP, MB_HEAD_DIM)
        bm = bm.reshape(b, l, MB_GROUPS, MB_STATE)
        cm = cm.reshape(b, l, MB_GROUPS, MB_STATE)
        dt = jax.nn.softplus(dt_raw.astype(F32).reshape(b, l, 2, MB_HEADS) + p['mb_dt_bias'].astype(F32))
        return xs, bm, cm, dt

    def scan_dir(xs, bm, cm, dt, d, h0):
        b, l = xs.shape[0], xs.shape[1]
        dt_d = dt[:, :, d].reshape(b, l, MB_GROUPS, MB_HEADS_PER_GROUP)
        xdt = xs * dt_d[..., None]
        da = dt_d * a_dec[d].reshape(MB_GROUPS, MB_HEADS_PER_GROUP)
        if d == 1:
            xdt, da, bm, cm = (jnp.flip(t, axis=1) for t in (xdt, da, bm, cm))
        y, h_last = ssd_scan(xdt, da, bm, cm, h0)
        if d == 1:
            y = jnp.flip(y, axis=1)
        return y, h_last

    def finish(xs, yf, yb, z):
        b, l = xs.shape[0], xs.shape[1]
        skip = xs.astype(F32) * p['mb_d'].astype(F32).reshape(MB_GROUPS, MB_HEADS_PER_GROUP, 1)
        y = (yf + yb + skip).reshape(b, l, MB_INNER)
        return rms_norm(y * jax.nn.silu(z.astype(F32)), p['mb_norm'])

    xs_c, bm_c, cm_c, dtc = prep(xbc_c, dt_c)
    xs_l, bm_l, cm_l, dtl = prep(xbc_l, dt_l)
    h_zero = jnp.zeros((xs_c.shape[0], MB_GROUPS, MB_HEADS_PER_GROUP, MB_HEAD_DIM, MB_STATE), F32)
    yc_f, hc_f = scan_dir(xs_c, bm_c, cm_c, dtc, 0, h_zero)
    yc_b, hc_b = scan_dir(xs_c, bm_c, cm_c, dtc, 1, h_zero)
    yl_f, _ = scan_dir(xs_l, bm_l, cm_l, dtl, 0, hc_f)
    yl_b, _ = scan_dir(xs_l, bm_l, cm_l, dtl, 1, hc_b)
    y_l = finish(xs_l, yl_f, yl_b, z_l)
    y_c = finish(xs_c, yc_f, yc_b, z_c) if need_ctx else None
    return y_c, y_l


def merge_branches(ys, gate_raw, p):
    b, l, _ = gate_raw.shape
    gates = jax.nn.sigmoid(gate_raw.astype(F32).reshape(b, l, N_BRANCH, D_MODEL))
    w_brs = (p['w_br_s5'], p['w_br_gqa'], p['w_br_mla'], p['w_br_mb'])
    m = gates[:, :, 0] * (ys[0] @ w_brs[0])
    for i in range(1, N_BRANCH):
        m = m + gates[:, :, i] * (ys[i] @ w_brs[i])
    return m @ p['w_out']


def token_mixers(h_c, h_l, p, rope_gqa, rope_mla, need_ctx):
    (s5_c, gq_c, gk_c, gv_c, mq_c, mkv_c, mkr_c, mz_c, mxbc_c, mdt_c, gate_c) = split_cols(h_c @ p['w_in'], IN_WIDTHS)
    (s5_l, gq_l, gk_l, gv_l, mq_l, mkv_l, mkr_l, mz_l, mxbc_l, mdt_l, gate_l) = split_cols(h_l @ p['w_in'], IN_WIDTHS)
    ya_c, ya_l = s5_mixer(s5_c, s5_l, p, need_ctx)
    yb_c, yb_l = gqa_mixer(gq_c, gk_c, gv_c, gq_l, gk_l, gv_l, p, rope_gqa, need_ctx)
    yc_c, yc_l = mla_mixer(mq_c, mkv_c, mkr_c, mq_l, mkv_l, mkr_l, p, rope_mla, need_ctx)
    yd_c, yd_l = mamba_mixer(mz_c, mxbc_c, mdt_c, mz_l, mxbc_l, mdt_l, p, need_ctx)
    out_l = merge_branches((ya_l, yb_l, yc_l, yd_l), gate_l, p)
    out_c = merge_branches((ya_c, yb_c, yc_c, yd_c), gate_c, p) if need_ctx else None
    return out_c, out_l


def hier_moe(t, p):
    n = t.shape[0]
    tf = t.astype(F32)
    g_prob = jax.nn.softmax(tf @ p['moe_w_group'] + p['moe_b_group'], axis=-1)
    g_w, g_idx = lax.top_k(g_prob, 1)
    e_logits = (tf @ p['moe_w_expert'] + p['moe_b_expert']).reshape(n, MOE_GROUPS, MOE_PER_GROUP)
    e_logits = jnp.take_along_axis(e_logits, g_idx[:, :, None], axis=1)[:, 0]
    e_w, e_idx = lax.top_k(jax.nn.softmax(e_logits, axis=-1), MOE_TOPK)
    w = g_w * e_w / jnp.sum(e_w, axis=-1, keepdims=True)
    expert = g_idx * MOE_PER_GROUP + e_idx
    gate = jnp.sum(jax.nn.one_hot(expert, N_EXPERTS, dtype=F32) * w[..., None], axis=1)
    out = jnp.zeros((n, t.shape[1]), F32)
    for e in range(N_EXPERTS):
        hid = jax.nn.silu(tf @ p['moe_w1'][e]) * (tf @ p['moe_w3'][e])
        out = out + gate[:, e:e + 1] * (hid @ p['moe_w2'][e])
    return out


def setup_inputs(seed: int = 0) -> dict:
    key = jax.random.key(seed)
    keys = iter(jax.random.split(key, 64))

    def nrm(shape, scale):
        return jax.random.normal(next(keys), shape, F32) * scale

    def unif(shape, lo, hi):
        return jax.random.uniform(next(keys), shape, F32, lo, hi)

    def gain(shape):
        return 1.0 + nrm(shape, 0.02)

    dl = DEPTH
    d = D_MODEL
    mb_dt = jnp.exp(unif((dl, 2, MB_HEADS), math.log(1e-3), math.log(1e-1)))
    return {
        'x': nrm((BATCH, SEQ, d), 1.0),
        'c': nrm((BATCH, d), 1.0),
        'ctx': nrm((BATCH, CTX_LEN, d), 1.0),
        'c_ctx': nrm((d,), 1.0),
        'norm1': gain((dl, d)),
        'norm2': gain((dl, d)),
        'w_ada': nrm((dl, d, N_MOD * d), 0.5 * d ** -0.5),
        'b_ada': nrm((dl, N_MOD * d), 0.01),
        'w_in': nrm((dl, d, IN_WIDTH), d ** -0.5),
        's5_lam_re': -0.5 + nrm((dl, 2, S5_GROUPS, S5_STATE), 0.01),
        's5_lam_im': math.pi * jnp.arange(S5_STATE, dtype=F32) + nrm((dl, 2, S5_GROUPS, S5_STATE), 0.01),
        's5_log_step': unif((dl, 2, S5_GROUPS), math.log(1e-3), math.log(1e-1)),
        's5_b_re': nrm((dl, 2, S5_GROUPS, S5_STATE, S5_GROUP), (2 * S5_GROUP) ** -0.5),
        's5_b_im': nrm((dl, 2, S5_GROUPS, S5_STATE, S5_GROUP), (2 * S5_GROUP) ** -0.5),
        's5_c_re': nrm((dl, 2, S5_GROUPS, S5_GROUP, S5_STATE), (2 * S5_STATE) ** -0.5),
        's5_c_im': nrm((dl, 2, S5_GROUPS, S5_GROUP, S5_STATE), (2 * S5_STATE) ** -0.5),
        's5_d': nrm((dl, S5_WIDTH), 1.0),
        's5_w_glu': nrm((dl, S5_WIDTH, S5_WIDTH), S5_WIDTH ** -0.5),
        'gqa_q_norm': gain((dl, GQA_HEAD_DIM)),
        'gqa_k_norm': gain((dl, GQA_HEAD_DIM)),
        'mla_q_norm': gain((dl, MLA_Q_RANK)),
        'mla_kv_norm': gain((dl, MLA_KV_RANK)),
        'mla_w_uq': nrm((dl, MLA_Q_RANK, MLA_HEADS * (MLA_NOPE + MLA_ROPE)), MLA_Q_RANK ** -0.5),
        'mla_w_uk': nrm((dl, MLA_KV_RANK, MLA_HEADS * MLA_NOPE), MLA_KV_RANK ** -0.5),
        'mla_w_uv': nrm((dl, MLA_KV_RANK, MLA_OUT), MLA_KV_RANK ** -0.5),
        'mla_qn_norm': gain((dl, MLA_NOPE)),
        'mla_kn_norm': gain((dl, MLA_NOPE)),
        'mla_qr_norm': gain((dl, MLA_ROPE)),
        'mla_kr_norm': gain((dl, MLA_ROPE)),
        'mb_conv_w': nrm((dl, MB_XBC, MB_CONV), MB_CONV ** -0.5),
        'mb_conv_b': nrm((dl, MB_XBC), 0.01),
        'mb_dt_bias': mb_dt + jnp.log(-jnp.expm1(-mb_dt)),
        'mb_a_log': jnp.log(unif((dl, 2, MB_HEADS), 1.0, 16.0)),
        'mb_d': 1.0 + nrm((dl, MB_HEADS), 0.1),
        'mb_norm': gain((dl, MB_INNER)),
        'w_br_s5': nrm((dl, S5_WIDTH, d), S5_WIDTH ** -0.5),
        'w_br_gqa': nrm((dl, GQA_OUT, d), GQA_OUT ** -0.5),
        'w_br_mla': nrm((dl, MLA_OUT, d), MLA_OUT ** -0.5),
        'w_br_mb': nrm((dl, MB_INNER, d), MB_INNER ** -0.5),
        'w_out': nrm((dl, d, d), d ** -0.5),
        'moe_w_group': nrm((dl, d, MOE_GROUPS), d ** -0.5),
        'moe_b_group': nrm((dl, MOE_GROUPS), 0.01),
        'moe_w_expert': nrm((dl, d, N_EXPERTS), d ** -0.5),
        'moe_b_expert': nrm((dl, N_EXPERTS), 0.01),
        'moe_w1': nrm((dl, N_EXPERTS, d, MOE_FF), d ** -0.5),
        'moe_w3': nrm((dl, N_EXPERTS, d, MOE_FF), d ** -0.5),
        'moe_w2': nrm((dl, N_EXPERTS, MOE_FF, d), MOE_FF ** -0.5),
    }


def reference(x, c, ctx, c_ctx, norm1, norm2, w_ada, b_ada, w_in,
              s5_lam_re, s5_lam_im, s5_log_step, s5_b_re, s5_b_im, s5_c_re, s5_c_im, s5_d, s5_w_glu,
              gqa_q_norm, gqa_k_norm,
              mla_q_norm, mla_kv_norm, mla_w_uq, mla_w_uk, mla_w_uv, mla_qn_norm, mla_kn_norm, mla_qr_norm, mla_kr_norm,
              mb_conv_w, mb_conv_b, mb_dt_bias, mb_a_log, mb_d, mb_norm,
              w_br_s5, w_br_gqa, w_br_mla, w_br_mb, w_out,
              moe_w_group, moe_b_group, moe_w_expert, moe_b_expert, moe_w1, moe_w3, moe_w2):
    b, n_lat, _ = x.shape
    rope_gqa = axial_rope_tables(n_lat, GQA_HEAD_DIM)
    rope_mla = axial_rope_tables(n_lat, MLA_ROPE)
    stacked = dict(norm1=norm1, norm2=norm2, w_ada=w_ada, b_ada=b_ada, w_in=w_in,
                   s5_lam_re=s5_lam_re, s5_lam_im=s5_lam_im, s5_log_step=s5_log_step, s5_b_re=s5_b_re,
                   s5_b_im=s5_b_im, s5_c_re=s5_c_re, s5_c_im=s5_c_im, s5_d=s5_d, s5_w_glu=s5_w_glu,
                   gqa_q_norm=gqa_q_norm, gqa_k_norm=gqa_k_norm,
                   mla_q_norm=mla_q_norm, mla_kv_norm=mla_kv_norm, mla_w_uq=mla_w_uq, mla_w_uk=mla_w_uk,
                   mla_w_uv=mla_w_uv, mla_qn_norm=mla_qn_norm, mla_kn_norm=mla_kn_norm,
                   mla_qr_norm=mla_qr_norm, mla_kr_norm=mla_kr_norm,
                   mb_conv_w=mb_conv_w, mb_conv_b=mb_conv_b, mb_dt_bias=mb_dt_bias, mb_a_log=mb_a_log,
                   mb_d=mb_d, mb_norm=mb_norm,
                   w_br_s5=w_br_s5, w_br_gqa=w_br_gqa, w_br_mla=w_br_mla, w_br_mb=w_br_mb, w_out=w_out,
                   moe_w_group=moe_w_group, moe_b_group=moe_b_group, moe_w_expert=moe_w_expert,
                   moe_b_expert=moe_b_expert, moe_w1=moe_w1, moe_w3=moe_w3, moe_w2=moe_w2)
    xc = ctx
    for i in range(DEPTH):
        p = {name: arr[i] for name, arr in stacked.items()}
        need_ctx = i < DEPTH - 1
        mod_l = (jax.nn.silu(c.astype(F32)) @ p['w_ada'] + p['b_ada']).reshape(b, N_MOD, 1, D_MODEL)
        mod_c = (jax.nn.silu(c_ctx.astype(F32)) @ p['w_ada'] + p['b_ada']).reshape(N_MOD, 1, 1, D_MODEL)
        h_l = modulate(rms_norm(x, p['norm1']), mod_l[:, 0], mod_l[:, 1])
        h_c = modulate(rms_norm(xc, p['norm1']), mod_c[0], mod_c[1])
        mix_c, mix_l = token_mixers(h_c, h_l, p, rope_gqa, rope_mla, need_ctx)
        x = x + mod_l[:, 2] * mix_l
        h2_l = modulate(rms_norm(x, p['norm2']), mod_l[:, 3], mod_l[:, 4])
        if need_ctx:
            xc = xc + mod_c[2] * mix_c
            h2_c = modulate(rms_norm(xc, p['norm2']), mod_c[3], mod_c[4])
            n_c = h2_c.shape[0] * h2_c.shape[1]
            tokens = jnp.concatenate([h2_c.reshape(-1, D_MODEL), h2_l.reshape(-1, D_MODEL)], axis=0)
            ff = hier_moe(tokens, p)
            xc = xc + mod_c[5] * ff[:n_c].reshape(xc.shape)
            x = x + mod_l[:, 5] * ff[n_c:].reshape(x.shape)
        else:
            x = x + mod_l[:, 5] * hier_moe(h2_l.reshape(-1, D_MODEL), p).reshape(x.shape)
    return x
```

```python
import functools
import math

import jax
import jax.numpy as jnp
from jax import lax
from jax.experimental import pallas as pl
from jax.experimental.pallas import tpu as pltpu

F32 = jnp.float32
BF16 = jnp.bfloat16
HIGHEST = lax.Precision.HIGHEST

EPS = 1e-6
ROPE_THETA = 10000.0
GRID_W = 64
D_MODEL = 2048
N_MOD = 6

S5_GROUP = 16
S5_WIDTH = 768
S5_GROUPS = S5_WIDTH // S5_GROUP
S5_STATE = 64
S5_CHUNK = 16

GQA_HEADS = 8
GQA_KV_HEADS = 2
HEAD_DIM = 128
GQA_OUT = GQA_HEADS * HEAD_DIM

MLA_HEADS = 8
MLA_Q_RANK = 512
MLA_KV_RANK = 256
MLA_NOPE = 128
MLA_ROPE = 64
MLA_OUT = MLA_HEADS * HEAD_DIM

MB_HEADS = 16
MB_HEAD_DIM = 64
MB_INNER = MB_HEADS * MB_HEAD_DIM
MB_GROUPS = 2
MB_HPG = MB_HEADS // MB_GROUPS
MB_STATE = 128
MB_CONV = 5
MB_XBC = MB_INNER + 2 * MB_GROUPS * MB_STATE
SSD_CHUNK = 128

MOE_GROUPS = 4
MOE_PER_GROUP = 8
N_EXPERTS = MOE_GROUPS * MOE_PER_GROUP
MOE_FF = 512
MOE_TILE = 512

LANE = 128
VMEM_LIMIT = 56 * 1024 * 1024

OFF_GATE = 0
OFF_GQ = 8192
OFF_MZ = 9216
OFF_XBC = 10240
OFF_MQ = 11776
OFF_S5 = 12288
OFF_GK = 13056
OFF_GV = 13312
OFF_MKV = 13568
OFF_MKR = 13824
OFF_DT = 13952
NW = 14592


def _tile(n, *cands):
    for c in cands:
        if n % c == 0:
            return c
    return n


def _cp(sem, vmem=VMEM_LIMIT):
    return pltpu.CompilerParams(dimension_semantics=sem, vmem_limit_bytes=vmem)


def _sigmoid(x):
    return 1.0 / (1.0 + jnp.exp(-x))


def _silu(x):
    return x * _sigmoid(x)


def _ada_kernel(c_ref, w_ref, b_ref, o_ref):
    s = _silu(c_ref[...])
    o_ref[0] = jnp.dot(s.astype(BF16), w_ref[0].astype(BF16), preferred_element_type=F32) + b_ref[0]


def _ada(cvec, w_ada, b_ada):
    depth, d, n = w_ada.shape
    tn = _tile(n, 1024, 512)
    return pl.pallas_call(
        _ada_kernel,
        grid=(depth, n // tn),
        in_specs=[pl.BlockSpec((8, d), lambda l, j: (0, 0)),
                  pl.BlockSpec((1, d, tn), lambda l, j: (l, 0, j)),
                  pl.BlockSpec((1, 1, tn), lambda l, j: (l, 0, j))],
        out_specs=pl.BlockSpec((1, 8, tn), lambda l, j: (l, 0, j)),
        out_shape=jax.ShapeDtypeStruct((depth, 8, n), F32),
        compiler_params=_cp(("parallel", "parallel")),
        name="ada_mod",
    )(cvec, w_ada, b_ada.reshape(depth, 1, n))


ROW_CHUNK = 128


def _norm_mod(x, g, ml, mc, row_start, n_lat, row0):
    r = lax.rsqrt(jnp.mean(x * x, axis=-1, keepdims=True) + EPS)
    rows = row_start + lax.broadcasted_iota(jnp.int32, (x.shape[0], 1), 0)
    is_ctx = rows >= n_lat
    shift = jnp.where(is_ctx, mc[row0:row0 + 1], ml[row0:row0 + 1])
    scale = jnp.where(is_ctx, mc[row0 + 1:row0 + 2], ml[row0 + 1:row0 + 2])
    return x * r * g * (1.0 + scale) + shift


def _inproj_kernel(x_ref, g_ref, ml_ref, mc_ref, w_ref, o_ref, h_sc, *, n_lat, tm):
    @pl.when(pl.program_id(2) == 0)
    def _():
        def chunk(r, carry):
            rs = pl.multiple_of(r * ROW_CHUNK, ROW_CHUNK)
            h = _norm_mod(x_ref[0, pl.ds(rs, ROW_CHUNK), :], g_ref[...], ml_ref[0], mc_ref[0],
                          pl.program_id(1) * tm + rs, n_lat, 0)
            h_sc[pl.ds(rs, ROW_CHUNK), :] = h.astype(BF16)
            return carry

        lax.fori_loop(0, tm // ROW_CHUNK, chunk, 0)

    o_ref[0] = jnp.dot(h_sc[...], w_ref[...], preferred_element_type=F32).astype(o_ref.dtype)


def _inproj(x, gain, mods, w, n_lat):
    b, lt, d = x.shape
    n = w.shape[1]
    tm = _tile(lt, 1280, 256)
    tn = _tile(n, 768)
    return pl.pallas_call(
        functools.partial(_inproj_kernel, n_lat=n_lat, tm=tm),
        grid=(b, lt // tm, n // tn),
        in_specs=[pl.BlockSpec((1, tm, d), lambda bi, i, j: (bi, i, 0)),
                  pl.BlockSpec((1, d), lambda bi, i, j: (0, 0)),
                  pl.BlockSpec((1, 8, d), lambda bi, i, j: (bi, 0, 0)),
                  pl.BlockSpec((1, 8, d), lambda bi, i, j: (b, 0, 0)),
                  pl.BlockSpec((d, tn), lambda bi, i, j: (0, j))],
        out_specs=pl.BlockSpec((1, tm, tn), lambda bi, i, j: (bi, i, j)),
        out_shape=jax.ShapeDtypeStruct((b, lt, n), BF16),
        scratch_shapes=[pltpu.VMEM((tm, d), BF16)],
        compiler_params=_cp(("parallel", "parallel", "arbitrary")),
        name="in_proj",
    )(x, gain.reshape(1, d), mods, mods, w)


def _head_norm_rope(x, gain, cos, sin, rmat, n_valid, scale):
    ms = jnp.sum(x * x, axis=-1, keepdims=True) * (1.0 / n_valid)
    y = x * lax.rsqrt(ms + EPS) * gain
    if rmat is not None:
        rot = jnp.dot(y.astype(BF16), rmat, preferred_element_type=F32)
        y = y * cos + rot * sin
    return y * scale


def _gqa_prep_kernel(q_ref, k_ref, cos_ref, sin_ref, r_ref, qg_ref, kg_ref, qo_ref, ko_ref):
    cos, sin, rmat = cos_ref[...], sin_ref[...], r_ref[...]
    scale = HEAD_DIM ** -0.5
    for h in range(GQA_HEADS):
        x = q_ref[0, :, h * HEAD_DIM:(h + 1) * HEAD_DIM].astype(F32)
        qo_ref[0, :, h * HEAD_DIM:(h + 1) * HEAD_DIM] = _head_norm_rope(
            x, qg_ref[...], cos, sin, rmat, HEAD_DIM, scale).astype(BF16)
    for h in range(GQA_KV_HEADS):
        x = k_ref[0, :, h * HEAD_DIM:(h + 1) * HEAD_DIM].astype(F32)
        ko_ref[0, :, h * HEAD_DIM:(h + 1) * HEAD_DIM] = _head_norm_rope(
            x, kg_ref[...], cos, sin, rmat, HEAD_DIM, 1.0).astype(BF16)


def _gqa_prep(proj, cos, sin, rmat, q_gain, k_gain):
    b, lt, _ = proj.shape
    tm = _tile(lt, 640, 256)
    kw = GQA_KV_HEADS * HEAD_DIM
    return pl.pallas_call(
        _gqa_prep_kernel,
        grid=(b, lt // tm),
        in_specs=[pl.BlockSpec((1, tm, GQA_OUT), lambda bi, i: (bi, i, OFF_GQ // GQA_OUT)),
                  pl.BlockSpec((1, tm, kw), lambda bi, i: (bi, i, OFF_GK // kw)),
                  pl.BlockSpec((tm, HEAD_DIM), lambda bi, i: (i, 0)),
                  pl.BlockSpec((tm, HEAD_DIM), lambda bi, i: (i, 0)),
                  pl.BlockSpec((HEAD_DIM, HEAD_DIM), lambda bi, i: (0, 0)),
                  pl.BlockSpec((1, HEAD_DIM), lambda bi, i: (0, 0)),
                  pl.BlockSpec((1, HEAD_DIM), lambda bi, i: (0, 0))],
        out_specs=[pl.BlockSpec((1, tm, GQA_OUT), lambda bi, i: (bi, i, 0)),
                   pl.BlockSpec((1, tm, kw), lambda bi, i: (bi, i, 0))],
        out_shape=[jax.ShapeDtypeStruct((b, lt, GQA_OUT), BF16),
                   jax.ShapeDtypeStruct((b, lt, kw), BF16)],
        compiler_params=_cp(("parallel", "parallel")),
        name="gqa_prep",
    )(proj, proj, cos, sin, rmat, q_gain.reshape(1, -1), k_gain.reshape(1, -1))


def _mla_prep_kernel(cq_ref, ckv_ref, kr_ref, cos_ref, sin_ref, r_ref,
                     qn_g_ref, kvn_g_ref, wqn_ref, wqr_ref, wuk_ref, wuv_ref,
                     qnn_g_ref, qrn_g_ref, knn_g_ref, krn_g_ref,
                     q_ref, kn_ref, kr_out_ref, v_ref):
    cos, sin, rmat = cos_ref[...], sin_ref[...], r_ref[...]
    scale = (MLA_NOPE + MLA_ROPE) ** -0.5
    cq = cq_ref[0].astype(F32)
    cqn = (cq * lax.rsqrt(jnp.mean(cq * cq, axis=-1, keepdims=True) + EPS) * qn_g_ref[...]).astype(BF16)
    q_nope = jnp.dot(cqn, wqn_ref[...], preferred_element_type=F32)
    q_rope = jnp.dot(cqn, wqr_ref[...], preferred_element_type=F32)
    for h in range(MLA_HEADS):
        sl = slice(h * HEAD_DIM, (h + 1) * HEAD_DIM)
        qn = _head_norm_rope(q_nope[:, sl], qnn_g_ref[...], None, None, None, MLA_NOPE, scale)
        qr = _head_norm_rope(q_rope[:, sl], qrn_g_ref[...], cos, sin, rmat, MLA_ROPE, scale)
        q_ref[0, :, 2 * h * HEAD_DIM:(2 * h + 1) * HEAD_DIM] = qn.astype(BF16)
        q_ref[0, :, (2 * h + 1) * HEAD_DIM:(2 * h + 2) * HEAD_DIM] = qr.astype(BF16)
    ckv = ckv_ref[0].astype(F32)
    lat = (ckv * lax.rsqrt(jnp.mean(ckv * ckv, axis=-1, keepdims=True) + EPS) * kvn_g_ref[...]).astype(BF16)
    k_nope = jnp.dot(lat, wuk_ref[...], preferred_element_type=F32)
    v_ref[0] = jnp.dot(lat, wuv_ref[...], preferred_element_type=F32).astype(BF16)
    for h in range(MLA_HEADS):
        sl = slice(h * HEAD_DIM, (h + 1) * HEAD_DIM)
        kn_ref[0, :, sl] = _head_norm_rope(k_nope[:, sl], knn_g_ref[...], None, None, None, MLA_NOPE, 1.0).astype(BF16)
    kr = kr_ref[0].astype(F32)
    kr_out_ref[0] = _head_norm_rope(kr, krn_g_ref[...], cos, sin, rmat, MLA_ROPE, 1.0).astype(BF16)


def _mla_prep(proj, cos, sin, rmat, p):
    b, lt, _ = proj.shape
    tm = _tile(lt, 640, 256)
    hd = MLA_HEADS * HEAD_DIM
    full = lambda shape: pl.BlockSpec(shape, lambda bi, i: tuple(0 for _ in shape))
    return pl.pallas_call(
        _mla_prep_kernel,
        grid=(b, lt // tm),
        in_specs=[pl.BlockSpec((1, tm, MLA_Q_RANK), lambda bi, i: (bi, i, OFF_MQ // MLA_Q_RANK)),
                  pl.BlockSpec((1, tm, MLA_KV_RANK), lambda bi, i: (bi, i, OFF_MKV // MLA_KV_RANK)),
                  pl.BlockSpec((1, tm, LANE), lambda bi, i: (bi, i, OFF_MKR // LANE)),
                  pl.BlockSpec((tm, LANE), lambda bi, i: (i, 0)),
                  pl.BlockSpec((tm, LANE), lambda bi, i: (i, 0)),
                  full((LANE, LANE)),
                  full((1, MLA_Q_RANK)), full((1, MLA_KV_RANK)),
                  full((MLA_Q_RANK, hd)), full((MLA_Q_RANK, hd)),
                  full((MLA_KV_RANK, hd)), full((MLA_KV_RANK, hd)),
                  full((1, LANE)), full((1, LANE)), full((1, LANE)), full((1, LANE))],
        out_specs=[pl.BlockSpec((1, tm, 2 * hd), lambda bi, i: (bi, i, 0)),
                   pl.BlockSpec((1, tm, hd), lambda bi, i: (bi, i, 0)),
                   pl.BlockSpec((1, tm, LANE), lambda bi, i: (bi, i, 0)),
                   pl.BlockSpec((1, tm, hd), lambda bi, i: (bi, i, 0))],
        out_shape=[jax.ShapeDtypeStruct((b, lt, 2 * hd), BF16),
                   jax.ShapeDtypeStruct((b, lt, hd), BF16),
                   jax.ShapeDtypeStruct((b, lt, LANE), BF16),
                   jax.ShapeDtypeStruct((b, lt, hd), BF16)],
        compiler_params=_cp(("parallel", "parallel")),
        name="mla_prep",
    )(proj, proj, proj, cos, sin, rmat,
      p["mla_q_norm"].reshape(1, -1), p["mla_kv_norm"].reshape(1, -1),
      p["wq_nope"], p["wq_rope"], p["w_uk"], p["w_uv"],
      p["mla_qn_norm"].reshape(1, -1), p["qr_gain"], p["mla_kn_norm"].reshape(1, -1), p["kr_gain"])


def _flash_kernel(*refs, grp, dq, tq, has_kr):
    if has_kr:
        q_ref, k_ref, kr_ref, v_ref, o_ref, q_sc, m_sc, l_sc, acc_sc = refs
    else:
        q_ref, k_ref, v_ref, o_ref, q_sc, m_sc, l_sc, acc_sc = refs
        kr_ref = None
    kv = pl.program_id(3)

    @pl.when(kv == 0)
    def _():
        for g in range(grp):
            q_sc[g * tq:(g + 1) * tq, :] = q_ref[0, :, g * dq:(g + 1) * dq]
        m_sc[...] = jnp.full_like(m_sc, -jnp.inf)
        l_sc[...] = jnp.zeros_like(l_sc)
        acc_sc[...] = jnp.zeros_like(acc_sc)

    k = k_ref[0]
    if has_kr:
        k = jnp.concatenate([k, kr_ref[0]], axis=-1)
    s = lax.dot_general(q_sc[...], k, (((1,), (1,)), ((), ())), preferred_element_type=F32)
    m_prev = m_sc[...]
    m_new = jnp.maximum(m_prev, jnp.max(s, axis=-1, keepdims=True))
    alpha = jnp.exp(m_prev - m_new)
    p = jnp.exp(s - m_new)
    l_sc[...] = alpha * l_sc[...] + jnp.sum(p, axis=-1, keepdims=True)
    acc_sc[...] = alpha * acc_sc[...] + jnp.dot(p.astype(BF16), v_ref[0], preferred_element_type=F32)
    m_sc[...] = m_new

    @pl.when(kv == pl.num_programs(3) - 1)
    def _():
        o = acc_sc[...] / l_sc[...]
        for g in range(grp):
            o_ref[0, :, g * HEAD_DIM:(g + 1) * HEAD_DIM] = o[g * tq:(g + 1) * tq].astype(o_ref.dtype)


def _flash(q, k, kr, v, *, n_kv_heads, grp, dq, tq, tk, q_rows, q_off, kv_rows, kv_off, v_col0):
    b = q.shape[0]
    has_kr = kr is not None
    qo, ko = q_off // tq, kv_off // tk
    in_specs = [pl.BlockSpec((1, tq, grp * dq), lambda bi, h, i, j: (bi, i + qo, h)),
                pl.BlockSpec((1, tk, HEAD_DIM), lambda bi, h, i, j: (bi, j + ko, h))]
    args = [q, k]
    if has_kr:
        in_specs.append(pl.BlockSpec((1, tk, LANE), lambda bi, h, i, j: (bi, j + ko, 0)))
        args.append(kr)
    in_specs.append(pl.BlockSpec((1, tk, HEAD_DIM), lambda bi, h, i, j: (bi, j + ko, v_col0 + h)))
    args.append(v)
    return pl.pallas_call(
        functools.partial(_flash_kernel, grp=grp, dq=dq, tq=tq, has_kr=has_kr),
        grid=(b, n_kv_heads, q_rows // tq, kv_rows // tk),
        in_specs=in_specs,
        out_specs=pl.BlockSpec((1, tq, grp * HEAD_DIM), lambda bi, h, i, j: (bi, i, h)),
        out_shape=jax.ShapeDtypeStruct((b, q_rows, n_kv_heads * grp * HEAD_DIM), BF16),
        scratch_shapes=[pltpu.VMEM((grp * tq, dq), BF16),
                        pltpu.VMEM((grp * tq, 1), F32),
                        pltpu.VMEM((grp * tq, 1), F32),
                        pltpu.VMEM((grp * tq, HEAD_DIM), F32)],
        compiler_params=_cp(("parallel", "parallel", "parallel", "arbitrary")),
        name="flash_attn",
    )(*args)


def _attention(q, k, kr, v, *, n_kv_heads, grp, dq, n_lat, v_col0, tq_lat):
    lt = q.shape[1]
    n_ctx = lt - n_lat
    tk = _tile(lt, 1280, 256)
    tq = _tile(n_lat, tq_lat, 256)
    y_lat = _flash(q, k, kr, v, n_kv_heads=n_kv_heads, grp=grp, dq=dq, tq=tq, tk=tk,
                   q_rows=n_lat, q_off=0, kv_rows=lt, kv_off=0, v_col0=v_col0)
    y_ctx = _flash(q, k, kr, v, n_kv_heads=n_kv_heads, grp=grp, dq=dq, tq=n_ctx, tk=n_ctx,
                   q_rows=n_ctx, q_off=n_lat, kv_rows=n_ctx, kv_off=n_lat, v_col0=v_col0)
    return jnp.concatenate([y_lat, y_ctx], axis=1)


def _conv_kernel(prev_ref, cur_ref, next_ref, w_ref, b_ref, o_ref, *, tr, n_lat, lt):
    i = pl.program_id(1)
    start = i * tr
    has_prev = jnp.logical_and(start != 0, start != n_lat).astype(F32)
    has_next = jnp.logical_and(start + tr != n_lat, start + tr != lt).astype(F32)
    ext = jnp.concatenate([prev_ref[0].astype(F32) * has_prev, cur_ref[0].astype(F32),
                           next_ref[0].astype(F32) * has_next], axis=0)
    w = w_ref[...]
    acc = jnp.zeros((tr, cur_ref.shape[2]), F32) + b_ref[...]
    half = MB_CONV // 2
    for kk in range(MB_CONV):
        acc = acc + ext[8 + kk - half:8 + kk - half + tr, :] * w[kk:kk + 1, :]
    o_ref[0] = _silu(acc).astype(o_ref.dtype)


def _mamba_conv(proj, conv_w, conv_b, n_lat):
    b, lt, _ = proj.shape
    tr = _tile(n_lat, 256)
    cw = 512
    cb0 = OFF_XBC // cw
    r8 = tr // 8
    nblk8 = lt // 8
    w8 = jnp.zeros((8, MB_XBC), F32).at[:MB_CONV].set(conv_w.T.astype(F32))
    return pl.pallas_call(
        functools.partial(_conv_kernel, tr=tr, n_lat=n_lat, lt=lt),
        grid=(b, lt // tr, MB_XBC // cw),
        in_specs=[pl.BlockSpec((1, 8, cw), lambda bi, i, c: (bi, jnp.maximum(i * r8 - 1, 0), cb0 + c)),
                  pl.BlockSpec((1, tr, cw), lambda bi, i, c: (bi, i, cb0 + c)),
                  pl.BlockSpec((1, 8, cw), lambda bi, i, c: (bi, jnp.minimum((i + 1) * r8, nblk8 - 1), cb0 + c)),
                  pl.BlockSpec((8, cw), lambda bi, i, c: (0, c)),
                  pl.BlockSpec((1, cw), lambda bi, i, c: (0, c))],
        out_specs=pl.BlockSpec((1, tr, cw), lambda bi, i, c: (bi, i, c)),
        out_shape=jax.ShapeDtypeStruct((b, lt, MB_XBC), BF16),
        compiler_params=_cp(("parallel", "parallel", "parallel")),
        name="mamba_conv",
    )(proj, proj, proj, w8, conv_b.reshape(1, -1).astype(F32))


def _softplus(x):
    return jnp.maximum(x, 0.0) + jnp.log1p(jnp.exp(-jnp.abs(x)))


def _ssd_kernel(xs_ref, bm_ref, cm_ref, dt_ref, bias_ref, a_ref, y_ref, h_sc):
    t = SSD_CHUNK
    d = pl.program_id(2)

    @pl.when(pl.program_id(3) == 0)
    def _():
        h_sc[...] = jnp.zeros_like(h_sc)

    dt = _softplus(dt_ref[0].astype(F32) + bias_ref[0])
    a = dt * a_ref[0]
    row = lax.broadcasted_iota(jnp.int32, (t, t), 0)
    col = lax.broadcasted_iota(jnp.int32, (t, t), 1)
    sgn = 1 - 2 * d
    mask = (row - col) * sgn >= 0
    cum = jnp.dot(mask.astype(F32), a, precision=HIGHEST, preferred_element_type=F32)
    total = jnp.sum(a, axis=0, keepdims=True)
    cum_t = cum.T
    width = MB_HPG * MB_HEAD_DIM
    expand = (lax.broadcasted_iota(jnp.int32, (LANE, width), 1) // MB_HEAD_DIM
              == lax.broadcasted_iota(jnp.int32, (LANE, width), 0)).astype(F32)

    def ex(v):
        return jnp.dot(v, expand, precision=HIGHEST, preferred_element_type=F32)

    xdt = xs_ref[0].astype(F32) * ex(dt)
    bm = bm_ref[0]
    cm = cm_ref[0]
    scores = lax.dot_general(cm, bm, (((1,), (1,)), ((), ())), preferred_element_type=F32)
    h = h_sc[...]
    y_off = jnp.dot(cm, h.astype(BF16), preferred_element_type=F32) * ex(jnp.exp(cum))
    xw = (xdt * ex(jnp.exp(total - cum))).astype(BF16)
    bm_t = bm.astype(F32).T.astype(BF16)
    chunk_decay = ex(jnp.broadcast_to(jnp.exp(total), (8, LANE)))[0:1]
    h_sc[...] = h * chunk_decay + jnp.dot(bm_t, xw, preferred_element_type=F32)

    xdt_b = xdt.astype(BF16)
    lane = lax.broadcasted_iota(jnp.int32, (t, LANE), 1)
    parts = []
    for m in range(MB_HPG // 2):
        pair = xdt_b[:, m * LANE:(m + 1) * LANE]
        acc = jnp.zeros((t, LANE), F32)
        for q in range(2):
            j = 2 * m + q
            diff = cum[:, j:j + 1] - cum_t[j:j + 1, :]
            decay = jnp.where(mask, jnp.exp(jnp.minimum(diff, 0.0)), 0.0)
            keep = (lane < MB_HEAD_DIM) if q == 0 else (lane >= MB_HEAD_DIM)
            acc = acc + jnp.dot((scores * decay).astype(BF16), jnp.where(keep, pair, jnp.zeros_like(pair)),
                                preferred_element_type=F32)
        parts.append(acc)
    y_ref[0, 0] = jnp.concatenate(parts, axis=1) + y_off


def _ssd(xbc, proj, dt_bias4, a4, n_lat):
    b, lt, _ = xbc.shape
    t = SSD_CHUNK
    n_lat_c = n_lat // t
    n_ctx_c = (lt - n_lat) // t
    nc = n_lat_c + n_ctx_c
    width = MB_HPG * MB_HEAD_DIM

    def chunk(d, i):
        fwd = jnp.where(i < n_ctx_c, n_lat_c + i, i - n_ctx_c)
        bwd = jnp.where(i < n_ctx_c, n_lat_c + n_ctx_c - 1 - i, n_lat_c - 1 - (i - n_ctx_c))
        return jnp.where(d == 0, fwd, bwd)

    return pl.pallas_call(
        _ssd_kernel,
        grid=(b, MB_GROUPS, 2, nc),
        in_specs=[pl.BlockSpec((1, t, width), lambda bi, g, d, i: (bi, chunk(d, i), g)),
                  pl.BlockSpec((1, t, MB_STATE), lambda bi, g, d, i: (bi, chunk(d, i), MB_INNER // MB_STATE + g)),
                  pl.BlockSpec((1, t, MB_STATE), lambda bi, g, d, i: (bi, chunk(d, i), MB_INNER // MB_STATE + MB_GROUPS + g)),
                  pl.BlockSpec((1, t, LANE), lambda bi, g, d, i: (bi, chunk(d, i), OFF_DT // LANE + d * MB_GROUPS + g)),
                  pl.BlockSpec((1, 1, LANE), lambda bi, g, d, i: (d * MB_GROUPS + g, 0, 0)),
                  pl.BlockSpec((1, 1, LANE), lambda bi, g, d, i: (d * MB_GROUPS + g, 0, 0))],
        out_specs=pl.BlockSpec((1, 1, t, width), lambda bi, g, d, i: (bi, d, chunk(d, i), g)),
        out_shape=jax.ShapeDtypeStruct((b, 2, lt, MB_INNER), F32),
        scratch_shapes=[pltpu.VMEM((MB_STATE, width), F32)],
        compiler_params=_cp(("parallel", "parallel", "parallel", "arbitrary")),
        name="ssd_scan",
    )(xbc, xbc, xbc, proj, dt_bias4, a4)


def _mamba_finish_kernel(yf_ref, yb_ref, xs_ref, z_ref, d_ref, g_ref, o_ref):
    y = yf_ref[0, 0] + yb_ref[0, 0] + xs_ref[0].astype(F32) * d_ref[...]
    y = y * _silu(z_ref[0].astype(F32))
    o_ref[0] = (y * lax.rsqrt(jnp.mean(y * y, axis=-1, keepdims=True) + EPS) * g_ref[...]).astype(o_ref.dtype)


def _mamba_finish(y2, xbc, proj, d_vec, gain):
    b, lt, _ = xbc.shape
    tm = _tile(lt, 640, 256)
    w = MB_INNER
    return pl.pallas_call(
        _mamba_finish_kernel,
        grid=(b, lt // tm),
        in_specs=[pl.BlockSpec((1, 1, tm, w), lambda bi, i: (bi, 0, i, 0)),
                  pl.BlockSpec((1, 1, tm, w), lambda bi, i: (bi, 1, i, 0)),
                  pl.BlockSpec((1, tm, w), lambda bi, i: (bi, i, 0)),
                  pl.BlockSpec((1, tm, w), lambda bi, i: (bi, i, OFF_MZ // w)),
                  pl.BlockSpec((1, w), lambda bi, i: (0, 0)),
                  pl.BlockSpec((1, w), lambda bi, i: (0, 0))],
        out_specs=pl.BlockSpec((1, tm, w), lambda bi, i: (bi, i, 0)),
        out_shape=jax.ShapeDtypeStruct((b, lt, w), BF16),
        compiler_params=_cp(("parallel", "parallel")),
        name="mamba_finish",
    )(y2, y2, xbc, proj, d_vec, gain.reshape(1, -1))


def _s5_in_kernel(u_ref, w_ref, yi_ref, s_ref):
    r = jnp.dot(u_ref[0], w_ref[0], preferred_element_type=F32)
    k = S5_CHUNK * S5_GROUP
    yi_ref[0] = r[:, :k]
    s_ref[...] = r[:, k:]


def _s5_in(u_g, w_g):
    g, m, k = u_g.shape
    n = w_g.shape[2]
    return pl.pallas_call(
        _s5_in_kernel,
        grid=(g,),
        in_specs=[pl.BlockSpec((1, m, k), lambda gi: (gi, 0, 0)),
                  pl.BlockSpec((1, k, n), lambda gi: (gi, 0, 0))],
        out_specs=[pl.BlockSpec((1, m, k), lambda gi: (gi, 0, 0)),
                   pl.BlockSpec((m, n - k), lambda gi: (0, gi))],
        out_shape=[jax.ShapeDtypeStruct((g, m, k), F32),
                   jax.ShapeDtypeStruct((m, g * (n - k)), F32)],
        compiler_params=_cp(("parallel",)),
        name="s5_intra",
    )(u_g, w_g)


def _s5_scan_kernel(sf_ref, sb_ref, af_ref, ab_ref, hf_ref, hb_ref, st_sc, *, tc):
    @pl.when(pl.program_id(1) == 0)
    def _():
        st_sc[...] = jnp.zeros_like(st_sc)

    a1f, a2f = af_ref[0], af_ref[1]
    a1b, a2b = ab_ref[0], ab_ref[1]

    def body(c, carry):
        hf, hb = carry
        hf_ref[0, c] = hf
        hf = a1f * hf + a2f * pltpu.roll(hf, S5_STATE, 1) + sf_ref[0, c]
        cb = tc - 1 - c
        hb_ref[0, cb] = hb
        hb = a1b * hb + a2b * pltpu.roll(hb, S5_STATE, 1) + sb_ref[0, cb]
        return hf, hb

    hf, hb = lax.fori_loop(0, tc, body, (st_sc[0], st_sc[1]))
    st_sc[0] = hf
    st_sc[1] = hb


def _s5_scan(s4, a_f, a_b, n_lat_chunks):
    b, nc, g, _ = s4.shape
    tc = 16
    nlt = n_lat_chunks // tc
    nt = nc // tc
    w = 2 * S5_STATE

    def tile_f(i):
        return jnp.where(i < nt - nlt, nlt + i, i - (nt - nlt))

    def tile_b(i):
        return jnp.where(i < nt - nlt, nt - 1 - i, nlt - 1 - (i - (nt - nlt)))

    return pl.pallas_call(
        functools.partial(_s5_scan_kernel, tc=tc),
        grid=(b, nt),
        in_specs=[pl.BlockSpec((1, tc, g, w), lambda bi, i: (bi, tile_f(i), 0, 0)),
                  pl.BlockSpec((1, tc, g, w), lambda bi, i: (bi, tile_b(i), 0, 1)),
                  pl.BlockSpec((2, g, w), lambda bi, i: (0, 0, 0)),
                  pl.BlockSpec((2, g, w), lambda bi, i: (0, 0, 0))],
        out_specs=[pl.BlockSpec((1, tc, g, w), lambda bi, i: (bi, tile_f(i), 0, 0)),
                   pl.BlockSpec((1, tc, g, w), lambda bi, i: (bi, tile_b(i), 0, 0))],
        out_shape=[jax.ShapeDtypeStruct((b, nc, g, w), F32),
                   jax.ShapeDtypeStruct((b, nc, g, w), F32)],
        scratch_shapes=[pltpu.VMEM((2, g, w), F32)],
        compiler_params=_cp(("parallel", "arbitrary")),
        name="s5_state_scan",
    )(s4, s4, a_f, a_b)


def _s5_out_kernel(yi_ref, hf_ref, hb_ref, qf_ref, qb_ref, o_ref):
    o_ref[0] = (yi_ref[0]
                + jnp.dot(hf_ref[...].astype(BF16), qf_ref[0], preferred_element_type=F32)
                + jnp.dot(hb_ref[...].astype(BF16), qb_ref[0], preferred_element_type=F32))


def _s5_out(yi, hf2, hb2, qf, qb):
    g, m, k = yi.shape
    w = 2 * S5_STATE
    return pl.pallas_call(
        _s5_out_kernel,
        grid=(g,),
        in_specs=[pl.BlockSpec((1, m, k), lambda gi: (gi, 0, 0)),
                  pl.BlockSpec((m, w), lambda gi: (0, gi)),
                  pl.BlockSpec((m, w), lambda gi: (0, gi)),
                  pl.BlockSpec((1, w, k), lambda gi: (gi, 0, 0)),
                  pl.BlockSpec((1, w, k), lambda gi: (gi, 0, 0))],
        out_specs=pl.BlockSpec((1, m, k), lambda gi: (gi, 0, 0)),
        out_shape=jax.ShapeDtypeStruct((g, m, k), F32),
        compiler_params=_cp(("parallel",)),
        name="s5_readout",
    )(yi, hf2, hb2, qf, qb)


def _gelu_tanh(x):
    return 0.5 * x * (1.0 + jnp.tanh(math.sqrt(2.0 / math.pi) * (x + 0.044715 * (x * x * x))))


def _s5_finish_kernel(y_ref, u_ref, d_ref, w_ref, o_ref):
    y = y_ref[0] + d_ref[...] * u_ref[0].astype(F32)
    g = _gelu_tanh(y)
    o_ref[0] = (g * _sigmoid(jnp.dot(g.astype(BF16), w_ref[...], preferred_element_type=F32))).astype(o_ref.dtype)


def _s5_finish(y, proj, d_vec, w_glu):
    b, lt, w = y.shape
    tm = _tile(lt, 640, 256)
    return pl.pallas_call(
        _s5_finish_kernel,
        grid=(b, lt // tm),
        in_specs=[pl.BlockSpec((1, tm, w), lambda bi, i: (bi, i, 0)),
                  pl.BlockSpec((1, tm, w), lambda bi, i: (bi, i, OFF_S5 // w)),
                  pl.BlockSpec((1, w), lambda bi, i: (0, 0)),
                  pl.BlockSpec((w, w), lambda bi, i: (0, 0))],
        out_specs=pl.BlockSpec((1, tm, w), lambda bi, i: (bi, i, 0)),
        out_shape=jax.ShapeDtypeStruct((b, lt, w), BF16),
        compiler_params=_cp(("parallel", "parallel")),
        name="s5_finish",
    )(y, proj, d_vec.reshape(1, -1), w_glu)


def _s5_operators(p):
    t = S5_CHUNK
    lam = lax.complex(p["s5_lam_re"].astype(F32), p["s5_lam_im"].astype(F32))
    step = jnp.exp(p["s5_log_step"].astype(F32))[..., None]
    lam_bar = jnp.exp(lam * step)
    b_bar = ((lam_bar - 1.0) / lam)[..., None] * lax.complex(p["s5_b_re"].astype(F32), p["s5_b_im"].astype(F32))
    c_mat = lax.complex(p["s5_c_re"].astype(F32), p["s5_c_im"].astype(F32))
    ks = jnp.arange(t + 1, dtype=F32)
    pw = jnp.exp((lam * step)[:, :, None, :] * ks[None, None, :, None])
    kern = jnp.real(jnp.einsum("dgcn,dgkn,dgni->dgkci", c_mat, pw[:, :, :t], b_bar))
    s_idx = jnp.arange(t)[:, None]
    t_idx = jnp.arange(t)[None, :]
    lag_f = jnp.clip(t_idx - s_idx, 0, t - 1)
    lag_b = jnp.clip(s_idx - t_idx, 0, t - 1)
    kf = jnp.where((t_idx >= s_idx)[None, :, :, None, None], kern[0][:, lag_f], 0.0)
    kb = jnp.where((s_idx >= t_idx)[None, :, :, None, None], kern[1][:, lag_b], 0.0)
    w_intra = jnp.transpose(kf + kb, (0, 1, 4, 2, 3)).reshape(S5_GROUPS, t * S5_GROUP, t * S5_GROUP)
    inj_f = pw[0][:, t - 1 - jnp.arange(t), :, None] * b_bar[0][:, None]
    inj_b = pw[1][:, jnp.arange(t), :, None] * b_bar[1][:, None]

    def inj_mat(z):
        z = jnp.transpose(z, (0, 1, 3, 2)).reshape(S5_GROUPS, t * S5_GROUP, S5_STATE)
        return jnp.concatenate([jnp.real(z), jnp.imag(z)], axis=-1)

    w_in = jnp.concatenate([w_intra, inj_mat(inj_f), inj_mat(inj_b)], axis=-1).astype(BF16)
    m_f = c_mat[0][:, None, :, :] * pw[0][:, 1 + jnp.arange(t), None, :]
    m_b = c_mat[1][:, None, :, :] * pw[1][:, t - jnp.arange(t), None, :]

    def read_mat(z):
        z = jnp.transpose(z.reshape(S5_GROUPS, t * S5_GROUP, S5_STATE), (0, 2, 1))
        return jnp.concatenate([jnp.real(z), -jnp.imag(z)], axis=1).astype(BF16)

    def trans(z):
        re, im = jnp.real(z), jnp.imag(z)
        return jnp.stack([jnp.concatenate([re, re], -1), jnp.concatenate([-im, im], -1)])

    return dict(w_in=w_in, q_f=read_mat(m_f), q_b=read_mat(m_b), a_f=trans(pw[0][:, t]), a_b=trans(pw[1][:, t]))


def _s5_mixer(proj, ops, d_vec, w_glu, n_lat):
    b, lt, _ = proj.shape
    t = S5_CHUNK
    nc = lt // t
    u = proj[:, :, OFF_S5:OFF_S5 + S5_WIDTH]
    u_g = jnp.transpose(u.reshape(b, nc, t, S5_GROUPS, S5_GROUP), (3, 0, 1, 2, 4)).reshape(S5_GROUPS, b * nc, t * S5_GROUP)
    yi, s = _s5_in(u_g, ops["w_in"])
    s4 = s.reshape(b, nc, S5_GROUPS, 4 * S5_STATE)
    hf, hb = _s5_scan(s4, ops["a_f"], ops["a_b"], n_lat // t)
    y_g = _s5_out(yi, hf.reshape(b * nc, -1), hb.reshape(b * nc, -1), ops["q_f"], ops["q_b"])
    y = jnp.transpose(y_g.reshape(S5_GROUPS, b, nc, t, S5_GROUP), (1, 2, 3, 0, 4)).reshape(b, lt, S5_WIDTH)
    return _s5_finish(y, proj, d_vec, w_glu)


def _merge_kernel(ya_ref, yb_ref, yc_ref, yd_ref, g0_ref, g1_ref, g2_ref, g3_ref,
                  wa_ref, wb_ref, wc_ref, wd_ref, o_ref):
    acc = None
    for y_ref, g_ref, w_ref in ((ya_ref, g0_ref, wa_ref), (yb_ref, g1_ref, wb_ref),
                                (yc_ref, g2_ref, wc_ref), (yd_ref, g3_ref, wd_ref)):
        term = _sigmoid(g_ref[0].astype(F32)) * jnp.dot(y_ref[0], w_ref[...], preferred_element_type=F32)
        acc = term if acc is None else acc + term
    o_ref[0] = acc.astype(o_ref.dtype)


def _merge(ys, proj, ws):
    b, lt, _ = proj.shape
    d = D_MODEL
    tm = _tile(lt, 640, 256)
    tn = 512
    nb = d // tn
    in_specs = [pl.BlockSpec((1, tm, y.shape[2]), lambda bi, i, j: (bi, i, 0)) for y in ys]
    in_specs += [pl.BlockSpec((1, tm, tn), functools.partial(lambda bi, i, j, br: (bi, i, br * nb + j), br=br))
                 for br in range(4)]
    in_specs += [pl.BlockSpec((w.shape[0], tn), lambda bi, i, j: (0, j)) for w in ws]
    return pl.pallas_call(
        _merge_kernel,
        grid=(b, lt // tm, nb),
        in_specs=in_specs,
        out_specs=pl.BlockSpec((1, tm, tn), lambda bi, i, j: (bi, i, j)),
        out_shape=jax.ShapeDtypeStruct((b, lt, d), BF16),
        compiler_params=_cp(("parallel", "parallel", "parallel")),
        name="branch_merge",
    )(*ys, proj, proj, proj, proj, *ws)


def _row_gate(ml, mc, row_tile, tm, n_lat, row):
    rows = row_tile * tm + lax.broadcasted_iota(jnp.int32, (tm, 1), 0)
    return jnp.where(rows >= n_lat, mc[row:row + 1], ml[row:row + 1])


def _outproj_kernel(m_ref, w_ref, x_ref, ml_ref, mc_ref, o_ref, *, tm, n_lat):
    gate = _row_gate(ml_ref[0], mc_ref[0], pl.program_id(1), tm, n_lat, 2)
    o_ref[0] = x_ref[0] + gate * jnp.dot(m_ref[0], w_ref[...], preferred_element_type=F32)


def _outproj(m, w_out, x, mods, n_lat):
    b, lt, d = x.shape
    tm = _tile(lt, 1280, 256)
    tn = 512
    return pl.pallas_call(
        functools.partial(_outproj_kernel, tm=tm, n_lat=n_lat),
        grid=(b, lt // tm, d // tn),
        in_specs=[pl.BlockSpec((1, tm, d), lambda bi, i, j: (bi, i, 0)),
                  pl.BlockSpec((d, tn), lambda bi, i, j: (0, j)),
                  pl.BlockSpec((1, tm, tn), lambda bi, i, j: (bi, i, j)),
                  pl.BlockSpec((1, 8, tn), lambda bi, i, j: (bi, 0, j)),
                  pl.BlockSpec((1, 8, tn), lambda bi, i, j: (b, 0, j))],
        out_specs=pl.BlockSpec((1, tm, tn), lambda bi, i, j: (bi, i, j)),
        out_shape=jax.ShapeDtypeStruct((b, lt, d), F32),
        compiler_params=_cp(("parallel", "parallel", "parallel")),
        name="out_proj",
    )(m, w_out, x, mods, mods)


def _prenorm_router_kernel(x_ref, g_ref, ml_ref, mc_ref, wr_ref, br_ref, h_ref, lg_ref, *, tm, n_lat):
    def chunk(r, carry):
        rs = pl.multiple_of(r * ROW_CHUNK, ROW_CHUNK)
        h = _norm_mod(x_ref[0, pl.ds(rs, ROW_CHUNK), :], g_ref[...], ml_ref[0], mc_ref[0],
                      pl.program_id(1) * tm + rs, n_lat, 3).astype(BF16)
        h_ref[0, pl.ds(rs, ROW_CHUNK), :] = h
        lg_ref[0, pl.ds(rs, ROW_CHUNK), :] = jnp.dot(h, wr_ref[...], preferred_element_type=F32) + br_ref[...]
        return carry

    lax.fori_loop(0, tm // ROW_CHUNK, chunk, 0)


def _prenorm_router(x, gain, mods, w_r, b_r, n_lat):
    b, lt, d = x.shape
    tm = _tile(lt, 640, 256)
    return pl.pallas_call(
        functools.partial(_prenorm_router_kernel, tm=tm, n_lat=n_lat),
        grid=(b, lt // tm),
        in_specs=[pl.BlockSpec((1, tm, d), lambda bi, i: (bi, i, 0)),
                  pl.BlockSpec((1, d), lambda bi, i: (0, 0)),
                  pl.BlockSpec((1, 8, d), lambda bi, i: (bi, 0, 0)),
                  pl.BlockSpec((1, 8, d), lambda bi, i: (b, 0, 0)),
                  pl.BlockSpec((d, LANE), lambda bi, i: (0, 0)),
                  pl.BlockSpec((1, LANE), lambda bi, i: (0, 0))],
        out_specs=[pl.BlockSpec((1, tm, d), lambda bi, i: (bi, i, 0)),
                   pl.BlockSpec((1, tm, LANE), lambda bi, i: (bi, i, 0))],
        out_shape=[jax.ShapeDtypeStruct((b, lt, d), BF16),
                   jax.ShapeDtypeStruct((b, lt, LANE), F32)],
        compiler_params=_cp(("parallel", "parallel")),
        name="moe_prenorm_router",
    )(x, gain.reshape(1, d), mods, mods, w_r, b_r)


def _moe_kernel(te_ref, nu_ref, x_ref, rw_ref, w1_ref, w3_ref, w2_ref, o_ref):
    i = pl.program_id(0)

    @pl.when(i < nu_ref[0])
    def _():
        x = x_ref[...]
        hid = _silu(jnp.dot(x, w1_ref[0], preferred_element_type=F32)) * jnp.dot(x, w3_ref[0], preferred_element_type=F32)
        y = jnp.dot(hid.astype(BF16), w2_ref[0], preferred_element_type=F32)
        o_ref[...] = (y * rw_ref[...]).astype(o_ref.dtype)

    @pl.when(i >= nu_ref[0])
    def _():
        o_ref[...] = jnp.zeros_like(o_ref)


def _moe_ffn(xs, rw, tile_expert, n_used, w1, w3, w2):
    rp, d = xs.shape
    tile = MOE_TILE
    ff = w1.shape[2]
    grid_spec = pltpu.PrefetchScalarGridSpec(
        num_scalar_prefetch=2,
        grid=(rp // tile,),
        in_specs=[pl.BlockSpec((tile, d), lambda i, te, nu: (i, 0)),
                  pl.BlockSpec((tile, 1), lambda i, te, nu: (i, 0)),
                  pl.BlockSpec((1, d, ff), lambda i, te, nu: (te[i], 0, 0)),
                  pl.BlockSpec((1, d, ff), lambda i, te, nu: (te[i], 0, 0)),
                  pl.BlockSpec((1, ff, d), lambda i, te, nu: (te[i], 0, 0))],
        out_specs=pl.BlockSpec((tile, d), lambda i, te, nu: (i, 0)),
    )
    return pl.pallas_call(
        _moe_kernel,
        grid_spec=grid_spec,
        out_shape=jax.ShapeDtypeStruct((rp, d), BF16),
        compiler_params=_cp(("arbitrary",)),
        name="moe_experts",
    )(tile_expert, n_used, xs, rw, w1, w3, w2)


def _moe(h2, logits, w1, w3, w2):
    n, d = h2.shape
    tile = MOE_TILE
    g_prob = jax.nn.softmax(logits[:, :MOE_GROUPS], axis=-1)
    g_idx = jnp.argmax(g_prob, axis=-1)
    g_w = jnp.take_along_axis(g_prob, g_idx[:, None], axis=1)
    e_logits = logits[:, MOE_GROUPS:MOE_GROUPS + N_EXPERTS].reshape(n, MOE_GROUPS, MOE_PER_GROUP)
    e_logits = jnp.take_along_axis(e_logits, g_idx[:, None, None], axis=1)[:, 0]
    e_w, e_idx = lax.top_k(jax.nn.softmax(e_logits, axis=-1), 2)
    w = g_w * e_w / jnp.sum(e_w, axis=-1, keepdims=True)
    expert = (g_idx[:, None] * MOE_PER_GROUP + e_idx).astype(jnp.int32)

    flat_e = expert.reshape(-1)
    order = jnp.argsort(flat_e, stable=True)
    sorted_e = flat_e[order]
    counts = jnp.sum(jax.nn.one_hot(flat_e, N_EXPERTS, dtype=jnp.int32), axis=0)
    padded = ((counts + tile - 1) // tile) * tile
    pad_end = jnp.cumsum(padded)
    pad_start = pad_end - padded
    cnt_start = jnp.cumsum(counts) - counts
    rank = jnp.arange(2 * n, dtype=jnp.int32) - cnt_start[sorted_e]
    dest_sorted = (pad_start[sorted_e] + rank).astype(jnp.int32)
    rp = ((2 * n + tile - 1) // tile) * tile + N_EXPERTS * tile
    src_token = jnp.zeros((rp,), jnp.int32).at[dest_sorted].set((order // 2).astype(jnp.int32))
    row_w = jnp.zeros((rp,), F32).at[dest_sorted].set(w.reshape(-1)[order])
    dest = jnp.zeros((2 * n,), jnp.int32).at[order].set(dest_sorted).reshape(n, 2)
    tile_start = jnp.arange(rp // tile, dtype=jnp.int32) * tile
    tile_expert = jnp.minimum(jnp.searchsorted(pad_end, tile_start, side="right"), N_EXPERTS - 1).astype(jnp.int32)
    n_used = (pad_end[-1] // tile).astype(jnp.int32).reshape(1)

    xs = jnp.take(h2, src_token, axis=0)
    ys = _moe_ffn(xs, row_w[:, None], tile_expert, n_used, w1, w3, w2)
    return jnp.take(ys, dest[:, 0], axis=0).astype(F32) + jnp.take(ys, dest[:, 1], axis=0).astype(F32)


def _pack_w_in(w_in):
    depth, d, _ = w_in.shape
    cuts = {}
    off = 0
    for name, width in (("s5", S5_WIDTH), ("gq", GQA_OUT), ("gk", GQA_KV_HEADS * HEAD_DIM), ("gv", GQA_KV_HEADS * HEAD_DIM),
                        ("mq", MLA_Q_RANK), ("mkv", MLA_KV_RANK), ("mkr", MLA_ROPE), ("mz", MB_INNER), ("xbc", MB_XBC),
                        ("dt", 2 * MB_HEADS), ("gate", 4 * D_MODEL)):
        cuts[name] = w_in[:, :, off:off + width]
        off += width
    zeros = lambda n: jnp.zeros((depth, d, n), w_in.dtype)
    dt_blocks = []
    for blk in range(2 * MB_GROUPS):
        dt_blocks += [cuts["dt"][:, :, blk * MB_HPG:(blk + 1) * MB_HPG], zeros(LANE - MB_HPG)]
    packed = jnp.concatenate(
        [cuts["gate"], cuts["gq"], cuts["mz"], cuts["xbc"], cuts["mq"], cuts["s5"], cuts["gk"], cuts["gv"],
         cuts["mkv"], cuts["mkr"], zeros(LANE - MLA_ROPE)] + dt_blocks + [zeros(NW - OFF_DT - 4 * LANE)], axis=-1)
    return packed.astype(BF16)


def _rope_table(n_lat, n_ctx, dim, width):
    rows = n_lat // GRID_W
    row = jnp.repeat(jnp.arange(rows, dtype=F32), GRID_W)
    col = jnp.tile(jnp.arange(GRID_W, dtype=F32), rows)
    quarter = dim // 4
    inv_freq = ROPE_THETA ** (-jnp.arange(quarter, dtype=F32) / quarter)
    ang_r = row[:, None] * inv_freq
    ang_c = col[:, None] * inv_freq
    ang = jnp.concatenate([ang_r, ang_r, ang_c, ang_c], axis=-1)
    cos = jnp.concatenate([jnp.cos(ang), jnp.ones((n_lat, width - dim), F32)], axis=-1)
    sin = jnp.concatenate([jnp.sin(ang), jnp.zeros((n_lat, width - dim), F32)], axis=-1)
    cos = jnp.concatenate([cos, jnp.ones((n_ctx, width), F32)], axis=0)
    sin = jnp.concatenate([sin, jnp.zeros((n_ctx, width), F32)], axis=0)
    return cos, sin


def _rot_matrix(dim, width):
    quarter = dim // 4
    idx = jnp.arange(width)
    in_first = (idx % (2 * quarter)) < quarter
    valid = idx < dim
    src_for_first = idx + quarter
    src_for_second = idx - quarter
    rows = jnp.arange(width)[:, None]
    r = jnp.where(in_first[None, :] & (rows == src_for_first[None, :]), -1.0, 0.0)
    r = r + jnp.where((~in_first)[None, :] & (rows == src_for_second[None, :]), 1.0, 0.0)
    return (r * valid[None, :]).astype(BF16)


def _pad_lanes(v, width):
    return jnp.concatenate([v.astype(F32), jnp.zeros((width - v.shape[0],), F32)]).reshape(1, width)


def kernel(x, c, ctx, c_ctx, norm1, norm2, w_ada, b_ada, w_in, s5_lam_re, s5_lam_im, s5_log_step, s5_b_re, s5_b_im, s5_c_re, s5_c_im, s5_d, s5_w_glu, gqa_q_norm, gqa_k_norm, mla_q_norm, mla_kv_norm, mla_w_uq, mla_w_uk, mla_w_uv, mla_qn_norm, mla_kn_norm, mla_qr_norm, mla_kr_norm, mb_conv_w, mb_conv_b, mb_dt_bias, mb_a_log, mb_d, mb_norm, w_br_s5, w_br_gqa, w_br_mla, w_br_mb, w_out, moe_w_group, moe_b_group, moe_w_expert, moe_b_expert, moe_w1, moe_w3, moe_w2):
    b, n_lat, d = x.shape
    n_ctx = ctx.shape[1]
    lt = n_lat + n_ctx
    depth = w_in.shape[0]

    cvec = jnp.zeros((8, d), F32).at[:b].set(c.astype(F32)).at[b].set(c_ctx.astype(F32))
    mod_all = _ada(cvec, w_ada, b_ada).reshape(depth, 8, N_MOD, d)[:, :b + 1]
    mod_all = jnp.concatenate([mod_all, jnp.zeros((depth, b + 1, 8 - N_MOD, d), F32)], axis=2)

    w_in_p = _pack_w_in(w_in)
    cos_g, sin_g = _rope_table(n_lat, n_ctx, HEAD_DIM, HEAD_DIM)
    cos_m, sin_m = _rope_table(n_lat, n_ctx, MLA_ROPE, LANE)
    rot_g = _rot_matrix(HEAD_DIM, HEAD_DIM)
    rot_m = _rot_matrix(MLA_ROPE, LANE)

    uq = mla_w_uq.reshape(depth, MLA_Q_RANK, MLA_HEADS, MLA_NOPE + MLA_ROPE)
    wq_nope = uq[..., :MLA_NOPE].reshape(depth, MLA_Q_RANK, MLA_HEADS * MLA_NOPE).astype(BF16)
    wq_rope = jnp.concatenate([uq[..., MLA_NOPE:], jnp.zeros(uq.shape[:3] + (LANE - MLA_ROPE,), uq.dtype)],
                              axis=-1).reshape(depth, MLA_Q_RANK, MLA_HEADS * LANE).astype(BF16)
    a_dec = -jnp.exp(mb_a_log.astype(F32))

    xs = jnp.concatenate([x.astype(F32), ctx.astype(F32)], axis=1)

    for i in range(depth):
        mods = mod_all[i]
        proj = _inproj(xs, norm1[i], mods, w_in_p[i], n_lat)

        s5p = dict(s5_lam_re=s5_lam_re[i], s5_lam_im=s5_lam_im[i], s5_log_step=s5_log_step[i], s5_b_re=s5_b_re[i],
                   s5_b_im=s5_b_im[i], s5_c_re=s5_c_re[i], s5_c_im=s5_c_im[i])
        ya = _s5_mixer(proj, _s5_operators(s5p), s5_d[i].astype(F32), s5_w_glu[i].astype(BF16), n_lat)

        qg, kg = _gqa_prep(proj, cos_g, sin_g, rot_g, gqa_q_norm[i].astype(F32), gqa_k_norm[i].astype(F32))
        yb = _attention(qg, kg, None, proj, n_kv_heads=GQA_KV_HEADS, grp=GQA_HEADS // GQA_KV_HEADS, dq=HEAD_DIM,
                        n_lat=n_lat, v_col0=OFF_GV // HEAD_DIM, tq_lat=256)

        mp = dict(mla_q_norm=mla_q_norm[i].astype(F32), mla_kv_norm=mla_kv_norm[i].astype(F32),
                  wq_nope=wq_nope[i], wq_rope=wq_rope[i], w_uk=mla_w_uk[i].astype(BF16), w_uv=mla_w_uv[i].astype(BF16),
                  mla_qn_norm=mla_qn_norm[i].astype(F32), mla_kn_norm=mla_kn_norm[i].astype(F32),
                  qr_gain=_pad_lanes(mla_qr_norm[i], LANE), kr_gain=_pad_lanes(mla_kr_norm[i], LANE))
        qm, kn, kr, vm = _mla_prep(proj, cos_m, sin_m, rot_m, mp)
        yc = _attention(qm, kn, kr, vm, n_kv_heads=MLA_HEADS, grp=1, dq=2 * HEAD_DIM,
                        n_lat=n_lat, v_col0=0, tq_lat=1024)

        xbc = _mamba_conv(proj, mb_conv_w[i], mb_conv_b[i], n_lat)
        dt_bias4 = jnp.concatenate([mb_dt_bias[i].astype(F32).reshape(2 * MB_GROUPS, MB_HPG),
                                    jnp.zeros((2 * MB_GROUPS, LANE - MB_HPG), F32)], axis=-1).reshape(2 * MB_GROUPS, 1, LANE)
        a4 = jnp.concatenate([a_dec[i].reshape(2 * MB_GROUPS, MB_HPG),
                              jnp.zeros((2 * MB_GROUPS, LANE - MB_HPG), F32)], axis=-1).reshape(2 * MB_GROUPS, 1, LANE)
        y2 = _ssd(xbc, proj, dt_bias4, a4, n_lat)
        d_vec = jnp.repeat(mb_d[i].astype(F32), MB_HEAD_DIM).reshape(1, MB_INNER)
        yd = _mamba_finish(y2, xbc, proj, d_vec, mb_norm[i].astype(F32))

        merged = _merge((ya, yb, yc, yd), proj,
                        (w_br_s5[i].astype(BF16), w_br_gqa[i].astype(BF16), w_br_mla[i].astype(BF16), w_br_mb[i].astype(BF16)))
        xs = _outproj(merged, w_out[i].astype(BF16), xs, mods, n_lat)

        w_r = jnp.concatenate([moe_w_group[i], moe_w_expert[i],
                               jnp.zeros((d, LANE - MOE_GROUPS - N_EXPERTS), F32)], axis=-1).astype(BF16)
        b_r = _pad_lanes(jnp.concatenate([moe_b_group[i], moe_b_expert[i]]), LANE)
        h2, logits = _prenorm_router(xs, norm2[i], mods, w_r, b_r, n_lat)
        ff = _moe(h2.reshape(b * lt, d), logits.reshape(b * lt, LANE),
                  moe_w1[i].astype(BF16), moe_w3[i].astype(BF16), moe_w2[i].astype(BF16)).reshape(b, lt, d)
        gate2 = jnp.concatenate([jnp.broadcast_to(mods[:b, 5][:, None, :], (b, n_lat, d)),
                                 jnp.broadcast_to(mods[b, 5][None, None, :], (b, n_ctx, d))], axis=1)
        xs = xs + gate2 * ff

    return xs[:, :n_lat].astype(x.dtype)
```

```python
import functools
import math

import jax
import jax.numpy as jnp
from jax import lax
from jax.experimental import pallas as pl
from jax.experimental.pallas import tpu as pltpu

F32 = jnp.float32
BF16 = jnp.bfloat16
HIGHEST = lax.Precision.HIGHEST

EPS = 1e-6
ROPE_THETA = 10000.0
GRID_W = 64
D_MODEL = 2048
N_MOD = 6

S5_GROUP = 16
S5_WIDTH = 768
S5_GROUPS = S5_WIDTH // S5_GROUP
S5_STATE = 64
S5_CHUNK = 16

GQA_HEADS = 8
GQA_KV_HEADS = 2
HEAD_DIM = 128
GQA_OUT = GQA_HEADS * HEAD_DIM

MLA_HEADS = 8
MLA_Q_RANK = 512
MLA_KV_RANK = 256
MLA_NOPE = 128
MLA_ROPE = 64
MLA_OUT = MLA_HEADS * HEAD_DIM

MB_HEADS = 16
MB_HEAD_DIM = 64
MB_INNER = MB_HEADS * MB_HEAD_DIM
MB_GROUPS = 2
MB_HPG = MB_HEADS // MB_GROUPS
MB_STATE = 128
MB_CONV = 5
MB_XBC = MB_INNER + 2 * MB_GROUPS * MB_STATE
SSD_CHUNK = 128

MOE_GROUPS = 4
MOE_PER_GROUP = 8
N_EXPERTS = MOE_GROUPS * MOE_PER_GROUP
MOE_FF = 512
MOE_TILE = 512

LANE = 128
VMEM_LIMIT = 56 * 1024 * 1024
LOG2E = math.log2(math.e)
FLASH_ROW_BLOCK = 256
FLASH_DQ = 256
FLASH_MAX_SHIFT = 60.0

OFF_GATE = 0
OFF_GQ = 8192
OFF_MZ = 9216
OFF_XBC = 10240
OFF_MQ = 11776
OFF_S5 = 12288
OFF_GK = 13056
OFF_GV = 13312
OFF_MKV = 13568
OFF_MKR = 13824
OFF_DT = 13952
NW = 14592


def _tile(n, *cands):
    for c in cands:
        if n % c == 0:
            return c
    return n


def _cp(sem, vmem=VMEM_LIMIT):
    return pltpu.CompilerParams(dimension_semantics=sem, vmem_limit_bytes=vmem)


def _sigmoid(x):
    return 1.0 / (1.0 + jnp.exp(-x))


def _silu(x):
    return x * _sigmoid(x)


def _ada_kernel(c_ref, w_ref, b_ref, o_ref):
    s = _silu(c_ref[...])
    o_ref[0] = jnp.dot(s.astype(BF16), w_ref[0].astype(BF16), preferred_element_type=F32) + b_ref[0]


def _ada(cvec, w_ada, b_ada):
    depth, d, n = w_ada.shape
    tn = _tile(n, 1024, 512)
    return pl.pallas_call(
        _ada_kernel,
        grid=(depth, n // tn),
        in_specs=[pl.BlockSpec((8, d), lambda l, j: (0, 0)),
                  pl.BlockSpec((1, d, tn), lambda l, j: (l, 0, j)),
                  pl.BlockSpec((1, 1, tn), lambda l, j: (l, 0, j))],
        out_specs=pl.BlockSpec((1, 8, tn), lambda l, j: (l, 0, j)),
        out_shape=jax.ShapeDtypeStruct((depth, 8, n), F32),
        compiler_params=_cp(("parallel", "parallel")),
        name="ada_mod",
    )(cvec, w_ada, b_ada.reshape(depth, 1, n))


ROW_CHUNK = 128


def _norm_mod(x, g, ml, mc, row_start, n_lat, row0):
    r = lax.rsqrt(jnp.mean(x * x, axis=-1, keepdims=True) + EPS)
    rows = row_start + lax.broadcasted_iota(jnp.int32, (x.shape[0], 1), 0)
    is_ctx = rows >= n_lat
    shift = jnp.where(is_ctx, mc[row0:row0 + 1], ml[row0:row0 + 1])
    scale = jnp.where(is_ctx, mc[row0 + 1:row0 + 2], ml[row0 + 1:row0 + 2])
    return x * r * g * (1.0 + scale) + shift


def _inproj_kernel(x_ref, g_ref, ml_ref, mc_ref, w_ref, o_ref, h_sc, *, n_lat, tm):
    @pl.when(pl.program_id(2) == 0)
    def _():
        def chunk(r, carry):
            rs = pl.multiple_of(r * ROW_CHUNK, ROW_CHUNK)
            h = _norm_mod(x_ref[0, pl.ds(rs, ROW_CHUNK), :], g_ref[...], ml_ref[0], mc_ref[0],
                          pl.program_id(1) * tm + rs, n_lat, 0)
            h_sc[pl.ds(rs, ROW_CHUNK), :] = h.astype(BF16)
            return carry

        lax.fori_loop(0, tm // ROW_CHUNK, chunk, 0)

    o_ref[0] = jnp.dot(h_sc[...], w_ref[...], preferred_element_type=F32).astype(o_ref.dtype)


def _inproj(x, gain, mods, w, n_lat):
    b, lt, d = x.shape
    n = w.shape[1]
    tm = _tile(lt, 1280, 256)
    tn = _tile(n, 768)
    return pl.pallas_call(
        functools.partial(_inproj_kernel, n_lat=n_lat, tm=tm),
        grid=(b, lt // tm, n // tn),
        in_specs=[pl.BlockSpec((1, tm, d), lambda bi, i, j: (bi, i, 0)),
                  pl.BlockSpec((1, d), lambda bi, i, j: (0, 0)),
                  pl.BlockSpec((1, 8, d), lambda bi, i, j: (bi, 0, 0)),
                  pl.BlockSpec((1, 8, d), lambda bi, i, j: (b, 0, 0)),
                  pl.BlockSpec((d, tn), lambda bi, i, j: (0, j))],
        out_specs=pl.BlockSpec((1, tm, tn), lambda bi, i, j: (bi, i, j)),
        out_shape=jax.ShapeDtypeStruct((b, lt, n), BF16),
        scratch_shapes=[pltpu.VMEM((tm, d), BF16)],
        compiler_params=_cp(("parallel", "parallel", "arbitrary")),
        name="in_proj",
    )(x, gain.reshape(1, d), mods, mods, w)


def _head_norm_rope(x, gain, cos, sin, rmat, n_valid, scale):
    ms = jnp.sum(x * x, axis=-1, keepdims=True) * (1.0 / n_valid)
    y = x * lax.rsqrt(ms + EPS) * gain
    if rmat is not None:
        rot = jnp.dot(y.astype(BF16), rmat, preferred_element_type=F32)
        y = y * cos + rot * sin
    return y * scale


def _gqa_prep_kernel(q_ref, k_ref, cos_ref, sin_ref, r_ref, qg_ref, kg_ref, sh_ref, qo_ref, ko_ref):
    cos, sin, rmat = cos_ref[...], sin_ref[...], r_ref[...]
    scale = HEAD_DIM ** -0.5 * LOG2E
    aux = jnp.broadcast_to(sh_ref[...], (q_ref.shape[1], LANE)).astype(BF16)
    for h in range(GQA_HEADS):
        x = q_ref[0, :, h * HEAD_DIM:(h + 1) * HEAD_DIM].astype(F32)
        qo_ref[0, :, h * FLASH_DQ:h * FLASH_DQ + HEAD_DIM] = _head_norm_rope(
            x, qg_ref[...], cos, sin, rmat, HEAD_DIM, scale).astype(BF16)
        qo_ref[0, :, h * FLASH_DQ + HEAD_DIM:(h + 1) * FLASH_DQ] = aux
    for h in range(GQA_KV_HEADS):
        x = k_ref[0, :, h * HEAD_DIM:(h + 1) * HEAD_DIM].astype(F32)
        ko_ref[0, :, h * HEAD_DIM:(h + 1) * HEAD_DIM] = _head_norm_rope(
            x, kg_ref[...], cos, sin, rmat, HEAD_DIM, 1.0).astype(BF16)


def _gqa_prep(proj, cos, sin, rmat, q_gain, k_gain, shift_row):
    b, lt, _ = proj.shape
    tm = _tile(lt, 640, 256)
    kw = GQA_KV_HEADS * HEAD_DIM
    qw = GQA_HEADS * FLASH_DQ
    return pl.pallas_call(
        _gqa_prep_kernel,
        grid=(b, lt // tm),
        in_specs=[pl.BlockSpec((1, tm, GQA_OUT), lambda bi, i: (bi, i, OFF_GQ // GQA_OUT)),
                  pl.BlockSpec((1, tm, kw), lambda bi, i: (bi, i, OFF_GK // kw)),
                  pl.BlockSpec((tm, HEAD_DIM), lambda bi, i: (i, 0)),
                  pl.BlockSpec((tm, HEAD_DIM), lambda bi, i: (i, 0)),
                  pl.BlockSpec((HEAD_DIM, HEAD_DIM), lambda bi, i: (0, 0)),
                  pl.BlockSpec((1, HEAD_DIM), lambda bi, i: (0, 0)),
                  pl.BlockSpec((1, HEAD_DIM), lambda bi, i: (0, 0)),
                  pl.BlockSpec((1, LANE), lambda bi, i: (0, 0))],
        out_specs=[pl.BlockSpec((1, tm, qw), lambda bi, i: (bi, i, 0)),
                   pl.BlockSpec((1, tm, kw), lambda bi, i: (bi, i, 0))],
        out_shape=[jax.ShapeDtypeStruct((b, lt, qw), BF16),
                   jax.ShapeDtypeStruct((b, lt, kw), BF16)],
        compiler_params=_cp(("parallel", "parallel")),
        name="gqa_prep",
    )(proj, proj, cos, sin, rmat, q_gain.reshape(1, -1), k_gain.reshape(1, -1), shift_row)


def _mla_prep_kernel(cq_ref, ckv_ref, kr_ref, cos_ref, sin_ref, r_ref,
                     qn_g_ref, kvn_g_ref, wqn_ref, wqr_ref, wuk_ref, wuv_ref,
                     qnn_g_ref, qrn_g_ref, knn_g_ref, krn_g_ref, qsh_ref, kone_ref,
                     q_ref, kn_ref, kr_out_ref, v_ref):
    cos, sin, rmat = cos_ref[...], sin_ref[...], r_ref[...]
    scale = (MLA_NOPE + MLA_ROPE) ** -0.5 * LOG2E
    cq = cq_ref[0].astype(F32)
    cqn = (cq * lax.rsqrt(jnp.mean(cq * cq, axis=-1, keepdims=True) + EPS) * qn_g_ref[...]).astype(BF16)
    q_nope = jnp.dot(cqn, wqn_ref[...], preferred_element_type=F32)
    q_rope = jnp.dot(cqn, wqr_ref[...], preferred_element_type=F32)
    for h in range(MLA_HEADS):
        sl = slice(h * HEAD_DIM, (h + 1) * HEAD_DIM)
        qn = _head_norm_rope(q_nope[:, sl], qnn_g_ref[...], None, None, None, MLA_NOPE, scale)
        qr = _head_norm_rope(q_rope[:, sl], qrn_g_ref[...], cos, sin, rmat, MLA_ROPE, scale)
        q_ref[0, :, 2 * h * HEAD_DIM:(2 * h + 1) * HEAD_DIM] = qn.astype(BF16)
        q_ref[0, :, (2 * h + 1) * HEAD_DIM:(2 * h + 2) * HEAD_DIM] = (qr + qsh_ref[...]).astype(BF16)
    ckv = ckv_ref[0].astype(F32)
    lat = (ckv * lax.rsqrt(jnp.mean(ckv * ckv, axis=-1, keepdims=True) + EPS) * kvn_g_ref[...]).astype(BF16)
    k_nope = jnp.dot(lat, wuk_ref[...], preferred_element_type=F32)
    v_ref[0] = jnp.dot(lat, wuv_ref[...], preferred_element_type=F32).astype(BF16)
    for h in range(MLA_HEADS):
        sl = slice(h * HEAD_DIM, (h + 1) * HEAD_DIM)
        kn_ref[0, :, sl] = _head_norm_rope(k_nope[:, sl], knn_g_ref[...], None, None, None, MLA_NOPE, 1.0).astype(BF16)
    kr = kr_ref[0].astype(F32)
    kr_out_ref[0] = (_head_norm_rope(kr, krn_g_ref[...], cos, sin, rmat, MLA_ROPE, 1.0) + kone_ref[...]).astype(BF16)


def _mla_prep(proj, cos, sin, rmat, p):
    b, lt, _ = proj.shape
    tm = _tile(lt, 640, 256)
    hd = MLA_HEADS * HEAD_DIM
    full = lambda shape: pl.BlockSpec(shape, lambda bi, i: tuple(0 for _ in shape))
    return pl.pallas_call(
        _mla_prep_kernel,
        grid=(b, lt // tm),
        in_specs=[pl.BlockSpec((1, tm, MLA_Q_RANK), lambda bi, i: (bi, i, OFF_MQ // MLA_Q_RANK)),
                  pl.BlockSpec((1, tm, MLA_KV_RANK), lambda bi, i: (bi, i, OFF_MKV // MLA_KV_RANK)),
                  pl.BlockSpec((1, tm, LANE), lambda bi, i: (bi, i, OFF_MKR // LANE)),
                  pl.BlockSpec((tm, LANE), lambda bi, i: (i, 0)),
                  pl.BlockSpec((tm, LANE), lambda bi, i: (i, 0)),
                  full((LANE, LANE)),
                  full((1, MLA_Q_RANK)), full((1, MLA_KV_RANK)),
                  full((MLA_Q_RANK, hd)), full((MLA_Q_RANK, hd)),
                  full((MLA_KV_RANK, hd)), full((MLA_KV_RANK, hd)),
                  full((1, LANE)), full((1, LANE)), full((1, LANE)), full((1, LANE)),
                  full((1, LANE)), full((1, LANE))],
        out_specs=[pl.BlockSpec((1, tm, 2 * hd), lambda bi, i: (bi, i, 0)),
                   pl.BlockSpec((1, tm, hd), lambda bi, i: (bi, i, 0)),
                   pl.BlockSpec((1, tm, LANE), lambda bi, i: (bi, i, 0)),
                   pl.BlockSpec((1, tm, hd), lambda bi, i: (bi, i, 0))],
        out_shape=[jax.ShapeDtypeStruct((b, lt, 2 * hd), BF16),
                   jax.ShapeDtypeStruct((b, lt, hd), BF16),
                   jax.ShapeDtypeStruct((b, lt, LANE), BF16),
                   jax.ShapeDtypeStruct((b, lt, hd), BF16)],
        compiler_params=_cp(("parallel", "parallel")),
        name="mla_prep",
    )(proj, proj, proj, cos, sin, rmat,
      p["mla_q_norm"].reshape(1, -1), p["mla_kv_norm"].reshape(1, -1),
      p["wq_nope"], p["wq_rope"], p["w_uk"], p["w_uv"],
      p["mla_qn_norm"].reshape(1, -1), p["qr_gain"], p["mla_kn_norm"].reshape(1, -1), p["kr_gain"],
      p["q_shift_row"], p["k_one_row"])


def _stack_queries(q_ref, q_sc, grp, tq):
    for g in range(grp):
        q_sc[g * tq:(g + 1) * tq, :] = q_ref[0, :, g * FLASH_DQ:(g + 1) * FLASH_DQ]


def _write_heads(o_ref, o, grp, tq):
    for g in range(grp):
        o_ref[0, :, g * HEAD_DIM:(g + 1) * HEAD_DIM] = o[g * tq:(g + 1) * tq].astype(o_ref.dtype)


def _flash_online_kernel(q_ref, k_ref, ka_ref, v_ref, o_ref, q_sc, m_sc, l_sc, acc_sc, *, grp, tq):
    kv = pl.program_id(3)

    @pl.when(kv == 0)
    def _():
        _stack_queries(q_ref, q_sc, grp, tq)
        m_sc[...] = jnp.full_like(m_sc, -jnp.inf)
        l_sc[...] = jnp.zeros_like(l_sc)
        acc_sc[...] = jnp.zeros_like(acc_sc)

    k = jnp.concatenate([k_ref[0], ka_ref[0]], axis=-1)
    v = v_ref[0]
    n_chunks = k.shape[0] // LANE
    rb = min(FLASH_ROW_BLOCK, grp * tq)
    for r in range(grp * tq // rb):
        rows = slice(r * rb, (r + 1) * rb)
        s = lax.dot_general(q_sc[rows, :], k, (((1,), (1,)), ((), ())), preferred_element_type=F32)
        m_prev = m_sc[rows, :]
        m_new = jnp.maximum(m_prev, jnp.max(s, axis=-1, keepdims=True))
        alpha = jnp.exp2(m_prev - m_new)
        p = jnp.exp2(s - jnp.concatenate([m_new] * n_chunks, axis=-1))
        p_sum = p[:, :LANE]
        for c in range(1, n_chunks):
            p_sum = p_sum + p[:, c * LANE:(c + 1) * LANE]
        l_sc[rows, :] = alpha * l_sc[rows, :] + p_sum
        acc_sc[rows, :] = alpha * acc_sc[rows, :] + jnp.dot(p.astype(BF16), v, preferred_element_type=F32)
        m_sc[rows, :] = m_new

    @pl.when(kv == pl.num_programs(3) - 1)
    def _():
        _write_heads(o_ref, acc_sc[...] / jnp.sum(l_sc[...], axis=-1, keepdims=True), grp, tq)


def _flash_bounded_kernel(q_ref, k_ref, ka_ref, v_ref, o_ref, q_sc, acc_sc, *, grp, tq):
    kv = pl.program_id(3)

    @pl.when(kv == 0)
    def _():
        _stack_queries(q_ref, q_sc, grp, tq)
        acc_sc[...] = jnp.zeros_like(acc_sc)

    k = jnp.concatenate([k_ref[0], ka_ref[0]], axis=-1)
    v = v_ref[0]
    ones_col = (lax.broadcasted_iota(jnp.int32, v.shape, 1) == 0).astype(BF16)
    v1 = jnp.concatenate([v, ones_col], axis=-1)
    rb = min(FLASH_ROW_BLOCK, grp * tq)
    for r in range(grp * tq // rb):
        rows = slice(r * rb, (r + 1) * rb)
        s = lax.dot_general(q_sc[rows, :], k, (((1,), (1,)), ((), ())), preferred_element_type=F32)
        acc_sc[rows, :] += jnp.dot(jnp.exp2(s).astype(BF16), v1, preferred_element_type=F32)

    @pl.when(kv == pl.num_programs(3) - 1)
    def _():
        acc = acc_sc[...]
        _write_heads(o_ref, acc[:, :HEAD_DIM] / acc[:, HEAD_DIM:HEAD_DIM + 1], grp, tq)


def _flash(q, k, ka, v, *, bounded, n_kv_heads, grp, tq, tk, q_rows, q_off, kv_rows, kv_off, v_col0):
    b = q.shape[0]
    qo, ko = q_off // tq, kv_off // tk
    rows = grp * tq
    if bounded:
        body = _flash_bounded_kernel
        scratch = [pltpu.VMEM((rows, FLASH_DQ), BF16), pltpu.VMEM((rows, 2 * HEAD_DIM), F32)]
    else:
        body = _flash_online_kernel
        scratch = [pltpu.VMEM((rows, FLASH_DQ), BF16), pltpu.VMEM((rows, LANE), F32),
                   pltpu.VMEM((rows, LANE), F32), pltpu.VMEM((rows, HEAD_DIM), F32)]
    return pl.pallas_call(
        functools.partial(body, grp=grp, tq=tq),
        grid=(b, n_kv_heads, q_rows // tq, kv_rows // tk),
        in_specs=[pl.BlockSpec((1, tq, grp * FLASH_DQ), lambda bi, h, i, j: (bi, i + qo, h)),
                  pl.BlockSpec((1, tk, HEAD_DIM), lambda bi, h, i, j: (bi, j + ko, h)),
                  pl.BlockSpec((1, tk, LANE), lambda bi, h, i, j: (bi, j + ko, 0)),
                  pl.BlockSpec((1, tk, HEAD_DIM), lambda bi, h, i, j: (bi, j + ko, v_col0 + h))],
        out_specs=pl.BlockSpec((1, tq, grp * HEAD_DIM), lambda bi, h, i, j: (bi, i, h)),
        out_shape=jax.ShapeDtypeStruct((b, q_rows, n_kv_heads * grp * HEAD_DIM), BF16),
        scratch_shapes=scratch,
        compiler_params=_cp(("parallel", "parallel", "parallel", "arbitrary")),
        name="flash_bounded" if bounded else "flash_online",
    )(q, k, ka, v)


def _attention(q, k, ka, v, shift, *, n_kv_heads, grp, n_lat, v_col0, tq_lat, tk_lat):
    lt = q.shape[1]
    n_ctx = lt - n_lat
    tk = _tile(lt, tk_lat, 256)
    tq = _tile(n_lat, tq_lat, 256)

    def run(bounded):
        def go(q, k, ka, v):
            y_lat = _flash(q, k, ka, v, bounded=bounded, n_kv_heads=n_kv_heads, grp=grp, tq=tq, tk=tk,
                           q_rows=n_lat, q_off=0, kv_rows=lt, kv_off=0, v_col0=v_col0)
            y_ctx = _flash(q, k, ka, v, bounded=bounded, n_kv_heads=n_kv_heads, grp=grp, tq=n_ctx, tk=n_ctx,
                           q_rows=n_ctx, q_off=n_lat, kv_rows=n_ctx, kv_off=n_lat, v_col0=v_col0)
            return jnp.concatenate([y_lat, y_ctx], axis=1)
        return go

    return lax.cond(shift <= FLASH_MAX_SHIFT, run(True), run(False), q, k, ka, v)


def _conv_kernel(prev_ref, cur_ref, next_ref, w_ref, b_ref, o_ref, *, tr, n_lat, lt):
    i = pl.program_id(1)
    start = i * tr
    has_prev = jnp.logical_and(start != 0, start != n_lat).astype(F32)
    has_next = jnp.logical_and(start + tr != n_lat, start + tr != lt).astype(F32)
    ext = jnp.concatenate([prev_ref[0].astype(F32) * has_prev, cur_ref[0].astype(F32),
                           next_ref[0].astype(F32) * has_next], axis=0)
    w = w_ref[...]
    acc = jnp.zeros((tr, cur_ref.shape[2]), F32) + b_ref[...]
    half = MB_CONV // 2
    for kk in range(MB_CONV):
        acc = acc + ext[8 + kk - half:8 + kk - half + tr, :] * w[kk:kk + 1, :]
    o_ref[0] = _silu(acc).astype(o_ref.dtype)


def _mamba_conv(proj, conv_w, conv_b, n_lat):
    b, lt, _ = proj.shape
    tr = _tile(n_lat, 256)
    cw = 512
    cb0 = OFF_XBC // cw
    r8 = tr // 8
    nblk8 = lt // 8
    w8 = jnp.zeros((8, MB_XBC), F32).at[:MB_CONV].set(conv_w.T.astype(F32))
    return pl.pallas_call(
        functools.partial(_conv_kernel, tr=tr, n_lat=n_lat, lt=lt),
        grid=(b, lt // tr, MB_XBC // cw),
        in_specs=[pl.BlockSpec((1, 8, cw), lambda bi, i, c: (bi, jnp.maximum(i * r8 - 1, 0), cb0 + c)),
                  pl.BlockSpec((1, tr, cw), lambda bi, i, c: (bi, i, cb0 + c)),
                  pl.BlockSpec((1, 8, cw), lambda bi, i, c: (bi, jnp.minimum((i + 1) * r8, nblk8 - 1), cb0 + c)),
                  pl.BlockSpec((8, cw), lambda bi, i, c: (0, c)),
                  pl.BlockSpec((1, cw), lambda bi, i, c: (0, c))],
        out_specs=pl.BlockSpec((1, tr, cw), lambda bi, i, c: (bi, i, c)),
        out_shape=jax.ShapeDtypeStruct((b, lt, MB_XBC), BF16),
        compiler_params=_cp(("parallel", "parallel", "parallel")),
        name="mamba_conv",
    )(proj, proj, proj, w8, conv_b.reshape(1, -1).astype(F32))


def _softplus(x):
    return jnp.maximum(x, 0.0) + jnp.log1p(jnp.exp(-jnp.abs(x)))


def _ssd_kernel(xs_ref, bm_ref, cm_ref, dt_ref, bias_ref, a_ref, y_ref, h_sc):
    t = SSD_CHUNK
    d = pl.program_id(2)

    @pl.when(pl.program_id(3) == 0)
    def _():
        h_sc[...] = jnp.zeros_like(h_sc)

    dt = _softplus(dt_ref[0].astype(F32) + bias_ref[0])
    a = dt * a_ref[0]
    row = lax.broadcasted_iota(jnp.int32, (t, t), 0)
    col = lax.broadcasted_iota(jnp.int32, (t, t), 1)
    sgn = 1 - 2 * d
    mask = (row - col) * sgn >= 0
    tri = mask.astype(BF16)
    a_hi = a.astype(BF16)
    a_lo = (a - a_hi.astype(F32)).astype(BF16)
    cum = jnp.dot(tri, a_hi, preferred_element_type=F32) + jnp.dot(tri, a_lo, preferred_element_type=F32)
    total = jnp.sum(a, axis=0, keepdims=True)
    cum_t = cum.T
    e_cum = jnp.exp(cum)
    e_end = jnp.exp(total - cum)
    e_tot = jnp.exp(total)

    bm = bm_ref[0]
    cm = cm_ref[0]
    scores = lax.dot_general(cm, bm, (((1,), (1,)), ((), ())), preferred_element_type=F32)
    bm_t = bm.astype(F32).T.astype(BF16)
    lane = lax.broadcasted_iota(jnp.int32, (t, LANE), 1)
    first = lane < MB_HEAD_DIM
    first_row = first[0:1]

    def per_head(v, j):
        return jnp.where(first if v.shape[0] == t else first_row, v[:, j:j + 1], v[:, j + 1:j + 2])

    for m in range(MB_HPG // 2):
        j = 2 * m
        cols = slice(m * LANE, (m + 1) * LANE)
        xdt = xs_ref[0, :, cols].astype(F32) * per_head(dt, j)
        h = h_sc[:, cols]
        y = jnp.dot(cm, h.astype(BF16), preferred_element_type=F32) * per_head(e_cum, j)
        xw = (xdt * per_head(e_end, j)).astype(BF16)
        h_sc[:, cols] = h * per_head(e_tot, j) + jnp.dot(bm_t, xw, preferred_element_type=F32)
        xdt_b = xdt.astype(BF16)
        for q in range(2):
            diff = cum[:, j + q:j + q + 1] - cum_t[j + q:j + q + 1, :]
            decay = jnp.where(mask, jnp.exp(jnp.minimum(diff, 0.0)), 0.0)
            keep = first if q == 0 else jnp.logical_not(first)
            y = y + jnp.dot((scores * decay).astype(BF16), jnp.where(keep, xdt_b, jnp.zeros_like(xdt_b)),
                            preferred_element_type=F32)
        y_ref[0, 0, :, cols] = y


def _ssd(xbc, proj, dt_bias4, a4, n_lat):
    b, lt, _ = xbc.shape
    t = SSD_CHUNK
    n_lat_c = n_lat // t
    n_ctx_c = (lt - n_lat) // t
    nc = n_lat_c + n_ctx_c
    width = MB_HPG * MB_HEAD_DIM

    def chunk(d, i):
        fwd = jnp.where(i < n_ctx_c, n_lat_c + i, i - n_ctx_c)
        bwd = jnp.where(i < n_ctx_c, n_lat_c + n_ctx_c - 1 - i, n_lat_c - 1 - (i - n_ctx_c))
        return jnp.where(d == 0, fwd, bwd)

    return pl.pallas_call(
        _ssd_kernel,
        grid=(b, MB_GROUPS, 2, nc),
        in_specs=[pl.BlockSpec((1, t, width), lambda bi, g, d, i: (bi, chunk(d, i), g)),
                  pl.BlockSpec((1, t, MB_STATE), lambda bi, g, d, i: (bi, chunk(d, i), MB_INNER // MB_STATE + g)),
                  pl.BlockSpec((1, t, MB_STATE), lambda bi, g, d, i: (bi, chunk(d, i), MB_INNER // MB_STATE + MB_GROUPS + g)),
                  pl.BlockSpec((1, t, LANE), lambda bi, g, d, i: (bi, chunk(d, i), OFF_DT // LANE + d * MB_GROUPS + g)),
                  pl.BlockSpec((1, 1, LANE), lambda bi, g, d, i: (d * MB_GROUPS + g, 0, 0)),
                  pl.BlockSpec((1, 1, LANE), lambda bi, g, d, i: (d * MB_GROUPS + g, 0, 0))],
        out_specs=pl.BlockSpec((1, 1, t, width), lambda bi, g, d, i: (bi, d, chunk(d, i), g)),
        out_shape=jax.ShapeDtypeStruct((b, 2, lt, MB_INNER), F32),
        scratch_shapes=[pltpu.VMEM((MB_STATE, width), F32)],
        compiler_params=_cp(("parallel", "parallel", "parallel", "arbitrary")),
        name="ssd_scan",
    )(xbc, xbc, xbc, proj, dt_bias4, a4)


def _mamba_finish_kernel(yf_ref, yb_ref, xs_ref, z_ref, d_ref, g_ref, o_ref):
    y = yf_ref[0, 0] + yb_ref[0, 0] + xs_ref[0].astype(F32) * d_ref[...]
    y = y * _silu(z_ref[0].astype(F32))
    o_ref[0] = (y * lax.rsqrt(jnp.mean(y * y, axis=-1, keepdims=True) + EPS) * g_ref[...]).astype(o_ref.dtype)


def _mamba_finish(y2, xbc, proj, d_vec, gain):
    b, lt, _ = xbc.shape
    tm = _tile(lt, 640, 256)
    w = MB_INNER
    return pl.pallas_call(
        _mamba_finish_kernel,
        grid=(b, lt // tm),
        in_specs=[pl.BlockSpec((1, 1, tm, w), lambda bi, i: (bi, 0, i, 0)),
                  pl.BlockSpec((1, 1, tm, w), lambda bi, i: (bi, 1, i, 0)),
                  pl.BlockSpec((1, tm, w), lambda bi, i: (bi, i, 0)),
                  pl.BlockSpec((1, tm, w), lambda bi, i: (bi, i, OFF_MZ // w)),
                  pl.BlockSpec((1, w), lambda bi, i: (0, 0)),
                  pl.BlockSpec((1, w), lambda bi, i: (0, 0))],
        out_specs=pl.BlockSpec((1, tm, w), lambda bi, i: (bi, i, 0)),
        out_shape=jax.ShapeDtypeStruct((b, lt, w), BF16),
        compiler_params=_cp(("parallel", "parallel")),
        name="mamba_finish",
    )(y2, y2, xbc, proj, d_vec, gain.reshape(1, -1))


def _s5_in_kernel(u_ref, w_ref, yi_ref, s_ref):
    r = jnp.dot(u_ref[0], w_ref[0], preferred_element_type=F32)
    k = S5_CHUNK * S5_GROUP
    yi_ref[0] = r[:, :k]
    s_ref[...] = r[:, k:]


def _s5_in(u_g, w_g):
    g, m, k = u_g.shape
    n = w_g.shape[2]
    return pl.pallas_call(
        _s5_in_kernel,
        grid=(g,),
        in_specs=[pl.BlockSpec((1, m, k), lambda gi: (gi, 0, 0)),
                  pl.BlockSpec((1, k, n), lambda gi: (gi, 0, 0))],
        out_specs=[pl.BlockSpec((1, m, k), lambda gi: (gi, 0, 0)),
                   pl.BlockSpec((m, n - k), lambda gi: (0, gi))],
        out_shape=[jax.ShapeDtypeStruct((g, m, k), F32),
                   jax.ShapeDtypeStruct((m, g * (n - k)), F32)],
        compiler_params=_cp(("parallel",)),
        name="s5_intra",
    )(u_g, w_g)


def _s5_scan_kernel(sf_ref, sb_ref, af_ref, ab_ref, hf_ref, hb_ref, st_sc, *, tc):
    @pl.when(pl.program_id(1) == 0)
    def _():
        st_sc[...] = jnp.zeros_like(st_sc)

    a1f, a2f = af_ref[0], af_ref[1]
    a1b, a2b = ab_ref[0], ab_ref[1]

    def body(c, carry):
        hf, hb = carry
        hf_ref[0, c] = hf
        hf = a1f * hf + a2f * pltpu.roll(hf, S5_STATE, 1) + sf_ref[0, c]
        cb = tc - 1 - c
        hb_ref[0, cb] = hb
        hb = a1b * hb + a2b * pltpu.roll(hb, S5_STATE, 1) + sb_ref[0, cb]
        return hf, hb

    hf, hb = lax.fori_loop(0, tc, body, (st_sc[0], st_sc[1]))
    st_sc[0] = hf
    st_sc[1] = hb


def _s5_scan(s4, a_f, a_b, n_lat_chunks):
    b, nc, g, _ = s4.shape
    tc = 16
    nlt = n_lat_chunks // tc
    nt = nc // tc
    w = 2 * S5_STATE

    def tile_f(i):
        return jnp.where(i < nt - nlt, nlt + i, i - (nt - nlt))

    def tile_b(i):
        return jnp.where(i < nt - nlt, nt - 1 - i, nlt - 1 - (i - (nt - nlt)))

    return pl.pallas_call(
        functools.partial(_s5_scan_kernel, tc=tc),
        grid=(b, nt),
        in_specs=[pl.BlockSpec((1, tc, g, w), lambda bi, i: (bi, tile_f(i), 0, 0)),
                  pl.BlockSpec((1, tc, g, w), lambda bi, i: (bi, tile_b(i), 0, 1)),
                  pl.BlockSpec((2, g, w), lambda bi, i: (0, 0, 0)),
                  pl.BlockSpec((2, g, w), lambda bi, i: (0, 0, 0))],
        out_specs=[pl.BlockSpec((1, tc, g, w), lambda bi, i: (bi, tile_f(i), 0, 0)),
                   pl.BlockSpec((1, tc, g, w), lambda bi, i: (bi, tile_b(i), 0, 0))],
        out_shape=[jax.ShapeDtypeStruct((b, nc, g, w), F32),
                   jax.ShapeDtypeStruct((b, nc, g, w), F32)],
        scratch_shapes=[pltpu.VMEM((2, g, w), F32)],
        compiler_params=_cp(("parallel", "arbitrary")),
        name="s5_state_scan",
    )(s4, s4, a_f, a_b)


def _s5_out_kernel(yi_ref, hf_ref, hb_ref, qf_ref, qb_ref, o_ref):
    o_ref[0] = (yi_ref[0]
                + jnp.dot(hf_ref[...].astype(BF16), qf_ref[0], preferred_element_type=F32)
                + jnp.dot(hb_ref[...].astype(BF16), qb_ref[0], preferred_element_type=F32))


def _s5_out(yi, hf2, hb2, qf, qb):
    g, m, k = yi.shape
    w = 2 * S5_STATE
    return pl.pallas_call(
        _s5_out_kernel,
        grid=(g,),
        in_specs=[pl.BlockSpec((1, m, k), lambda gi: (gi, 0, 0)),
                  pl.BlockSpec((m, w), lambda gi: (0, gi)),
                  pl.BlockSpec((m, w), lambda gi: (0, gi)),
                  pl.BlockSpec((1, w, k), lambda gi: (gi, 0, 0)),
                  pl.BlockSpec((1, w, k), lambda gi: (gi, 0, 0))],
        out_specs=pl.BlockSpec((1, m, k), lambda gi: (gi, 0, 0)),
        out_shape=jax.ShapeDtypeStruct((g, m, k), F32),
        compiler_params=_cp(("parallel",)),
        name="s5_readout",
    )(yi, hf2, hb2, qf, qb)


def _gelu_tanh(x):
    return 0.5 * x * (1.0 + jnp.tanh(math.sqrt(2.0 / math.pi) * (x + 0.044715 * (x * x * x))))


def _s5_finish_kernel(y_ref, u_ref, d_ref, w_ref, o_ref):
    y = y_ref[0] + d_ref[...] * u_ref[0].astype(F32)
    g = _gelu_tanh(y)
    o_ref[0] = (g * _sigmoid(jnp.dot(g.astype(BF16), w_ref[...], preferred_element_type=F32))).astype(o_ref.dtype)


def _s5_finish(y, proj, d_vec, w_glu):
    b, lt, w = y.shape
    tm = _tile(lt, 640, 256)
    return pl.pallas_call(
        _s5_finish_kernel,
        grid=(b, lt // tm),
        in_specs=[pl.BlockSpec((1, tm, w), lambda bi, i: (bi, i, 0)),
                  pl.BlockSpec((1, tm, w), lambda bi, i: (bi, i, OFF_S5 // w)),
                  pl.BlockSpec((1, w), lambda bi, i: (0, 0)),
                  pl.BlockSpec((w, w), lambda bi, i: (0, 0))],
        out_specs=pl.BlockSpec((1, tm, w), lambda bi, i: (bi, i, 0)),
        out_shape=jax.ShapeDtypeStruct((b, lt, w), BF16),
        compiler_params=_cp(("parallel", "parallel")),
        name="s5_finish",
    )(y, proj, d_vec.reshape(1, -1), w_glu)


def _s5_operators(p):
    t = S5_CHUNK
    lam = lax.complex(p["s5_lam_re"].astype(F32), p["s5_lam_im"].astype(F32))
    step = jnp.exp(p["s5_log_step"].astype(F32))[..., None]
    lam_bar = jnp.exp(lam * step)
    b_bar = ((lam_bar - 1.0) / lam)[..., None] * lax.complex(p["s5_b_re"].astype(F32), p["s5_b_im"].astype(F32))
    c_mat = lax.complex(p["s5_c_re"].astype(F32), p["s5_c_im"].astype(F32))
    ks = jnp.arange(t + 1, dtype=F32)
    pw = jnp.exp((lam * step)[:, :, None, :] * ks[None, None, :, None])
    kern = jnp.real(jnp.einsum("dgcn,dgkn,dgni->dgkci", c_mat, pw[:, :, :t], b_bar))
    s_idx = jnp.arange(t)[:, None]
    t_idx = jnp.arange(t)[None, :]
    lag_f = jnp.clip(t_idx - s_idx, 0, t - 1)
    lag_b = jnp.clip(s_idx - t_idx, 0, t - 1)
    kf = jnp.where((t_idx >= s_idx)[None, :, :, None, None], kern[0][:, lag_f], 0.0)
    kb = jnp.where((s_idx >= t_idx)[None, :, :, None, None], kern[1][:, lag_b], 0.0)
    w_intra = jnp.transpose(kf + kb, (0, 1, 4, 2, 3)).reshape(S5_GROUPS, t * S5_GROUP, t * S5_GROUP)
    inj_f = pw[0][:, t - 1 - jnp.arange(t), :, None] * b_bar[0][:, None]
    inj_b = pw[1][:, jnp.arange(t), :, None] * b_bar[1][:, None]

    def inj_mat(z):
        z = jnp.transpose(z, (0, 1, 3, 2)).reshape(S5_GROUPS, t * S5_GROUP, S5_STATE)
        return jnp.concatenate([jnp.real(z), jnp.imag(z)], axis=-1)

    w_in = jnp.concatenate([w_intra, inj_mat(inj_f), inj_mat(inj_b)], axis=-1).astype(BF16)
    m_f = c_mat[0][:, None, :, :] * pw[0][:, 1 + jnp.arange(t), None, :]
    m_b = c_mat[1][:, None, :, :] * pw[1][:, t - jnp.arange(t), None, :]

    def read_mat(z):
        z = jnp.transpose(z.reshape(S5_GROUPS, t * S5_GROUP, S5_STATE), (0, 2, 1))
        return jnp.concatenate([jnp.real(z), -jnp.imag(z)], axis=1).astype(BF16)

    def trans(z):
        re, im = jnp.real(z), jnp.imag(z)
        return jnp.stack([jnp.concatenate([re, re], -1), jnp.concatenate([-im, im], -1)])

    return dict(w_in=w_in, q_f=read_mat(m_f), q_b=read_mat(m_b), a_f=trans(pw[0][:, t]), a_b=trans(pw[1][:, t]))


def _s5_mixer(proj, ops, d_vec, w_glu, n_lat):
    b, lt, _ = proj.shape
    t = S5_CHUNK
    nc = lt // t
    u = proj[:, :, OFF_S5:OFF_S5 + S5_WIDTH]
    u_g = jnp.transpose(u.reshape(b, nc, t, S5_GROUPS, S5_GROUP), (3, 0, 1, 2, 4)).reshape(S5_GROUPS, b * nc, t * S5_GROUP)
    yi, s = _s5_in(u_g, ops["w_in"])
    s4 = s.reshape(b, nc, S5_GROUPS, 4 * S5_STATE)
    hf, hb = _s5_scan(s4, ops["a_f"], ops["a_b"], n_lat // t)
    y_g = _s5_out(yi, hf.reshape(b * nc, -1), hb.reshape(b * nc, -1), ops["q_f"], ops["q_b"])
    y = jnp.transpose(y_g.reshape(S5_GROUPS, b, nc, t, S5_GROUP), (1, 2, 3, 0, 4)).reshape(b, lt, S5_WIDTH)
    return _s5_finish(y, proj, d_vec, w_glu)


def _merge_kernel(ya_ref, yb_ref, yc_ref, yd_ref, g0_ref, g1_ref, g2_ref, g3_ref,
                  wa_ref, wb_ref, wc_ref, wd_ref, o_ref):
    acc = None
    for y_ref, g_ref, w_ref in ((ya_ref, g0_ref, wa_ref), (yb_ref, g1_ref, wb_ref),
                                (yc_ref, g2_ref, wc_ref), (yd_ref, g3_ref, wd_ref)):
        term = _sigmoid(g_ref[0].astype(F32)) * jnp.dot(y_ref[0], w_ref[...], preferred_element_type=F32)
        acc = term if acc is None else acc + term
    o_ref[0] = acc.astype(o_ref.dtype)


def _merge(ys, proj, ws):
    b, lt, _ = proj.shape
    d = D_MODEL
    tm = _tile(lt, 640, 256)
    tn = 512
    nb = d // tn
    in_specs = [pl.BlockSpec((1, tm, y.shape[2]), lambda bi, i, j: (bi, i, 0)) for y in ys]
    in_specs += [pl.BlockSpec((1, tm, tn), functools.partial(lambda bi, i, j, br: (bi, i, br * nb + j), br=br))
                 for br in range(4)]
    in_specs += [pl.BlockSpec((w.shape[0], tn), lambda bi, i, j: (0, j)) for w in ws]
    return pl.pallas_call(
        _merge_kernel,
        grid=(b, lt // tm, nb),
        in_specs=in_specs,
        out_specs=pl.BlockSpec((1, tm, tn), lambda bi, i, j: (bi, i, j)),
        out_shape=jax.ShapeDtypeStruct((b, lt, d), BF16),
        compiler_params=_cp(("parallel", "parallel", "parallel")),
        name="branch_merge",
    )(*ys, proj, proj, proj, proj, *ws)


def _row_gate(ml, mc, row_tile, tm, n_lat, row):
    rows = row_tile * tm + lax.broadcasted_iota(jnp.int32, (tm, 1), 0)
    return jnp.where(rows >= n_lat, mc[row:row + 1], ml[row:row + 1])


def _outproj_kernel(m_ref, w_ref, x_ref, ml_ref, mc_ref, o_ref, *, tm, n_lat):
    gate = _row_gate(ml_ref[0], mc_ref[0], pl.program_id(1), tm, n_lat, 2)
    o_ref[0] = x_ref[0] + gate * jnp.dot(m_ref[0], w_ref[...], preferred_element_type=F32)


def _outproj(m, w_out, x, mods, n_lat):
    b, lt, d = x.shape
    tm = _tile(lt, 1280, 256)
    tn = 512
    return pl.pallas_call(
        functools.partial(_outproj_kernel, tm=tm, n_lat=n_lat),
        grid=(b, lt // tm, d // tn),
        in_specs=[pl.BlockSpec((1, tm, d), lambda bi, i, j: (bi, i, 0)),
                  pl.BlockSpec((d, tn), lambda bi, i, j: (0, j)),
                  pl.BlockSpec((1, tm, tn), lambda bi, i, j: (bi, i, j)),
                  pl.BlockSpec((1, 8, tn), lambda bi, i, j: (bi, 0, j)),
                  pl.BlockSpec((1, 8, tn), lambda bi, i, j: (b, 0, j))],
        out_specs=pl.BlockSpec((1, tm, tn), lambda bi, i, j: (bi, i, j)),
        out_shape=jax.ShapeDtypeStruct((b, lt, d), F32),
        compiler_params=_cp(("parallel", "parallel", "parallel")),
        name="out_proj",
    )(m, w_out, x, mods, mods)


def _prenorm_router_kernel(x_ref, g_ref, ml_ref, mc_ref, wr_ref, br_ref, h_ref, lg_ref, *, tm, n_lat):
    def chunk(r, carry):
        rs = pl.multiple_of(r * ROW_CHUNK, ROW_CHUNK)
        h = _norm_mod(x_ref[0, pl.ds(rs, ROW_CHUNK), :], g_ref[...], ml_ref[0], mc_ref[0],
                      pl.program_id(1) * tm + rs, n_lat, 3).astype(BF16)
        h_ref[0, pl.ds(rs, ROW_CHUNK), :] = h
        lg_ref[0, pl.ds(rs, ROW_CHUNK), :] = jnp.dot(h, wr_ref[...], preferred_element_type=F32) + br_ref[...]
        return carry

    lax.fori_loop(0, tm // ROW_CHUNK, chunk, 0)


def _prenorm_router(x, gain, mods, w_r, b_r, n_lat):
    b, lt, d = x.shape
    tm = _tile(lt, 640, 256)
    return pl.pallas_call(
        functools.partial(_prenorm_router_kernel, tm=tm, n_lat=n_lat),
        grid=(b, lt // tm),
        in_specs=[pl.BlockSpec((1, tm, d), lambda bi, i: (bi, i, 0)),
                  pl.BlockSpec((1, d), lambda bi, i: (0, 0)),
                  pl.BlockSpec((1, 8, d), lambda bi, i: (bi, 0, 0)),
                  pl.BlockSpec((1, 8, d), lambda bi, i: (b, 0, 0)),
                  pl.BlockSpec((d, LANE), lambda bi, i: (0, 0)),
                  pl.BlockSpec((1, LANE), lambda bi, i: (0, 0))],
        out_specs=[pl.BlockSpec((1, tm, d), lambda bi, i: (bi, i, 0)),
                   pl.BlockSpec((1, tm, LANE), lambda bi, i: (bi, i, 0))],
        out_shape=[jax.ShapeDtypeStruct((b, lt, d), BF16),
                   jax.ShapeDtypeStruct((b, lt, LANE), F32)],
        compiler_params=_cp(("parallel", "parallel")),
        name="moe_prenorm_router",
    )(x, gain.reshape(1, d), mods, mods, w_r, b_r)


def _moe_kernel(te_ref, nu_ref, x_ref, rw_ref, w1_ref, w3_ref, w2_ref, o_ref):
    i = pl.program_id(0)

    @pl.when(i < nu_ref[0])
    def _():
        x = x_ref[...]
        hid = _silu(jnp.dot(x, w1_ref[0], preferred_element_type=F32)) * jnp.dot(x, w3_ref[0], preferred_element_type=F32)
        y = jnp.dot(hid.astype(BF16), w2_ref[0], preferred_element_type=F32)
        o_ref[...] = (y * rw_ref[...]).astype(o_ref.dtype)

    @pl.when(i >= nu_ref[0])
    def _():
        o_ref[...] = jnp.zeros_like(o_ref)


def _moe_ffn(xs, rw, tile_expert, n_used, w1, w3, w2):
    rp, d = xs.shape
    tile = MOE_TILE
    ff = w1.shape[2]
    grid_spec = pltpu.PrefetchScalarGridSpec(
        num_scalar_prefetch=2,
        grid=(rp // tile,),
        in_specs=[pl.BlockSpec((tile, d), lambda i, te, nu: (i, 0)),
                  pl.BlockSpec((tile, 1), lambda i, te, nu: (i, 0)),
                  pl.BlockSpec((1, d, ff), lambda i, te, nu: (te[i], 0, 0)),
                  pl.BlockSpec((1, d, ff), lambda i, te, nu: (te[i], 0, 0)),
                  pl.BlockSpec((1, ff, d), lambda i, te, nu: (te[i], 0, 0))],
        out_specs=pl.BlockSpec((tile, d), lambda i, te, nu: (i, 0)),
    )
    return pl.pallas_call(
        _moe_kernel,
        grid_spec=grid_spec,
        out_shape=jax.ShapeDtypeStruct((rp, d), BF16),
        compiler_params=_cp(("arbitrary",)),
        name="moe_experts",
    )(tile_expert, n_used, xs, rw, w1, w3, w2)


def _moe(h2, logits, w1, w3, w2):
    n, d = h2.shape
    tile = MOE_TILE
    g_prob = jax.nn.softmax(logits[:, :MOE_GROUPS], axis=-1)
    g_idx = jnp.argmax(g_prob, axis=-1)
    g_w = jnp.take_along_axis(g_prob, g_idx[:, None], axis=1)
    e_logits = logits[:, MOE_GROUPS:MOE_GROUPS + N_EXPERTS].reshape(n, MOE_GROUPS, MOE_PER_GROUP)
    e_logits = jnp.take_along_axis(e_logits, g_idx[:, None, None], axis=1)[:, 0]
    e_w, e_idx = lax.top_k(jax.nn.softmax(e_logits, axis=-1), 2)
    w = g_w * e_w / jnp.sum(e_w, axis=-1, keepdims=True)
    expert = (g_idx[:, None] * MOE_PER_GROUP + e_idx).astype(jnp.int32)

    flat_e = expert.reshape(-1)
    order = jnp.argsort(flat_e, stable=True).astype(jnp.int32)
    cnt_end = jnp.searchsorted(flat_e[order], jnp.arange(N_EXPERTS, dtype=jnp.int32), side="right").astype(jnp.int32)
    counts = cnt_end - jnp.concatenate([jnp.zeros((1,), jnp.int32), cnt_end[:-1]])
    cnt_start = cnt_end - counts
    padded = ((counts + tile - 1) // tile) * tile
    pad_end = jnp.cumsum(padded)
    pad_start = pad_end - padded
    rp = ((2 * n + tile - 1) // tile) * tile + N_EXPERTS * tile
    tile_start = jnp.arange(rp // tile, dtype=jnp.int32) * tile
    tile_expert = jnp.minimum(jnp.searchsorted(pad_end, tile_start, side="right"), N_EXPERTS - 1).astype(jnp.int32)
    n_used = (pad_end[-1] // tile).astype(jnp.int32).reshape(1)
    row_e = jnp.repeat(tile_expert, tile)
    row_rank = jnp.arange(rp, dtype=jnp.int32) - pad_start[row_e]
    row_valid = row_rank < counts[row_e]
    row_item = order[jnp.clip(cnt_start[row_e] + row_rank, 0, 2 * n - 1)]
    src_token = jnp.where(row_valid, row_item // 2, 0)
    row_w = jnp.where(row_valid, w.reshape(-1)[row_item], 0.0)
    sorted_pos = jnp.argsort(order).astype(jnp.int32)
    dest = (pad_start[flat_e] + sorted_pos - cnt_start[flat_e]).astype(jnp.int32).reshape(n, 2)

    xs = jnp.take(h2, src_token, axis=0)
    ys = _moe_ffn(xs, row_w[:, None], tile_expert, n_used, w1, w3, w2)
    return jnp.take(ys, dest[:, 0], axis=0).astype(F32) + jnp.take(ys, dest[:, 1], axis=0).astype(F32)


def _pack_w_in(w_in):
    depth, d, _ = w_in.shape
    cuts = {}
    off = 0
    for name, width in (("s5", S5_WIDTH), ("gq", GQA_OUT), ("gk", GQA_KV_HEADS * HEAD_DIM), ("gv", GQA_KV_HEADS * HEAD_DIM),
                        ("mq", MLA_Q_RANK), ("mkv", MLA_KV_RANK), ("mkr", MLA_ROPE), ("mz", MB_INNER), ("xbc", MB_XBC),
                        ("dt", 2 * MB_HEADS), ("gate", 4 * D_MODEL)):
        cuts[name] = w_in[:, :, off:off + width]
        off += width
    zeros = lambda n: jnp.zeros((depth, d, n), w_in.dtype)
    dt_blocks = []
    for blk in range(2 * MB_GROUPS):
        dt_blocks += [cuts["dt"][:, :, blk * MB_HPG:(blk + 1) * MB_HPG], zeros(LANE - MB_HPG)]
    packed = jnp.concatenate(
        [cuts["gate"], cuts["gq"], cuts["mz"], cuts["xbc"], cuts["mq"], cuts["s5"], cuts["gk"], cuts["gv"],
         cuts["mkv"], cuts["mkr"], zeros(LANE - MLA_ROPE)] + dt_blocks + [zeros(NW - OFF_DT - 4 * LANE)], axis=-1)
    return packed.astype(BF16)


def _rope_table(n_lat, n_ctx, dim, width):
    rows = n_lat // GRID_W
    row = jnp.repeat(jnp.arange(rows, dtype=F32), GRID_W)
    col = jnp.tile(jnp.arange(GRID_W, dtype=F32), rows)
    quarter = dim // 4
    inv_freq = ROPE_THETA ** (-jnp.arange(quarter, dtype=F32) / quarter)
    ang_r = row[:, None] * inv_freq
    ang_c = col[:, None] * inv_freq
    ang = jnp.concatenate([ang_r, ang_r, ang_c, ang_c], axis=-1)
    cos = jnp.concatenate([jnp.cos(ang), jnp.ones((n_lat, width - dim), F32)], axis=-1)
    sin = jnp.concatenate([jnp.sin(ang), jnp.zeros((n_lat, width - dim), F32)], axis=-1)
    cos = jnp.concatenate([cos, jnp.ones((n_ctx, width), F32)], axis=0)
    sin = jnp.concatenate([sin, jnp.zeros((n_ctx, width), F32)], axis=0)
    return cos, sin


def _rot_matrix(dim, width):
    quarter = dim // 4
    idx = jnp.arange(width)
    in_first = (idx % (2 * quarter)) < quarter
    valid = idx < dim
    src_for_first = idx + quarter
    src_for_second = idx - quarter
    rows = jnp.arange(width)[:, None]
    r = jnp.where(in_first[None, :] & (rows == src_for_first[None, :]), -1.0, 0.0)
    r = r + jnp.where((~in_first)[None, :] & (rows == src_for_second[None, :]), 1.0, 0.0)
    return (r * valid[None, :]).astype(BF16)


def _pad_lanes(v, width):
    return jnp.concatenate([v.astype(F32), jnp.zeros((width - v.shape[0],), F32)]).reshape(1, width)


def _lane_row(lane, value):
    return jnp.zeros((1, LANE), F32).at[0, lane].set(value)


def _score_bound(q_sq_norm, k_sq_norm, scale):
    return (jnp.sqrt(q_sq_norm * k_sq_norm) * (scale * LOG2E * 1.02) + 0.5).astype(F32)


def kernel(x, c, ctx, c_ctx, norm1, norm2, w_ada, b_ada, w_in, s5_lam_re, s5_lam_im, s5_log_step, s5_b_re, s5_b_im, s5_c_re, s5_c_im, s5_d, s5_w_glu, gqa_q_norm, gqa_k_norm, mla_q_norm, mla_kv_norm, mla_w_uq, mla_w_uk, mla_w_uv, mla_qn_norm, mla_kn_norm, mla_qr_norm, mla_kr_norm, mb_conv_w, mb_conv_b, mb_dt_bias, mb_a_log, mb_d, mb_norm, w_br_s5, w_br_gqa, w_br_mla, w_br_mb, w_out, moe_w_group, moe_b_group, moe_w_expert, moe_b_expert, moe_w1, moe_w3, moe_w2):
    b, n_lat, d = x.shape
    n_ctx = ctx.shape[1]
    lt = n_lat + n_ctx
    depth = w_in.shape[0]

    cvec = jnp.zeros((8, d), F32).at[:b].set(c.astype(F32)).at[b].set(c_ctx.astype(F32))
    mod_all = _ada(cvec, w_ada, b_ada).reshape(depth, 8, N_MOD, d)[:, :b + 1]
    mod_all = jnp.concatenate([mod_all, jnp.zeros((depth, b + 1, 8 - N_MOD, d), F32)], axis=2)

    w_in_p = _pack_w_in(w_in)
    cos_g, sin_g = _rope_table(n_lat, n_ctx, HEAD_DIM, HEAD_DIM)
    cos_m, sin_m = _rope_table(n_lat, n_ctx, MLA_ROPE, LANE)
    rot_g = _rot_matrix(HEAD_DIM, HEAD_DIM)
    rot_m = _rot_matrix(MLA_ROPE, LANE)
    ka_g = jnp.broadcast_to(_lane_row(0, 1.0).astype(BF16), (b, lt, LANE))

    uq = mla_w_uq.reshape(depth, MLA_Q_RANK, MLA_HEADS, MLA_NOPE + MLA_ROPE)
    wq_nope = uq[..., :MLA_NOPE].reshape(depth, MLA_Q_RANK, MLA_HEADS * MLA_NOPE).astype(BF16)
    wq_rope = jnp.concatenate([uq[..., MLA_NOPE:], jnp.zeros(uq.shape[:3] + (LANE - MLA_ROPE,), uq.dtype)],
                              axis=-1).reshape(depth, MLA_Q_RANK, MLA_HEADS * LANE).astype(BF16)
    a_dec = -jnp.exp(mb_a_log.astype(F32))

    xs = jnp.concatenate([x.astype(F32), ctx.astype(F32)], axis=1)

    for i in range(depth):
        mods = mod_all[i]
        proj = _inproj(xs, norm1[i], mods, w_in_p[i], n_lat)

        s5p = dict(s5_lam_re=s5_lam_re[i], s5_lam_im=s5_lam_im[i], s5_log_step=s5_log_step[i], s5_b_re=s5_b_re[i],
                   s5_b_im=s5_b_im[i], s5_c_re=s5_c_re[i], s5_c_im=s5_c_im[i])
        ya = _s5_mixer(proj, _s5_operators(s5p), s5_d[i].astype(F32), s5_w_glu[i].astype(BF16), n_lat)

        shift_g = _score_bound(HEAD_DIM * jnp.max(jnp.abs(gqa_q_norm[i])) ** 2,
                               HEAD_DIM * jnp.max(jnp.abs(gqa_k_norm[i])) ** 2, HEAD_DIM ** -0.5)
        qg, kg = _gqa_prep(proj, cos_g, sin_g, rot_g, gqa_q_norm[i].astype(F32), gqa_k_norm[i].astype(F32),
                           _lane_row(0, -shift_g))
        yb = _attention(qg, kg, ka_g, proj, shift_g, n_kv_heads=GQA_KV_HEADS, grp=GQA_HEADS // GQA_KV_HEADS,
                        n_lat=n_lat, v_col0=OFF_GV // HEAD_DIM, tq_lat=1024, tk_lat=3328)

        mp = dict(mla_q_norm=mla_q_norm[i].astype(F32), mla_kv_norm=mla_kv_norm[i].astype(F32),
                  wq_nope=wq_nope[i], wq_rope=wq_rope[i], w_uk=mla_w_uk[i].astype(BF16), w_uv=mla_w_uv[i].astype(BF16),
                  mla_qn_norm=mla_qn_norm[i].astype(F32), mla_kn_norm=mla_kn_norm[i].astype(F32),
                  qr_gain=_pad_lanes(mla_qr_norm[i], LANE), kr_gain=_pad_lanes(mla_kr_norm[i], LANE))
        shift_m = _score_bound(
            MLA_NOPE * jnp.max(jnp.abs(mla_qn_norm[i])) ** 2 + MLA_ROPE * jnp.max(jnp.abs(mla_qr_norm[i])) ** 2,
            MLA_NOPE * jnp.max(jnp.abs(mla_kn_norm[i])) ** 2 + MLA_ROPE * jnp.max(jnp.abs(mla_kr_norm[i])) ** 2,
            (MLA_NOPE + MLA_ROPE) ** -0.5)
        mp["q_shift_row"] = _lane_row(MLA_ROPE, -shift_m)
        mp["k_one_row"] = _lane_row(MLA_ROPE, 1.0)
        qm, kn, kr, vm = _mla_prep(proj, cos_m, sin_m, rot_m, mp)
        yc = _attention(qm, kn, kr, vm, shift_m, n_kv_heads=MLA_HEADS, grp=1,
                        n_lat=n_lat, v_col0=0, tq_lat=4096, tk_lat=3328)

        xbc = _mamba_conv(proj, mb_conv_w[i], mb_conv_b[i], n_lat)
        dt_bias4 = jnp.concatenate([mb_dt_bias[i].astype(F32).reshape(2 * MB_GROUPS, MB_HPG),
                                    jnp.zeros((2 * MB_GROUPS, LANE - MB_HPG), F32)], axis=-1).reshape(2 * MB_GROUPS, 1, LANE)
        a4 = jnp.concatenate([a_dec[i].reshape(2 * MB_GROUPS, MB_HPG),
                              jnp.zeros((2 * MB_GROUPS, LANE - MB_HPG), F32)], axis=-1).reshape(2 * MB_GROUPS, 1, LANE)
        y2 = _ssd(xbc, proj, dt_bias4, a4, n_lat)
        d_vec = jnp.repeat(mb_d[i].astype(F32), MB_HEAD_DIM).reshape(1, MB_INNER)
        yd = _mamba_finish(y2, xbc, proj, d_vec, mb_norm[i].astype(F32))

        merged = _merge((ya, yb, yc, yd), proj,
                        (w_br_s5[i].astype(BF16), w_br_gqa[i].astype(BF16), w_br_mla[i].astype(BF16), w_br_mb[i].astype(BF16)))
        xs = _outproj(merged, w_out[i].astype(BF16), xs, mods, n_lat)

        w_r = jnp.concatenate([moe_w_group[i], moe_w_expert[i],
                               jnp.zeros((d, LANE - MOE_GROUPS - N_EXPERTS), F32)], axis=-1).astype(BF16)
        b_r = _pad_lanes(jnp.concatenate([moe_b_group[i], moe_b_expert[i]]), LANE)
        h2, logits = _prenorm_router(xs, norm2[i], mods, w_r, b_r, n_lat)
        ff = _moe(h2.reshape(b * lt, d), logits.reshape(b * lt, LANE),
                  moe_w1[i].astype(BF16), moe_w3[i].astype(BF16), moe_w2[i].astype(BF16)).reshape(b, lt, d)
        gate2 = jnp.concatenate([jnp.broadcast_to(mods[:b, 5][:, None, :], (b, n_lat, d)),
                                 jnp.broadcast_to(mods[b, 5][None, None, :], (b, n_ctx, d))], axis=1)
        xs = xs + gate2 * ff

    return xs[:, :n_lat].astype(x.dtype)
```

```python
import functools
import math

import jax
import jax.numpy as jnp
from jax import lax
from jax.experimental import pallas as pl
from jax.experimental.pallas import tpu as pltpu

F32 = jnp.float32
BF16 = jnp.bfloat16
HIGHEST = lax.Precision.HIGHEST

EPS = 1e-6
ROPE_THETA = 10000.0
GRID_W = 64
D_MODEL = 2048
N_MOD = 6

S5_GROUP = 16
S5_WIDTH = 768
S5_GROUPS = S5_WIDTH // S5_GROUP
S5_STATE = 64
S5_CHUNK = 16

GQA_HEADS = 8
GQA_KV_HEADS = 2
HEAD_DIM = 128
GQA_OUT = GQA_HEADS * HEAD_DIM

MLA_HEADS = 8
MLA_Q_RANK = 512
MLA_KV_RANK = 256
MLA_NOPE = 128
MLA_ROPE = 64
MLA_OUT = MLA_HEADS * HEAD_DIM

MB_HEADS = 16
MB_HEAD_DIM = 64
MB_INNER = MB_HEADS * MB_HEAD_DIM
MB_GROUPS = 2
MB_HPG = MB_HEADS // MB_GROUPS
MB_STATE = 128
MB_CONV = 5
MB_XBC = MB_INNER + 2 * MB_GROUPS * MB_STATE
SSD_CHUNK = 128

MOE_GROUPS = 4
MOE_PER_GROUP = 8
N_EXPERTS = MOE_GROUPS * MOE_PER_GROUP
MOE_FF = 512
MOE_TILE = 512

LANE = 128
VMEM_LIMIT = 56 * 1024 * 1024
LOG2E = math.log2(math.e)
FLASH_ROW_BLOCK = 256
FLASH_DQ = 256
FLASH_MAX_SHIFT = 60.0

OFF_GATE = 0
OFF_GQ = 8192
OFF_MZ = 9216
OFF_XBC = 10240
OFF_MQ = 11776
OFF_S5 = 12288
OFF_GK = 13056
OFF_GV = 13312
OFF_MKV = 13568
OFF_MKR = 13824
OFF_DT = 13952
NW = 14592


def _tile(n, *cands):
    for c in cands:
        if n % c == 0:
            return c
    return n


def _cp(sem, vmem=VMEM_LIMIT):
    return pltpu.CompilerParams(dimension_semantics=sem, vmem_limit_bytes=vmem)


def _sigmoid(x):
    return 1.0 / (1.0 + jnp.exp(-x))


def _silu(x):
    return x * _sigmoid(x)


def _ada_kernel(c_ref, w_ref, b_ref, o_ref):
    s = _silu(c_ref[...])
    o_ref[0] = jnp.dot(s.astype(BF16), w_ref[0].astype(BF16), preferred_element_type=F32) + b_ref[0]


def _ada(cvec, w_ada, b_ada):
    depth, d, n = w_ada.shape
    tn = _tile(n, 1024, 512)
    return pl.pallas_call(
        _ada_kernel,
        grid=(depth, n // tn),
        in_specs=[pl.BlockSpec((8, d), lambda l, j: (0, 0)),
                  pl.BlockSpec((1, d, tn), lambda l, j: (l, 0, j)),
                  pl.BlockSpec((1, 1, tn), lambda l, j: (l, 0, j))],
        out_specs=pl.BlockSpec((1, 8, tn), lambda l, j: (l, 0, j)),
        out_shape=jax.ShapeDtypeStruct((depth, 8, n), F32),
        compiler_params=_cp(("parallel", "parallel")),
        name="ada_mod",
    )(cvec, w_ada, b_ada.reshape(depth, 1, n))


ROW_CHUNK = 128


def _norm_mod(x, g, ml, mc, row_start, n_lat, row0):
    r = lax.rsqrt(jnp.mean(x * x, axis=-1, keepdims=True) + EPS)
    rows = row_start + lax.broadcasted_iota(jnp.int32, (x.shape[0], 1), 0)
    is_ctx = rows >= n_lat
    shift = jnp.where(is_ctx, mc[row0:row0 + 1], ml[row0:row0 + 1])
    scale = jnp.where(is_ctx, mc[row0 + 1:row0 + 2], ml[row0 + 1:row0 + 2])
    return x * r * g * (1.0 + scale) + shift


def _inproj_kernel(x_ref, g_ref, ml_ref, mc_ref, w_ref, o_ref, h_sc, *, n_lat, tm):
    @pl.when(pl.program_id(2) == 0)
    def _():
        def chunk(r, carry):
            rs = pl.multiple_of(r * ROW_CHUNK, ROW_CHUNK)
            h = _norm_mod(x_ref[0, pl.ds(rs, ROW_CHUNK), :], g_ref[...], ml_ref[0], mc_ref[0],
                          pl.program_id(1) * tm + rs, n_lat, 0)
            h_sc[pl.ds(rs, ROW_CHUNK), :] = h.astype(BF16)
            return carry

        lax.fori_loop(0, tm // ROW_CHUNK, chunk, 0)

    o_ref[0] = jnp.dot(h_sc[...], w_ref[...], preferred_element_type=F32).astype(o_ref.dtype)


def _inproj(x, gain, mods, w, n_lat):
    b, lt, d = x.shape
    n = w.shape[1]
    tm = _tile(lt, 1280, 256)
    tn = _tile(n, 768)
    return pl.pallas_call(
        functools.partial(_inproj_kernel, n_lat=n_lat, tm=tm),
        grid=(b, lt // tm, n // tn),
        in_specs=[pl.BlockSpec((1, tm, d), lambda bi, i, j: (bi, i, 0)),
                  pl.BlockSpec((1, d), lambda bi, i, j: (0, 0)),
                  pl.BlockSpec((1, 8, d), lambda bi, i, j: (bi, 0, 0)),
                  pl.BlockSpec((1, 8, d), lambda bi, i, j: (b, 0, 0)),
                  pl.BlockSpec((d, tn), lambda bi, i, j: (0, j))],
        out_specs=pl.BlockSpec((1, tm, tn), lambda bi, i, j: (bi, i, j)),
        out_shape=jax.ShapeDtypeStruct((b, lt, n), BF16),
        scratch_shapes=[pltpu.VMEM((tm, d), BF16)],
        compiler_params=_cp(("parallel", "parallel", "arbitrary")),
        name="in_proj",
    )(x, gain.reshape(1, d), mods, mods, w)


def _head_norm_rope(x, gain, cos, sin, rmat, n_valid, scale):
    ms = jnp.sum(x * x, axis=-1, keepdims=True) * (1.0 / n_valid)
    y = x * lax.rsqrt(ms + EPS) * gain
    if rmat is not None:
        rot = jnp.dot(y.astype(BF16), rmat, preferred_element_type=F32)
        y = y * cos + rot * sin
    return y * scale


def _gqa_prep_kernel(q_ref, k_ref, cos_ref, sin_ref, r_ref, qg_ref, kg_ref, sh_ref, qo_ref, ko_ref):
    cos, sin, rmat = cos_ref[...], sin_ref[...], r_ref[...]
    scale = HEAD_DIM ** -0.5 * LOG2E
    aux = jnp.broadcast_to(sh_ref[...], (q_ref.shape[1], LANE)).astype(BF16)
    for h in range(GQA_HEADS):
        x = q_ref[0, :, h * HEAD_DIM:(h + 1) * HEAD_DIM].astype(F32)
        qo_ref[0, :, h * FLASH_DQ:h * FLASH_DQ + HEAD_DIM] = _head_norm_rope(
            x, qg_ref[...], cos, sin, rmat, HEAD_DIM, scale).astype(BF16)
        qo_ref[0, :, h * FLASH_DQ + HEAD_DIM:(h + 1) * FLASH_DQ] = aux
    for h in range(GQA_KV_HEADS):
        x = k_ref[0, :, h * HEAD_DIM:(h + 1) * HEAD_DIM].astype(F32)
        ko_ref[0, :, h * HEAD_DIM:(h + 1) * HEAD_DIM] = _head_norm_rope(
            x, kg_ref[...], cos, sin, rmat, HEAD_DIM, 1.0).astype(BF16)


def _gqa_prep(proj, cos, sin, rmat, q_gain, k_gain, shift_row):
    b, lt, _ = proj.shape
    tm = _tile(lt, 640, 256)
    kw = GQA_KV_HEADS * HEAD_DIM
    qw = GQA_HEADS * FLASH_DQ
    return pl.pallas_call(
        _gqa_prep_kernel,
        grid=(b, lt // tm),
        in_specs=[pl.BlockSpec((1, tm, GQA_OUT), lambda bi, i: (bi, i, OFF_GQ // GQA_OUT)),
                  pl.BlockSpec((1, tm, kw), lambda bi, i: (bi, i, OFF_GK // kw)),
                  pl.BlockSpec((tm, HEAD_DIM), lambda bi, i: (i, 0)),
                  pl.BlockSpec((tm, HEAD_DIM), lambda bi, i: (i, 0)),
                  pl.BlockSpec((HEAD_DIM, HEAD_DIM), lambda bi, i: (0, 0)),
                  pl.BlockSpec((1, HEAD_DIM), lambda bi, i: (0, 0)),
                  pl.BlockSpec((1, HEAD_DIM), lambda bi, i: (0, 0)),
                  pl.BlockSpec((1, LANE), lambda bi, i: (0, 0))],
        out_specs=[pl.BlockSpec((1, tm, qw), lambda bi, i: (bi, i, 0)),
                   pl.BlockSpec((1, tm, kw), lambda bi, i: (bi, i, 0))],
        out_shape=[jax.ShapeDtypeStruct((b, lt, qw), BF16),
                   jax.ShapeDtypeStruct((b, lt, kw), BF16)],
        compiler_params=_cp(("parallel", "parallel")),
        name="gqa_prep",
    )(proj, proj, cos, sin, rmat, q_gain.reshape(1, -1), k_gain.reshape(1, -1), shift_row)


def _mla_prep_kernel(cq_ref, ckv_ref, kr_ref, cos_ref, sin_ref, r_ref,
                     qn_g_ref, kvn_g_ref, wqn_ref, wqr_ref, wuk_ref, wuv_ref,
                     qnn_g_ref, qrn_g_ref, knn_g_ref, krn_g_ref, qsh_ref, kone_ref,
                     q_ref, kn_ref, kr_out_ref, v_ref):
    cos, sin, rmat = cos_ref[...], sin_ref[...], r_ref[...]
    scale = (MLA_NOPE + MLA_ROPE) ** -0.5 * LOG2E
    cq = cq_ref[0].astype(F32)
    cqn = (cq * lax.rsqrt(jnp.mean(cq * cq, axis=-1, keepdims=True) + EPS) * qn_g_ref[...]).astype(BF16)
    q_nope = jnp.dot(cqn, wqn_ref[...], preferred_element_type=F32)
    q_rope = jnp.dot(cqn, wqr_ref[...], preferred_element_type=F32)
    for h in range(MLA_HEADS):
        sl = slice(h * HEAD_DIM, (h + 1) * HEAD_DIM)
        qn = _head_norm_rope(q_nope[:, sl], qnn_g_ref[...], None, None, None, MLA_NOPE, scale)
        qr = _head_norm_rope(q_rope[:, sl], qrn_g_ref[...], cos, sin, rmat, MLA_ROPE, scale)
        q_ref[0, :, 2 * h * HEAD_DIM:(2 * h + 1) * HEAD_DIM] = qn.astype(BF16)
        q_ref[0, :, (2 * h + 1) * HEAD_DIM:(2 * h + 2) * HEAD_DIM] = (qr + qsh_ref[...]).astype(BF16)
    ckv = ckv_ref[0].astype(F32)
    lat = (ckv * lax.rsqrt(jnp.mean(ckv * ckv, axis=-1, keepdims=True) + EPS) * kvn_g_ref[...]).astype(BF16)
    k_nope = jnp.dot(lat, wuk_ref[...], preferred_element_type=F32)
    v_ref[0] = jnp.dot(lat, wuv_ref[...], preferred_element_type=F32).astype(BF16)
    for h in range(MLA_HEADS):
        sl = slice(h * HEAD_DIM, (h + 1) * HEAD_DIM)
        kn_ref[0, :, sl] = _head_norm_rope(k_nope[:, sl], knn_g_ref[...], None, None, None, MLA_NOPE, 1.0).astype(BF16)
    kr = kr_ref[0].astype(F32)
    kr_out_ref[0] = (_head_norm_rope(kr, krn_g_ref[...], cos, sin, rmat, MLA_ROPE, 1.0) + kone_ref[...]).astype(BF16)


def _mla_prep(proj, cos, sin, rmat, p):
    b, lt, _ = proj.shape
    tm = _tile(lt, 640, 256)
    hd = MLA_HEADS * HEAD_DIM
    full = lambda shape: pl.BlockSpec(shape, lambda bi, i: tuple(0 for _ in shape))
    return pl.pallas_call(
        _mla_prep_kernel,
        grid=(b, lt // tm),
        in_specs=[pl.BlockSpec((1, tm, MLA_Q_RANK), lambda bi, i: (bi, i, OFF_MQ // MLA_Q_RANK)),
                  pl.BlockSpec((1, tm, MLA_KV_RANK), lambda bi, i: (bi, i, OFF_MKV // MLA_KV_RANK)),
                  pl.BlockSpec((1, tm, LANE), lambda bi, i: (bi, i, OFF_MKR // LANE)),
                  pl.BlockSpec((tm, LANE), lambda bi, i: (i, 0)),
                  pl.BlockSpec((tm, LANE), lambda bi, i: (i, 0)),
                  full((LANE, LANE)),
                  full((1, MLA_Q_RANK)), full((1, MLA_KV_RANK)),
                  full((MLA_Q_RANK, hd)), full((MLA_Q_RANK, hd)),
                  full((MLA_KV_RANK, hd)), full((MLA_KV_RANK, hd)),
                  full((1, LANE)), full((1, LANE)), full((1, LANE)), full((1, LANE)),
                  full((1, LANE)), full((1, LANE))],
        out_specs=[pl.BlockSpec((1, tm, 2 * hd), lambda bi, i: (bi, i, 0)),
                   pl.BlockSpec((1, tm, hd), lambda bi, i: (bi, i, 0)),
                   pl.BlockSpec((1, tm, LANE), lambda bi, i: (bi, i, 0)),
                   pl.BlockSpec((1, tm, hd), lambda bi, i: (bi, i, 0))],
        out_shape=[jax.ShapeDtypeStruct((b, lt, 2 * hd), BF16),
                   jax.ShapeDtypeStruct((b, lt, hd), BF16),
                   jax.ShapeDtypeStruct((b, lt, LANE), BF16),
                   jax.ShapeDtypeStruct((b, lt, hd), BF16)],
        compiler_params=_cp(("parallel", "parallel")),
        name="mla_prep",
    )(proj, proj, proj, cos, sin, rmat,
      p["mla_q_norm"].reshape(1, -1), p["mla_kv_norm"].reshape(1, -1),
      p["wq_nope"], p["wq_rope"], p["w_uk"], p["w_uv"],
      p["mla_qn_norm"].reshape(1, -1), p["qr_gain"], p["mla_kn_norm"].reshape(1, -1), p["kr_gain"],
      p["q_shift_row"], p["k_one_row"])


def _stack_queries(q_ref, q_sc, grp, tq):
    for g in range(grp):
        q_sc[g * tq:(g + 1) * tq, :] = q_ref[0, :, g * FLASH_DQ:(g + 1) * FLASH_DQ]


def _write_heads(o_ref, o, grp, tq):
    for g in range(grp):
        o_ref[0, :, g * HEAD_DIM:(g + 1) * HEAD_DIM] = o[g * tq:(g + 1) * tq].astype(o_ref.dtype)


def _flash_online_kernel(q_ref, k_ref, ka_ref, v_ref, o_ref, q_sc, m_sc, l_sc, acc_sc, *, grp, tq):
    kv = pl.program_id(3)

    @pl.when(kv == 0)
    def _():
        _stack_queries(q_ref, q_sc, grp, tq)
        m_sc[...] = jnp.full_like(m_sc, -jnp.inf)
        l_sc[...] = jnp.zeros_like(l_sc)
        acc_sc[...] = jnp.zeros_like(acc_sc)

    k = jnp.concatenate([k_ref[0], ka_ref[0]], axis=-1)
    v = v_ref[0]
    n_chunks = k.shape[0] // LANE
    rb = min(FLASH_ROW_BLOCK, grp * tq)
    for r in range(grp * tq // rb):
        rows = slice(r * rb, (r + 1) * rb)
        s = lax.dot_general(q_sc[rows, :], k, (((1,), (1,)), ((), ())), preferred_element_type=F32)
        m_prev = m_sc[rows, :]
        m_new = jnp.maximum(m_prev, jnp.max(s, axis=-1, keepdims=True))
        alpha = jnp.exp2(m_prev - m_new)
        p = jnp.exp2(s - jnp.concatenate([m_new] * n_chunks, axis=-1))
        p_sum = p[:, :LANE]
        for c in range(1, n_chunks):
            p_sum = p_sum + p[:, c * LANE:(c + 1) * LANE]
        l_sc[rows, :] = alpha * l_sc[rows, :] + p_sum
        acc_sc[rows, :] = alpha * acc_sc[rows, :] + jnp.dot(p.astype(BF16), v, preferred_element_type=F32)
        m_sc[rows, :] = m_new

    @pl.when(kv == pl.num_programs(3) - 1)
    def _():
        _write_heads(o_ref, acc_sc[...] / jnp.sum(l_sc[...], axis=-1, keepdims=True), grp, tq)


def _flash_bounded_kernel(q_ref, k_ref, ka_ref, v_ref, o_ref, q_sc, acc_sc, *, grp, tq):
    kv = pl.program_id(3)

    @pl.when(kv == 0)
    def _():
        _stack_queries(q_ref, q_sc, grp, tq)
        acc_sc[...] = jnp.zeros_like(acc_sc)

    k = jnp.concatenate([k_ref[0], ka_ref[0]], axis=-1)
    v = v_ref[0]
    ones_col = (lax.broadcasted_iota(jnp.int32, v.shape, 1) == 0).astype(BF16)
    v1 = jnp.concatenate([v, ones_col], axis=-1)
    rb = min(FLASH_ROW_BLOCK, grp * tq)
    for r in range(grp * tq // rb):
        rows = slice(r * rb, (r + 1) * rb)
        s = lax.dot_general(q_sc[rows, :], k, (((1,), (1,)), ((), ())), preferred_element_type=F32)
        acc_sc[rows, :] += jnp.dot(jnp.exp2(s).astype(BF16), v1, preferred_element_type=F32)

    @pl.when(kv == pl.num_programs(3) - 1)
    def _():
        acc = acc_sc[...]
        _write_heads(o_ref, acc[:, :HEAD_DIM] / acc[:, HEAD_DIM:HEAD_DIM + 1], grp, tq)


def _flash(q, k, ka, v, *, bounded, n_kv_heads, grp, tq, tk, q_rows, q_off, kv_rows, kv_off, v_col0):
    b = q.shape[0]
    qo, ko = q_off // tq, kv_off // tk
    rows = grp * tq
    if bounded:
        body = _flash_bounded_kernel
        scratch = [pltpu.VMEM((rows, FLASH_DQ), BF16), pltpu.VMEM((rows, 2 * HEAD_DIM), F32)]
    else:
        body = _flash_online_kernel
        scratch = [pltpu.VMEM((rows, FLASH_DQ), BF16), pltpu.VMEM((rows, LANE), F32),
                   pltpu.VMEM((rows, LANE), F32), pltpu.VMEM((rows, HEAD_DIM), F32)]
    return pl.pallas_call(
        functools.partial(body, grp=grp, tq=tq),
        grid=(b, n_kv_heads, q_rows // tq, kv_rows // tk),
        in_specs=[pl.BlockSpec((1, tq, grp * FLASH_DQ), lambda bi, h, i, j: (bi, i + qo, h)),
                  pl.BlockSpec((1, tk, HEAD_DIM), lambda bi, h, i, j: (bi, j + ko, h)),
                  pl.BlockSpec((1, tk, LANE), lambda bi, h, i, j: (bi, j + ko, 0)),
                  pl.BlockSpec((1, tk, HEAD_DIM), lambda bi, h, i, j: (bi, j + ko, v_col0 + h))],
        out_specs=pl.BlockSpec((1, tq, grp * HEAD_DIM), lambda bi, h, i, j: (bi, i, h)),
        out_shape=jax.ShapeDtypeStruct((b, q_rows, n_kv_heads * grp * HEAD_DIM), BF16),
        scratch_shapes=scratch,
        compiler_params=_cp(("parallel", "parallel", "parallel", "arbitrary")),
        name="flash_bounded" if bounded else "flash_online",
    )(q, k, ka, v)


def _attention(q, k, ka, v, shift, *, n_kv_heads, grp, n_lat, v_col0, tq_lat, tk_lat):
    lt = q.shape[1]
    n_ctx = lt - n_lat
    tk = _tile(lt, tk_lat, 256)
    tq = _tile(n_lat, tq_lat, 256)

    def run(bounded):
        def go(q, k, ka, v):
            y_lat = _flash(q, k, ka, v, bounded=bounded, n_kv_heads=n_kv_heads, grp=grp, tq=tq, tk=tk,
                           q_rows=n_lat, q_off=0, kv_rows=lt, kv_off=0, v_col0=v_col0)
            y_ctx = _flash(q, k, ka, v, bounded=bounded, n_kv_heads=n_kv_heads, grp=grp, tq=n_ctx, tk=n_ctx,
                           q_rows=n_ctx, q_off=n_lat, kv_rows=n_ctx, kv_off=n_lat, v_col0=v_col0)
            return jnp.concatenate([y_lat, y_ctx], axis=1)
        return go

    return lax.cond(shift <= FLASH_MAX_SHIFT, run(True), run(False), q, k, ka, v)


def _conv_kernel(prev_ref, cur_ref, next_ref, w_ref, b_ref, o_ref, *, tr, n_lat, lt):
    i = pl.program_id(1)
    start = i * tr
    has_prev = jnp.logical_and(start != 0, start != n_lat).astype(F32)
    has_next = jnp.logical_and(start + tr != n_lat, start + tr != lt).astype(F32)
    ext = jnp.concatenate([prev_ref[0].astype(F32) * has_prev, cur_ref[0].astype(F32),
                           next_ref[0].astype(F32) * has_next], axis=0)
    w = w_ref[...]
    acc = jnp.zeros((tr, cur_ref.shape[2]), F32) + b_ref[...]
    half = MB_CONV // 2
    for kk in range(MB_CONV):
        acc = acc + ext[8 + kk - half:8 + kk - half + tr, :] * w[kk:kk + 1, :]
    o_ref[0] = _silu(acc).astype(o_ref.dtype)


def _mamba_conv(proj, conv_w, conv_b, n_lat):
    b, lt, _ = proj.shape
    tr = _tile(n_lat, 256)
    cw = 512
    cb0 = OFF_XBC // cw
    r8 = tr // 8
    nblk8 = lt // 8
    w8 = jnp.zeros((8, MB_XBC), F32).at[:MB_CONV].set(conv_w.T.astype(F32))
    return pl.pallas_call(
        functools.partial(_conv_kernel, tr=tr, n_lat=n_lat, lt=lt),
        grid=(b, lt // tr, MB_XBC // cw),
        in_specs=[pl.BlockSpec((1, 8, cw), lambda bi, i, c: (bi, jnp.maximum(i * r8 - 1, 0), cb0 + c)),
                  pl.BlockSpec((1, tr, cw), lambda bi, i, c: (bi, i, cb0 + c)),
                  pl.BlockSpec((1, 8, cw), lambda bi, i, c: (bi, jnp.minimum((i + 1) * r8, nblk8 - 1), cb0 + c)),
                  pl.BlockSpec((8, cw), lambda bi, i, c: (0, c)),
                  pl.BlockSpec((1, cw), lambda bi, i, c: (0, c))],
        out_specs=pl.BlockSpec((1, tr, cw), lambda bi, i, c: (bi, i, c)),
        out_shape=jax.ShapeDtypeStruct((b, lt, MB_XBC), BF16),
        compiler_params=_cp(("parallel", "parallel", "parallel")),
        name="mamba_conv",
    )(proj, proj, proj, w8, conv_b.reshape(1, -1).astype(F32))


def _softplus(x):
    return jnp.maximum(x, 0.0) + jnp.log1p(jnp.exp(-jnp.abs(x)))


def _ssd_kernel(xs_ref, bm_ref, cm_ref, dt_ref, bias_ref, a_ref, y_ref, h_sc):
    t = SSD_CHUNK
    d = pl.program_id(2)

    @pl.when(pl.program_id(3) == 0)
    def _():
        h_sc[...] = jnp.zeros_like(h_sc)

    dt = _softplus(dt_ref[0].astype(F32) + bias_ref[0])
    a = dt * a_ref[0]
    row = lax.broadcasted_iota(jnp.int32, (t, t), 0)
    col = lax.broadcasted_iota(jnp.int32, (t, t), 1)
    sgn = 1 - 2 * d
    mask = (row - col) * sgn >= 0
    tri = mask.astype(BF16)
    a_hi = a.astype(BF16)
    a_lo = (a - a_hi.astype(F32)).astype(BF16)
    cum = jnp.dot(tri, a_hi, preferred_element_type=F32) + jnp.dot(tri, a_lo, preferred_element_type=F32)
    total = jnp.sum(a, axis=0, keepdims=True)
    cum_t = cum.T
    e_cum = jnp.exp(cum)
    e_end = jnp.exp(total - cum)
    e_tot = jnp.exp(total)

    bm = bm_ref[0]
    cm = cm_ref[0]
    scores = lax.dot_general(cm, bm, (((1,), (1,)), ((), ())), preferred_element_type=F32)
    bm_t = bm.astype(F32).T.astype(BF16)
    lane = lax.broadcasted_iota(jnp.int32, (t, LANE), 1)
    first = lane < MB_HEAD_DIM
    width = MB_HPG * MB_HEAD_DIM
    expand = (lax.broadcasted_iota(jnp.int32, (LANE, width), 1) // MB_HEAD_DIM
              == lax.broadcasted_iota(jnp.int32, (LANE, width), 0)).astype(BF16)
    stack = jnp.concatenate([dt, e_cum, e_end, jnp.broadcast_to(e_tot, (8, LANE))], axis=0)
    s_hi = stack.astype(BF16)
    s_lo = (stack - s_hi.astype(F32)).astype(BF16)
    wide = (jnp.dot(s_hi, expand, preferred_element_type=F32) + jnp.dot(s_lo, expand, preferred_element_type=F32))
    dt_w, e_cum_w, e_end_w, e_tot_w = wide[:t], wide[t:2 * t], wide[2 * t:3 * t], wide[3 * t:3 * t + 1]

    for m in range(MB_HPG // 2):
        j = 2 * m
        cols = slice(m * LANE, (m + 1) * LANE)
        xdt = xs_ref[0, :, cols].astype(F32) * dt_w[:, cols]
        h = h_sc[:, cols]
        y = jnp.dot(cm, h.astype(BF16), preferred_element_type=F32) * e_cum_w[:, cols]
        xw = (xdt * e_end_w[:, cols]).astype(BF16)
        h_sc[:, cols] = h * e_tot_w[:, cols] + jnp.dot(bm_t, xw, preferred_element_type=F32)
        xdt_b = xdt.astype(BF16)
        for q in range(2):
            diff = cum[:, j + q:j + q + 1] - cum_t[j + q:j + q + 1, :]
            decay = jnp.where(mask, jnp.exp(jnp.minimum(diff, 0.0)), 0.0)
            keep = first if q == 0 else jnp.logical_not(first)
            y = y + jnp.dot((scores * decay).astype(BF16), jnp.where(keep, xdt_b, jnp.zeros_like(xdt_b)),
                            preferred_element_type=F32)
        y_ref[0, 0, :, cols] = y


def _ssd(xbc, proj, dt_bias4, a4, n_lat):
    b, lt, _ = xbc.shape
    t = SSD_CHUNK
    n_lat_c = n_lat // t
    n_ctx_c = (lt - n_lat) // t
    nc = n_lat_c + n_ctx_c
    width = MB_HPG * MB_HEAD_DIM

    def chunk(d, i):
        fwd = jnp.where(i < n_ctx_c, n_lat_c + i, i - n_ctx_c)
        bwd = jnp.where(i < n_ctx_c, n_lat_c + n_ctx_c - 1 - i, n_lat_c - 1 - (i - n_ctx_c))
        return jnp.where(d == 0, fwd, bwd)

    return pl.pallas_call(
        _ssd_kernel,
        grid=(b, MB_GROUPS, 2, nc),
        in_specs=[pl.BlockSpec((1, t, width), lambda bi, g, d, i: (bi, chunk(d, i), g)),
                  pl.BlockSpec((1, t, MB_STATE), lambda bi, g, d, i: (bi, chunk(d, i), MB_INNER // MB_STATE + g)),
                  pl.BlockSpec((1, t, MB_STATE), lambda bi, g, d, i: (bi, chunk(d, i), MB_INNER // MB_STATE + MB_GROUPS + g)),
                  pl.BlockSpec((1, t, LANE), lambda bi, g, d, i: (bi, chunk(d, i), OFF_DT // LANE + d * MB_GROUPS + g)),
                  pl.BlockSpec((1, 1, LANE), lambda bi, g, d, i: (d * MB_GROUPS + g, 0, 0)),
                  pl.BlockSpec((1, 1, LANE), lambda bi, g, d, i: (d * MB_GROUPS + g, 0, 0))],
        out_specs=pl.BlockSpec((1, 1, t, width), lambda bi, g, d, i: (bi, d, chunk(d, i), g)),
        out_shape=jax.ShapeDtypeStruct((b, 2, lt, MB_INNER), F32),
        scratch_shapes=[pltpu.VMEM((MB_STATE, width), F32)],
        compiler_params=_cp(("parallel", "parallel", "parallel", "arbitrary")),
        name="ssd_scan",
    )(xbc, xbc, xbc, proj, dt_bias4, a4)


def _mamba_finish_kernel(yf_ref, yb_ref, xs_ref, z_ref, d_ref, g_ref, o_ref):
    y = yf_ref[0, 0] + yb_ref[0, 0] + xs_ref[0].astype(F32) * d_ref[...]
    y = y * _silu(z_ref[0].astype(F32))
    o_ref[0] = (y * lax.rsqrt(jnp.mean(y * y, axis=-1, keepdims=True) + EPS) * g_ref[...]).astype(o_ref.dtype)


def _mamba_finish(y2, xbc, proj, d_vec, gain):
    b, lt, _ = xbc.shape
    tm = _tile(lt, 640, 256)
    w = MB_INNER
    return pl.pallas_call(
        _mamba_finish_kernel,
        grid=(b, lt // tm),
        in_specs=[pl.BlockSpec((1, 1, tm, w), lambda bi, i: (bi, 0, i, 0)),
                  pl.BlockSpec((1, 1, tm, w), lambda bi, i: (bi, 1, i, 0)),
                  pl.BlockSpec((1, tm, w), lambda bi, i: (bi, i, 0)),
                  pl.BlockSpec((1, tm, w), lambda bi, i: (bi, i, OFF_MZ // w)),
                  pl.BlockSpec((1, w), lambda bi, i: (0, 0)),
                  pl.BlockSpec((1, w), lambda bi, i: (0, 0))],
        out_specs=pl.BlockSpec((1, tm, w), lambda bi, i: (bi, i, 0)),
        out_shape=jax.ShapeDtypeStruct((b, lt, w), BF16),
        compiler_params=_cp(("parallel", "parallel")),
        name="mamba_finish",
    )(y2, y2, xbc, proj, d_vec, gain.reshape(1, -1))


def _s5_in_kernel(u_ref, w_ref, yi_ref, sf_ref, sfs_ref, sb_ref, sbs_ref):
    r = jnp.dot(u_ref[0], w_ref[0], preferred_element_type=F32)
    k = S5_CHUNK * S5_GROUP
    w = 2 * S5_STATE
    yi_ref[0] = r[:, :k]
    for n, ref in enumerate((sf_ref, sfs_ref, sb_ref, sbs_ref)):
        ref[...] = r[:, k + n * w:k + (n + 1) * w]


def _s5_in(u_g, w_g):
    g, m, k = u_g.shape
    n = w_g.shape[2]
    w = 2 * S5_STATE
    state_spec = pl.BlockSpec((m, w), lambda gi: (0, gi))
    state_shape = jax.ShapeDtypeStruct((m, g * w), F32)
    return pl.pallas_call(
        _s5_in_kernel,
        grid=(g,),
        in_specs=[pl.BlockSpec((1, m, k), lambda gi: (gi, 0, 0)),
                  pl.BlockSpec((1, k, n), lambda gi: (gi, 0, 0))],
        out_specs=[pl.BlockSpec((1, m, k), lambda gi: (gi, 0, 0))] + [state_spec] * 4,
        out_shape=[jax.ShapeDtypeStruct((g, m, k), F32)] + [state_shape] * 4,
        compiler_params=_cp(("parallel",)),
        name="s5_intra",
    )(u_g, w_g)


S5_SCAN_LANES = 1024


def _s5_scan_kernel(sf_ref, sfs_ref, sb_ref, sbs_ref, a_ref, hf_ref, hb_ref, st_sc, *, tc):
    @pl.when(pl.program_id(1) == 0)
    def _():
        st_sc[...] = jnp.zeros_like(st_sc)

    for lc in range(sf_ref.shape[1] // S5_SCAN_LANES):
        ln = slice(lc * S5_SCAN_LANES, (lc + 1) * S5_SCAN_LANES)
        a1f, a2f, a1b, a2b = a_ref[0:1, ln], a_ref[1:2, ln], a_ref[2:3, ln], a_ref[3:4, ln]

        def body(c, carry):
            hf, hfs, hb, hbs = carry
            hf_ref[pl.ds(c, 1), ln] = hf
            nf = a1f * hf + a2f * hfs + sf_ref[pl.ds(c, 1), ln]
            nfs = a1f * hfs - a2f * hf + sfs_ref[pl.ds(c, 1), ln]
            cb = tc - 1 - c
            hb_ref[pl.ds(cb, 1), ln] = hb
            nb = a1b * hb + a2b * hbs + sb_ref[pl.ds(cb, 1), ln]
            nbs = a1b * hbs - a2b * hb + sbs_ref[pl.ds(cb, 1), ln]
            return nf, nfs, nb, nbs

        out = lax.fori_loop(0, tc, body, tuple(st_sc[n:n + 1, ln] for n in range(4)))
        for n in range(4):
            st_sc[n:n + 1, ln] = out[n]


def _s5_scan(sf, sfs, sb, sbs, a_rows, n_batch, n_lat_chunks):
    m, lanes = sf.shape
    nc = m // n_batch
    tc = 16
    nlt = n_lat_chunks // tc
    nt = nc // tc

    def tile_f(bi, i):
        return bi * nt + jnp.where(i < nt - nlt, nlt + i, i - (nt - nlt))

    def tile_b(bi, i):
        return bi * nt + jnp.where(i < nt - nlt, nt - 1 - i, nlt - 1 - (i - (nt - nlt)))

    fwd = pl.BlockSpec((tc, lanes), lambda bi, i: (tile_f(bi, i), 0))
    bwd = pl.BlockSpec((tc, lanes), lambda bi, i: (tile_b(bi, i), 0))
    return pl.pallas_call(
        functools.partial(_s5_scan_kernel, tc=tc),
        grid=(n_batch, nt),
        in_specs=[fwd, fwd, bwd, bwd, pl.BlockSpec((8, lanes), lambda bi, i: (0, 0))],
        out_specs=[fwd, bwd],
        out_shape=[jax.ShapeDtypeStruct((m, lanes), F32), jax.ShapeDtypeStruct((m, lanes), F32)],
        scratch_shapes=[pltpu.VMEM((8, lanes), F32)],
        compiler_params=_cp(("parallel", "arbitrary")),
        name="s5_state_scan",
    )(sf, sfs, sb, sbs, a_rows)


def _s5_out_kernel(yi_ref, hf_ref, hb_ref, qf_ref, qb_ref, o_ref):
    o_ref[0] = (yi_ref[0]
                + jnp.dot(hf_ref[...].astype(BF16), qf_ref[0], preferred_element_type=F32)
                + jnp.dot(hb_ref[...].astype(BF16), qb_ref[0], preferred_element_type=F32)).astype(o_ref.dtype)


def _s5_out(yi, hf2, hb2, qf, qb):
    g, m, k = yi.shape
    w = 2 * S5_STATE
    return pl.pallas_call(
        _s5_out_kernel,
        grid=(g,),
        in_specs=[pl.BlockSpec((1, m, k), lambda gi: (gi, 0, 0)),
                  pl.BlockSpec((m, w), lambda gi: (0, gi)),
                  pl.BlockSpec((m, w), lambda gi: (0, gi)),
                  pl.BlockSpec((1, w, k), lambda gi: (gi, 0, 0)),
                  pl.BlockSpec((1, w, k), lambda gi: (gi, 0, 0))],
        out_specs=pl.BlockSpec((1, m, k), lambda gi: (gi, 0, 0)),
        out_shape=jax.ShapeDtypeStruct((g, m, k), BF16),
        compiler_params=_cp(("parallel",)),
        name="s5_readout",
    )(yi, hf2, hb2, qf, qb)


def _gelu_tanh(x):
    return 0.5 * x * (1.0 + jnp.tanh(math.sqrt(2.0 / math.pi) * (x + 0.044715 * (x * x * x))))


def _s5_finish_kernel(y_ref, u_ref, d_ref, w_ref, o_ref):
    y = y_ref[0].astype(F32) + d_ref[...] * u_ref[0].astype(F32)
    g = _gelu_tanh(y)
    o_ref[0] = (g * _sigmoid(jnp.dot(g.astype(BF16), w_ref[...], preferred_element_type=F32))).astype(o_ref.dtype)


def _s5_finish(y, proj, d_vec, w_glu):
    b, lt, w = y.shape
    tm = _tile(lt, 640, 256)
    return pl.pallas_call(
        _s5_finish_kernel,
        grid=(b, lt // tm),
        in_specs=[pl.BlockSpec((1, tm, w), lambda bi, i: (bi, i, 0)),
                  pl.BlockSpec((1, tm, w), lambda bi, i: (bi, i, OFF_S5 // w)),
                  pl.BlockSpec((1, w), lambda bi, i: (0, 0)),
                  pl.BlockSpec((w, w), lambda bi, i: (0, 0))],
        out_specs=pl.BlockSpec((1, tm, w), lambda bi, i: (bi, i, 0)),
        out_shape=jax.ShapeDtypeStruct((b, lt, w), BF16),
        compiler_params=_cp(("parallel", "parallel")),
        name="s5_finish",
    )(y, proj, d_vec.reshape(1, -1), w_glu)


def _s5_operators(p):
    t = S5_CHUNK
    lam = lax.complex(p["s5_lam_re"].astype(F32), p["s5_lam_im"].astype(F32))
    step = jnp.exp(p["s5_log_step"].astype(F32))[..., None]
    lam_bar = jnp.exp(lam * step)
    b_bar = ((lam_bar - 1.0) / lam)[..., None] * lax.complex(p["s5_b_re"].astype(F32), p["s5_b_im"].astype(F32))
    c_mat = lax.complex(p["s5_c_re"].astype(F32), p["s5_c_im"].astype(F32))
    ks = jnp.arange(t + 1, dtype=F32)
    pw = jnp.exp((lam * step)[:, :, None, :] * ks[None, None, :, None])
    kern = jnp.real(jnp.einsum("dgcn,dgkn,dgni->dgkci", c_mat, pw[:, :, :t], b_bar))
    s_idx = jnp.arange(t)[:, None]
    t_idx = jnp.arange(t)[None, :]
    lag_f = jnp.clip(t_idx - s_idx, 0, t - 1)
    lag_b = jnp.clip(s_idx - t_idx, 0, t - 1)
    kf = jnp.where((t_idx >= s_idx)[None, :, :, None, None], kern[0][:, lag_f], 0.0)
    kb = jnp.where((s_idx >= t_idx)[None, :, :, None, None], kern[1][:, lag_b], 0.0)
    w_intra = jnp.transpose(kf + kb, (0, 1, 4, 2, 3)).reshape(S5_GROUPS, t * S5_GROUP, t * S5_GROUP)
    inj_f = pw[0][:, t - 1 - jnp.arange(t), :, None] * b_bar[0][:, None]
    inj_b = pw[1][:, jnp.arange(t), :, None] * b_bar[1][:, None]

    def inj_mat(z):
        z = jnp.transpose(z, (0, 1, 3, 2)).reshape(S5_GROUPS, t * S5_GROUP, S5_STATE)
        return jnp.concatenate([jnp.real(z), jnp.imag(z), jnp.imag(z), jnp.real(z)], axis=-1)

    w_in = jnp.concatenate([w_intra, inj_mat(inj_f), inj_mat(inj_b)], axis=-1).astype(BF16)
    m_f = c_mat[0][:, None, :, :] * pw[0][:, 1 + jnp.arange(t), None, :]
    m_b = c_mat[1][:, None, :, :] * pw[1][:, t - jnp.arange(t), None, :]

    def read_mat(z):
        z = jnp.transpose(z.reshape(S5_GROUPS, t * S5_GROUP, S5_STATE), (0, 2, 1))
        return jnp.concatenate([jnp.real(z), -jnp.imag(z)], axis=1).astype(BF16)

    def trans(z):
        re, im = jnp.real(z), jnp.imag(z)
        return [jnp.concatenate([re, re], -1).reshape(1, -1), jnp.concatenate([-im, im], -1).reshape(1, -1)]

    a_rows = jnp.concatenate(trans(pw[0][:, t]) + trans(pw[1][:, t])
                             + [jnp.zeros((4, 2 * S5_STATE * S5_GROUPS), F32)], axis=0)
    return dict(w_in=w_in, q_f=read_mat(m_f), q_b=read_mat(m_b), a_rows=a_rows)


def _s5_mixer(proj, ops, d_vec, w_glu, n_lat):
    b, lt, _ = proj.shape
    t = S5_CHUNK
    nc = lt // t
    u = proj[:, :, OFF_S5:OFF_S5 + S5_WIDTH]
    u_g = jnp.transpose(u.reshape(b, nc, t, S5_GROUPS, S5_GROUP), (3, 0, 1, 2, 4)).reshape(S5_GROUPS, b * nc, t * S5_GROUP)
    yi, sf, sfs, sb, sbs = _s5_in(u_g, ops["w_in"])
    hf, hb = _s5_scan(sf, sfs, sb, sbs, ops["a_rows"], b, n_lat // t)
    y_g = _s5_out(yi, hf, hb, ops["q_f"], ops["q_b"])
    y = jnp.transpose(y_g.reshape(S5_GROUPS, b, nc, t, S5_GROUP), (1, 2, 3, 0, 4)).reshape(b, lt, S5_WIDTH)
    return _s5_finish(y, proj, d_vec, w_glu)


def _merge_kernel(ya_ref, yb_ref, yc_ref, yd_ref, g0_ref, g1_ref, g2_ref, g3_ref,
                  wa_ref, wb_ref, wc_ref, wd_ref, o_ref):
    acc = None
    for y_ref, g_ref, w_ref in ((ya_ref, g0_ref, wa_ref), (yb_ref, g1_ref, wb_ref),
                                (yc_ref, g2_ref, wc_ref), (yd_ref, g3_ref, wd_ref)):
        term = _sigmoid(g_ref[0].astype(F32)) * jnp.dot(y_ref[0], w_ref[...], preferred_element_type=F32)
        acc = term if acc is None else acc + term
    o_ref[0] = acc.astype(o_ref.dtype)


def _merge(ys, proj, ws):
    b, lt, _ = proj.shape
    d = D_MODEL
    tm = _tile(lt, 640, 256)
    tn = 512
    nb = d // tn
    in_specs = [pl.BlockSpec((1, tm, y.shape[2]), lambda bi, i, j: (bi, i, 0)) for y in ys]
    in_specs += [pl.BlockSpec((1, tm, tn), functools.partial(lambda bi, i, j, br: (bi, i, br * nb + j), br=br))
                 for br in range(4)]
    in_specs += [pl.BlockSpec((w.shape[0], tn), lambda bi, i, j: (0, j)) for w in ws]
    return pl.pallas_call(
        _merge_kernel,
        grid=(b, lt // tm, nb),
        in_specs=in_specs,
        out_specs=pl.BlockSpec((1, tm, tn), lambda bi, i, j: (bi, i, j)),
        out_shape=jax.ShapeDtypeStruct((b, lt, d), BF16),
        compiler_params=_cp(("parallel", "parallel", "parallel")),
        name="branch_merge",
    )(*ys, proj, proj, proj, proj, *ws)


def _row_gate(ml, mc, row_tile, tm, n_lat, row):
    rows = row_tile * tm + lax.broadcasted_iota(jnp.int32, (tm, 1), 0)
    return jnp.where(rows >= n_lat, mc[row:row + 1], ml[row:row + 1])


def _outproj_kernel(m_ref, w_ref, x_ref, ml_ref, mc_ref, o_ref, *, tm, n_lat):
    gate = _row_gate(ml_ref[0], mc_ref[0], pl.program_id(1), tm, n_lat, 2)
    o_ref[0] = x_ref[0] + gate * jnp.dot(m_ref[0], w_ref[...], preferred_element_type=F32)


def _outproj(m, w_out, x, mods, n_lat):
    b, lt, d = x.shape
    tm = _tile(lt, 1280, 256)
    tn = 512
    return pl.pallas_call(
        functools.partial(_outproj_kernel, tm=tm, n_lat=n_lat),
        grid=(b, lt // tm, d // tn),
        in_specs=[pl.BlockSpec((1, tm, d), lambda bi, i, j: (bi, i, 0)),
                  pl.BlockSpec((d, tn), lambda bi, i, j: (0, j)),
                  pl.BlockSpec((1, tm, tn), lambda bi, i, j: (bi, i, j)),
                  pl.BlockSpec((1, 8, tn), lambda bi, i, j: (bi, 0, j)),
                  pl.BlockSpec((1, 8, tn), lambda bi, i, j: (b, 0, j))],
        out_specs=pl.BlockSpec((1, tm, tn), lambda bi, i, j: (bi, i, j)),
        out_shape=jax.ShapeDtypeStruct((b, lt, d), F32),
        compiler_params=_cp(("parallel", "parallel", "parallel")),
        name="out_proj",
    )(m, w_out, x, mods, mods)


def _prenorm_router_kernel(x_ref, g_ref, ml_ref, mc_ref, wr_ref, br_ref, h_ref, lg_ref, *, tm, n_lat):
    def chunk(r, carry):
        rs = pl.multiple_of(r * ROW_CHUNK, ROW_CHUNK)
        h = _norm_mod(x_ref[0, pl.ds(rs, ROW_CHUNK), :], g_ref[...], ml_ref[0], mc_ref[0],
                      pl.program_id(1) * tm + rs, n_lat, 3).astype(BF16)
        h_ref[0, pl.ds(rs, ROW_CHUNK), :] = h
        lg_ref[0, pl.ds(rs, ROW_CHUNK), :] = jnp.dot(h, wr_ref[...], preferred_element_type=F32) + br_ref[...]
        return carry

    lax.fori_loop(0, tm // ROW_CHUNK, chunk, 0)


def _prenorm_router(x, gain, mods, w_r, b_r, n_lat):
    b, lt, d = x.shape
    tm = _tile(lt, 640, 256)
    return pl.pallas_call(
        functools.partial(_prenorm_router_kernel, tm=tm, n_lat=n_lat),
        grid=(b, lt // tm),
        in_specs=[pl.BlockSpec((1, tm, d), lambda bi, i: (bi, i, 0)),
                  pl.BlockSpec((1, d), lambda bi, i: (0, 0)),
                  pl.BlockSpec((1, 8, d), lambda bi, i: (bi, 0, 0)),
                  pl.BlockSpec((1, 8, d), lambda bi, i: (b, 0, 0)),
                  pl.BlockSpec((d, LANE), lambda bi, i: (0, 0)),
                  pl.BlockSpec((1, LANE), lambda bi, i: (0, 0))],
        out_specs=[pl.BlockSpec((1, tm, d), lambda bi, i: (bi, i, 0)),
                   pl.BlockSpec((1, tm, LANE), lambda bi, i: (bi, i, 0))],
        out_shape=[jax.ShapeDtypeStruct((b, lt, d), BF16),
                   jax.ShapeDtypeStruct((b, lt, LANE), F32)],
        compiler_params=_cp(("parallel", "parallel")),
        name="moe_prenorm_router",
    )(x, gain.reshape(1, d), mods, mods, w_r, b_r)


def _moe_kernel(tile_ref, exp_ref, lo_ref, hi_ref, first_ref, x_ref, rw_ref, w1_ref, w3_ref, w2_ref, o_ref):
    k = pl.program_id(0)
    lo, hi = lo_ref[k], hi_ref[k]
    tile = x_ref.shape[0]

    @pl.when(hi > lo)
    def _():
        x = x_ref[...]
        hid = _silu(jnp.dot(x, w1_ref[0], preferred_element_type=F32)) * jnp.dot(x, w3_ref[0], preferred_element_type=F32)
        y = (jnp.dot(hid.astype(BF16), w2_ref[0], preferred_element_type=F32) * rw_ref[...]).astype(o_ref.dtype)
        rows = tile_ref[k] * tile + lax.broadcasted_iota(jnp.int32, (tile, 1), 0)
        mine = jnp.logical_and(rows >= lo, rows < hi)

        @pl.when(first_ref[k] == 1)
        def _():
            o_ref[...] = jnp.where(mine, y, jnp.zeros_like(y))

        @pl.when(first_ref[k] == 0)
        def _():
            o_ref[...] = jnp.where(mine, y, o_ref[...])


def _moe_ffn(xs, rw, items, w1, w3, w2):
    rows, d = xs.shape
    tile = MOE_TILE
    ff = w1.shape[2]
    item_tile, item_expert = items[0], items[1]
    grid_spec = pltpu.PrefetchScalarGridSpec(
        num_scalar_prefetch=5,
        grid=(item_tile.shape[0],),
        in_specs=[pl.BlockSpec((tile, d), lambda k, it, ie, lo, hi, fi: (it[k], 0)),
                  pl.BlockSpec((tile, 1), lambda k, it, ie, lo, hi, fi: (it[k], 0)),
                  pl.BlockSpec((1, d, ff), lambda k, it, ie, lo, hi, fi: (ie[k], 0, 0)),
                  pl.BlockSpec((1, d, ff), lambda k, it, ie, lo, hi, fi: (ie[k], 0, 0)),
                  pl.BlockSpec((1, ff, d), lambda k, it, ie, lo, hi, fi: (ie[k], 0, 0))],
        out_specs=pl.BlockSpec((tile, d), lambda k, it, ie, lo, hi, fi: (it[k], 0)),
    )
    return pl.pallas_call(
        _moe_kernel,
        grid_spec=grid_spec,
        out_shape=jax.ShapeDtypeStruct((rows, d), BF16),
        compiler_params=_cp(("arbitrary",)),
        name="moe_experts",
    )(*items, xs, rw, w1, w3, w2)


def _moe(h2, logits, w1, w3, w2):
    n, d = h2.shape
    tile = MOE_TILE
    g_prob = jax.nn.softmax(logits[:, :MOE_GROUPS], axis=-1)
    g_idx = jnp.argmax(g_prob, axis=-1)
    g_w = jnp.take_along_axis(g_prob, g_idx[:, None], axis=1)
    e_logits = logits[:, MOE_GROUPS:MOE_GROUPS + N_EXPERTS].reshape(n, MOE_GROUPS, MOE_PER_GROUP)
    e_logits = jnp.take_along_axis(e_logits, g_idx[:, None, None], axis=1)[:, 0]
    e_w, e_idx = lax.top_k(jax.nn.softmax(e_logits, axis=-1), 2)
    w = g_w * e_w / jnp.sum(e_w, axis=-1, keepdims=True)
    expert = (g_idx[:, None] * MOE_PER_GROUP + e_idx).astype(jnp.int32)

    n_rows = 2 * n
    assert n_rows % tile == 0
    n_tiles = n_rows // tile
    iota = jnp.arange(n_rows, dtype=jnp.int32)
    sorted_e, order, sorted_w = lax.sort((expert.reshape(-1), iota, w.reshape(-1)), num_keys=1, is_stable=True)
    _, sorted_pos = lax.sort((order, iota), num_keys=1)
    e_ids = jnp.arange(N_EXPERTS, dtype=jnp.int32)
    cnt_end = jnp.sum((sorted_e[None, :] <= e_ids[:, None]).astype(jnp.int32), axis=1)
    cnt_start = jnp.concatenate([jnp.zeros((1,), jnp.int32), cnt_end[:-1]])
    first_tile = cnt_start // tile
    last_tile = jnp.where(cnt_end > cnt_start, (cnt_end - 1) // tile, first_tile - 1)
    item_end = jnp.cumsum(last_tile - first_tile + 1)
    item_start = jnp.concatenate([jnp.zeros((1,), jnp.int32), item_end[:-1]])
    n_items = item_end[-1]
    k = jnp.arange(n_tiles + N_EXPERTS, dtype=jnp.int32)
    item_e = jnp.minimum(jnp.sum((item_end[None, :] <= k[:, None]).astype(jnp.int32), axis=1), N_EXPERTS - 1)
    item_t = first_tile[item_e] + k - item_start[item_e]
    valid = k < n_items
    item_e = jnp.where(valid, item_e, item_e[n_items - 1])
    item_t = jnp.where(valid, item_t, item_t[n_items - 1])
    item_lo = jnp.where(valid, jnp.maximum(cnt_start[item_e], item_t * tile), 0)
    item_hi = jnp.where(valid, jnp.minimum(cnt_end[item_e], (item_t + 1) * tile), 0)
    item_first = jnp.concatenate([jnp.ones((1,), jnp.int32), (item_t[1:] != item_t[:-1]).astype(jnp.int32)])
    items = tuple(a.astype(jnp.int32) for a in (item_t, item_e, item_lo, item_hi, item_first))

    xs = jnp.take(h2, order // 2, axis=0)
    ys = _moe_ffn(xs, sorted_w[:, None], items, w1, w3, w2)
    return jnp.sum(jnp.take(ys, sorted_pos, axis=0).reshape(n, 2, d).astype(F32), axis=1)


def _pack_w_in(w_in):
    depth, d, _ = w_in.shape
    cuts = {}
    off = 0
    for name, width in (("s5", S5_WIDTH), ("gq", GQA_OUT), ("gk", GQA_KV_HEADS * HEAD_DIM), ("gv", GQA_KV_HEADS * HEAD_DIM),
                        ("mq", MLA_Q_RANK), ("mkv", MLA_KV_RANK), ("mkr", MLA_ROPE), ("mz", MB_INNER), ("xbc", MB_XBC),
                        ("dt", 2 * MB_HEADS), ("gate", 4 * D_MODEL)):
        cuts[name] = w_in[:, :, off:off + width]
        off += width
    zeros = lambda n: jnp.zeros((depth, d, n), w_in.dtype)
    dt_blocks = []
    for blk in range(2 * MB_GROUPS):
        dt_blocks += [cuts["dt"][:, :, blk * MB_HPG:(blk + 1) * MB_HPG], zeros(LANE - MB_HPG)]
    packed = jnp.concatenate(
        [cuts["gate"], cuts["gq"], cuts["mz"], cuts["xbc"], cuts["mq"], cuts["s5"], cuts["gk"], cuts["gv"],
         cuts["mkv"], cuts["mkr"], zeros(LANE - MLA_ROPE)] + dt_blocks + [zeros(NW - OFF_DT - 4 * LANE)], axis=-1)
    return packed.astype(BF16)


def _rope_table(n_lat, n_ctx, dim, width):
    rows = n_lat // GRID_W
    row = jnp.repeat(jnp.arange(rows, dtype=F32), GRID_W)
    col = jnp.tile(jnp.arange(GRID_W, dtype=F32), rows)
    quarter = dim // 4
    inv_freq = ROPE_THETA ** (-jnp.arange(quarter, dtype=F32) / quarter)
    ang_r = row[:, None] * inv_freq
    ang_c = col[:, None] * inv_freq
    ang = jnp.concatenate([ang_r, ang_r, ang_c, ang_c], axis=-1)
    cos = jnp.concatenate([jnp.cos(ang), jnp.ones((n_lat, width - dim), F32)], axis=-1)
    sin = jnp.concatenate([jnp.sin(ang), jnp.zeros((n_lat, width - dim), F32)], axis=-1)
    cos = jnp.concatenate([cos, jnp.ones((n_ctx, width), F32)], axis=0)
    sin = jnp.concatenate([sin, jnp.zeros((n_ctx, width), F32)], axis=0)
    return cos, sin


def _rot_matrix(dim, width):
    quarter = dim // 4
    idx = jnp.arange(width)
    in_first = (idx % (2 * quarter)) < quarter
    valid = idx < dim
    src_for_first = idx + quarter
    src_for_second = idx - quarter
    rows = jnp.arange(width)[:, None]
    r = jnp.where(in_first[None, :] & (rows == src_for_first[None, :]), -1.0, 0.0)
    r = r + jnp.where((~in_first)[None, :] & (rows == src_for_second[None, :]), 1.0, 0.0)
    return (r * valid[None, :]).astype(BF16)


def _pad_lanes(v, width):
    return jnp.concatenate([v.astype(F32), jnp.zeros((width - v.shape[0],), F32)]).reshape(1, width)


def _lane_row(lane, value):
    return jnp.zeros((1, LANE), F32).at[0, lane].set(value)


def _score_bound(q_sq_norm, k_sq_norm, scale):
    return (jnp.sqrt(q_sq_norm * k_sq_norm) * (scale * LOG2E * 1.02) + 0.5).astype(F32)


def kernel(x, c, ctx, c_ctx, norm1, norm2, w_ada, b_ada, w_in, s5_lam_re, s5_lam_im, s5_log_step, s5_b_re, s5_b_im, s5_c_re, s5_c_im, s5_d, s5_w_glu, gqa_q_norm, gqa_k_norm, mla_q_norm, mla_kv_norm, mla_w_uq, mla_w_uk, mla_w_uv, mla_qn_norm, mla_kn_norm, mla_qr_norm, mla_kr_norm, mb_conv_w, mb_conv_b, mb_dt_bias, mb_a_log, mb_d, mb_norm, w_br_s5, w_br_gqa, w_br_mla, w_br_mb, w_out, moe_w_group, moe_b_group, moe_w_expert, moe_b_expert, moe_w1, moe_w3, moe_w2):
    b, n_lat, d = x.shape
    n_ctx = ctx.shape[1]
    lt = n_lat + n_ctx
    depth = w_in.shape[0]

    cvec = jnp.zeros((8, d), F32).at[:b].set(c.astype(F32)).at[b].set(c_ctx.astype(F32))
    mod_all = _ada(cvec, w_ada, b_ada).reshape(depth, 8, N_MOD, d)[:, :b + 1]
    mod_all = jnp.concatenate([mod_all, jnp.zeros((depth, b + 1, 8 - N_MOD, d), F32)], axis=2)

    w_in_p = _pack_w_in(w_in)
    cos_g, sin_g = _rope_table(n_lat, n_ctx, HEAD_DIM, HEAD_DIM)
    cos_m, sin_m = _rope_table(n_lat, n_ctx, MLA_ROPE, LANE)
    rot_g = _rot_matrix(HEAD_DIM, HEAD_DIM)
    rot_m = _rot_matrix(MLA_ROPE, LANE)
    ka_g = jnp.broadcast_to(_lane_row(0, 1.0).astype(BF16), (b, lt, LANE))

    uq = mla_w_uq.reshape(depth, MLA_Q_RANK, MLA_HEADS, MLA_NOPE + MLA_ROPE)
    wq_nope = uq[..., :MLA_NOPE].reshape(depth, MLA_Q_RANK, MLA_HEADS * MLA_NOPE).astype(BF16)
    wq_rope = jnp.concatenate([uq[..., MLA_NOPE:], jnp.zeros(uq.shape[:3] + (LANE - MLA_ROPE,), uq.dtype)],
                              axis=-1).reshape(depth, MLA_Q_RANK, MLA_HEADS * LANE).astype(BF16)
    a_dec = -jnp.exp(mb_a_log.astype(F32))

    xs = jnp.concatenate([x.astype(F32), ctx.astype(F32)], axis=1)

    for i in range(depth):
        mods = mod_all[i]
        proj = _inproj(xs, norm1[i], mods, w_in_p[i], n_lat)

        s5p = dict(s5_lam_re=s5_lam_re[i], s5_lam_im=s5_lam_im[i], s5_log_step=s5_log_step[i], s5_b_re=s5_b_re[i],
                   s5_b_im=s5_b_im[i], s5_c_re=s5_c_re[i], s5_c_im=s5_c_im[i])
        ya = _s5_mixer(proj, _s5_operators(s5p), s5_d[i].astype(F32), s5_w_glu[i].astype(BF16), n_lat)

        shift_g = _score_bound(HEAD_DIM * jnp.max(jnp.abs(gqa_q_norm[i])) ** 2,
                               HEAD_DIM * jnp.max(jnp.abs(gqa_k_norm[i])) ** 2, HEAD_DIM ** -0.5)
        qg, kg = _gqa_prep(proj, cos_g, sin_g, rot_g, gqa_q_norm[i].astype(F32), gqa_k_norm[i].astype(F32),
                           _lane_row(0, -shift_g))
        yb = _attention(qg, kg, ka_g, proj, shift_g, n_kv_heads=GQA_KV_HEADS, grp=GQA_HEADS // GQA_KV_HEADS,
                        n_lat=n_lat, v_col0=OFF_GV // HEAD_DIM, tq_lat=1024, tk_lat=3328)

        mp = dict(mla_q_norm=mla_q_norm[i].astype(F32), mla_kv_norm=mla_kv_norm[i].astype(F32),
                  wq_nope=wq_nope[i], wq_rope=wq_rope[i], w_uk=mla_w_uk[i].astype(BF16), w_uv=mla_w_uv[i].astype(BF16),
                  mla_qn_norm=mla_qn_norm[i].astype(F32), mla_kn_norm=mla_kn_norm[i].astype(F32),
                  qr_gain=_pad_lanes(mla_qr_norm[i], LANE), kr_gain=_pad_lanes(mla_kr_norm[i], LANE))
        shift_m = _score_bound(
            MLA_NOPE * jnp.max(jnp.abs(mla_qn_norm[i])) ** 2 + MLA_ROPE * jnp.max(jnp.abs(mla_qr_norm[i])) ** 2,
            MLA_NOPE * jnp.max(jnp.abs(mla_kn_norm[i])) ** 2 + MLA_ROPE * jnp.max(jnp.abs(mla_kr_norm[i])) ** 2,
            (MLA_NOPE + MLA_ROPE) ** -0.5)
        mp["q_shift_row"] = _lane_row(MLA_ROPE, -shift_m)
        mp["k_one_row"] = _lane_row(MLA_ROPE, 1.0)
        qm, kn, kr, vm = _mla_prep(proj, cos_m, sin_m, rot_m, mp)
        yc = _attention(qm, kn, kr, vm, shift_m, n_kv_heads=MLA_HEADS, grp=1,
                        n_lat=n_lat, v_col0=0, tq_lat=4096, tk_lat=3328)

        xbc = _mamba_conv(proj, mb_conv_w[i], mb_conv_b[i], n_lat)
        dt_bias4 = jnp.concatenate([mb_dt_bias[i].astype(F32).reshape(2 * MB_GROUPS, MB_HPG),
                                    jnp.zeros((2 * MB_GROUPS, LANE - MB_HPG), F32)], axis=-1).reshape(2 * MB_GROUPS, 1, LANE)
        a4 = jnp.concatenate([a_dec[i].reshape(2 * MB_GROUPS, MB_HPG),
                              jnp.zeros((2 * MB_GROUPS, LANE - MB_HPG), F32)], axis=-1).reshape(2 * MB_GROUPS, 1, LANE)
        y2 = _ssd(xbc, proj, dt_bias4, a4, n_lat)
        d_vec = jnp.repeat(mb_d[i].astype(F32), MB_HEAD_DIM).reshape(1, MB_INNER)
        yd = _mamba_finish(y2, xbc, proj, d_vec, mb_norm[i].astype(F32))

        merged = _merge((ya, yb, yc, yd), proj,
                        (w_br_s5[i].astype(BF16), w_br_gqa[i].astype(BF16), w_br_mla[i].astype(BF16), w_br_mb[i].astype(BF16)))
        xs = _outproj(merged, w_out[i].astype(BF16), xs, mods, n_lat)

        w_r = jnp.concatenate([moe_w_group[i], moe_w_expert[i],
                               jnp.zeros((d, LANE - MOE_GROUPS - N_EXPERTS), F32)], axis=-1).astype(BF16)
        b_r = _pad_lanes(jnp.concatenate([moe_b_group[i], moe_b_expert[i]]), LANE)
        h2, logits = _prenorm_router(xs, norm2[i], mods, w_r, b_r, n_lat)
        ff = _moe(h2.reshape(b * lt, d), logits.reshape(b * lt, LANE),
                  moe_w1[i].astype(BF16), moe_w3[i].astype(BF16), moe_w2[i].astype(BF16)).reshape(b, lt, d)
        gate2 = jnp.concatenate([jnp.broadcast_to(mods[:b, 5][:, None, :], (b, n_lat, d)),
                                 jnp.broadcast_to(mods[b, 5][None, None, :], (b, n_ctx, d))], axis=1)
        xs = xs + gate2 * ff

    return xs[:, :n_lat].astype(x.dtype)
```

```python
import functools
import math

import jax
import jax.numpy as jnp
from jax import lax
from jax.experimental import pallas as pl
from jax.experimental.pallas import tpu as pltpu

F32 = jnp.float32
BF16 = jnp.bfloat16
HIGHEST = lax.Precision.HIGHEST

EPS = 1e-6
ROPE_THETA = 10000.0
GRID_W = 64
D_MODEL = 2048
N_MOD = 6

S5_GROUP = 16
S5_WIDTH = 768
S5_GROUPS = S5_WIDTH // S5_GROUP
S5_STATE = 64
S5_CHUNK = 16

GQA_HEADS = 8
GQA_KV_HEADS = 2
HEAD_DIM = 128
GQA_OUT = GQA_HEADS * HEAD_DIM

MLA_HEADS = 8
MLA_Q_RANK = 512
MLA_KV_RANK = 256
MLA_NOPE = 128
MLA_ROPE = 64
MLA_OUT = MLA_HEADS * HEAD_DIM

MB_HEADS = 16
MB_HEAD_DIM = 64
MB_INNER = MB_HEADS * MB_HEAD_DIM
MB_GROUPS = 2
MB_HPG = MB_HEADS // MB_GROUPS
MB_STATE = 128
MB_CONV = 5
MB_XBC = MB_INNER + 2 * MB_GROUPS * MB_STATE
SSD_CHUNK = 128

MOE_GROUPS = 4
MOE_PER_GROUP = 8
N_EXPERTS = MOE_GROUPS * MOE_PER_GROUP
MOE_FF = 512
MOE_TILE = 512

LANE = 128
VMEM_LIMIT = 56 * 1024 * 1024
LOG2E = math.log2(math.e)
FLASH_ROW_BLOCK = 256
FLASH_DQ = 256
FLASH_MAX_SHIFT = 60.0

OFF_GATE = 0
OFF_GQ = 8192
OFF_MZ = 9216
OFF_XBC = 10240
OFF_MQ = 11776
OFF_S5 = 12288
OFF_GK = 13056
OFF_GV = 13312
OFF_MKV = 13568
OFF_MKR = 13824
OFF_DT = 13952
NW = 14592


def _tile(n, *cands):
    for c in cands:
        if n % c == 0:
            return c
    return n


def _cp(sem, vmem=VMEM_LIMIT):
    return pltpu.CompilerParams(dimension_semantics=sem, vmem_limit_bytes=vmem)


def _sigmoid(x):
    return 1.0 / (1.0 + jnp.exp(-x))


def _silu(x):
    return x * _sigmoid(x)


def _ada_kernel(c_ref, w_ref, b_ref, o_ref):
    s = _silu(c_ref[...])
    o_ref[0] = jnp.dot(s.astype(BF16), w_ref[0].astype(BF16), preferred_element_type=F32) + b_ref[0]


def _ada(cvec, w_ada, b_ada):
    depth, d, n = w_ada.shape
    tn = _tile(n, 1024, 512)
    return pl.pallas_call(
        _ada_kernel,
        grid=(depth, n // tn),
        in_specs=[pl.BlockSpec((8, d), lambda l, j: (0, 0)),
                  pl.BlockSpec((1, d, tn), lambda l, j: (l, 0, j)),
                  pl.BlockSpec((1, 1, tn), lambda l, j: (l, 0, j))],
        out_specs=pl.BlockSpec((1, 8, tn), lambda l, j: (l, 0, j)),
        out_shape=jax.ShapeDtypeStruct((depth, 8, n), F32),
        compiler_params=_cp(("parallel", "parallel")),
        name="ada_mod",
    )(cvec, w_ada, b_ada.reshape(depth, 1, n))


ROW_CHUNK = 128


def _norm_mod(x, g, ml, mc, row_start, n_lat, row0):
    r = lax.rsqrt(jnp.mean(x * x, axis=-1, keepdims=True) + EPS)
    rows = row_start + lax.broadcasted_iota(jnp.int32, (x.shape[0], 1), 0)
    is_ctx = rows >= n_lat
    shift = jnp.where(is_ctx, mc[row0:row0 + 1], ml[row0:row0 + 1])
    scale = jnp.where(is_ctx, mc[row0 + 1:row0 + 2], ml[row0 + 1:row0 + 2])
    return x * r * g * (1.0 + scale) + shift


def _inproj_kernel(x_ref, g_ref, ml_ref, mc_ref, w_ref, o_ref, u_ref, h_sc, *, n_lat, tm, s5_tile):
    @pl.when(pl.program_id(2) == 0)
    def _():
        def chunk(r, carry):
            rs = pl.multiple_of(r * ROW_CHUNK, ROW_CHUNK)
            h = _norm_mod(x_ref[0, pl.ds(rs, ROW_CHUNK), :], g_ref[...], ml_ref[0], mc_ref[0],
                          pl.program_id(1) * tm + rs, n_lat, 0)
            h_sc[pl.ds(rs, ROW_CHUNK), :] = h.astype(BF16)
            return carry

        lax.fori_loop(0, tm // ROW_CHUNK, chunk, 0)

    acc = jnp.dot(h_sc[...], w_ref[...], preferred_element_type=F32)
    o_ref[0] = acc.astype(o_ref.dtype)

    @pl.when(pl.program_id(2) == s5_tile)
    def _():
        u_ref[0] = acc


def _inproj(x, gain, mods, w, n_lat):
    b, lt, d = x.shape
    n = w.shape[1]
    tm = _tile(lt, 1280, 256)
    tn = S5_WIDTH
    assert n % tn == 0 and OFF_S5 % tn == 0
    return pl.pallas_call(
        functools.partial(_inproj_kernel, n_lat=n_lat, tm=tm, s5_tile=OFF_S5 // tn),
        grid=(b, lt // tm, n // tn),
        in_specs=[pl.BlockSpec((1, tm, d), lambda bi, i, j: (bi, i, 0)),
                  pl.BlockSpec((1, d), lambda bi, i, j: (0, 0)),
                  pl.BlockSpec((1, 8, d), lambda bi, i, j: (bi, 0, 0)),
                  pl.BlockSpec((1, 8, d), lambda bi, i, j: (b, 0, 0)),
                  pl.BlockSpec((d, tn), lambda bi, i, j: (0, j))],
        out_specs=[pl.BlockSpec((1, tm, tn), lambda bi, i, j: (bi, i, j)),
                   pl.BlockSpec((1, tm, tn), lambda bi, i, j: (bi, i, 0))],
        out_shape=[jax.ShapeDtypeStruct((b, lt, n), BF16),
                   jax.ShapeDtypeStruct((b, lt, tn), F32)],
        scratch_shapes=[pltpu.VMEM((tm, d), BF16)],
        compiler_params=_cp(("parallel", "parallel", "arbitrary")),
        name="in_proj",
    )(x, gain.reshape(1, d), mods, mods, w)


def _head_norm_rope(x, gain, cos, sin, rmat, n_valid, scale):
    ms = jnp.sum(x * x, axis=-1, keepdims=True) * (1.0 / n_valid)
    y = x * lax.rsqrt(ms + EPS) * gain
    if rmat is not None:
        rot = jnp.dot(y.astype(BF16), rmat, preferred_element_type=F32)
        y = y * cos + rot * sin
    return y * scale


def _gqa_prep_kernel(q_ref, k_ref, cos_ref, sin_ref, r_ref, qg_ref, kg_ref, sh_ref, qo_ref, ko_ref):
    cos, sin, rmat = cos_ref[...], sin_ref[...], r_ref[...]
    scale = HEAD_DIM ** -0.5 * LOG2E
    aux = jnp.broadcast_to(sh_ref[...], (q_ref.shape[1], LANE)).astype(BF16)
    for h in range(GQA_HEADS):
        x = q_ref[0, :, h * HEAD_DIM:(h + 1) * HEAD_DIM].astype(F32)
        qo_ref[0, :, h * FLASH_DQ:h * FLASH_DQ + HEAD_DIM] = _head_norm_rope(
            x, qg_ref[...], cos, sin, rmat, HEAD_DIM, scale).astype(BF16)
        qo_ref[0, :, h * FLASH_DQ + HEAD_DIM:(h + 1) * FLASH_DQ] = aux
    for h in range(GQA_KV_HEADS):
        x = k_ref[0, :, h * HEAD_DIM:(h + 1) * HEAD_DIM].astype(F32)
        ko_ref[0, :, h * HEAD_DIM:(h + 1) * HEAD_DIM] = _head_norm_rope(
            x, kg_ref[...], cos, sin, rmat, HEAD_DIM, 1.0).astype(BF16)


def _gqa_prep(proj, cos, sin, rmat, q_gain, k_gain, shift_row):
    b, lt, _ = proj.shape
    tm = _tile(lt, 640, 256)
    kw = GQA_KV_HEADS * HEAD_DIM
    qw = GQA_HEADS * FLASH_DQ
    return pl.pallas_call(
        _gqa_prep_kernel,
        grid=(b, lt // tm),
        in_specs=[pl.BlockSpec((1, tm, GQA_OUT), lambda bi, i: (bi, i, OFF_GQ // GQA_OUT)),
                  pl.BlockSpec((1, tm, kw), lambda bi, i: (bi, i, OFF_GK // kw)),
                  pl.BlockSpec((tm, HEAD_DIM), lambda bi, i: (i, 0)),
                  pl.BlockSpec((tm, HEAD_DIM), lambda bi, i: (i, 0)),
                  pl.BlockSpec((HEAD_DIM, HEAD_DIM), lambda bi, i: (0, 0)),
                  pl.BlockSpec((1, HEAD_DIM), lambda bi, i: (0, 0)),
                  pl.BlockSpec((1, HEAD_DIM), lambda bi, i: (0, 0)),
                  pl.BlockSpec((1, LANE), lambda bi, i: (0, 0))],
        out_specs=[pl.BlockSpec((1, tm, qw), lambda bi, i: (bi, i, 0)),
                   pl.BlockSpec((1, tm, kw), lambda bi, i: (bi, i, 0))],
        out_shape=[jax.ShapeDtypeStruct((b, lt, qw), BF16),
                   jax.ShapeDtypeStruct((b, lt, kw), BF16)],
        compiler_params=_cp(("parallel", "parallel")),
        name="gqa_prep",
    )(proj, proj, cos, sin, rmat, q_gain.reshape(1, -1), k_gain.reshape(1, -1), shift_row)


def _mla_prep_kernel(cq_ref, ckv_ref, kr_ref, cos_ref, sin_ref, r_ref,
                     qn_g_ref, kvn_g_ref, wqn_ref, wqr_ref, wuk_ref, wuv_ref,
                     qnn_g_ref, qrn_g_ref, knn_g_ref, krn_g_ref, qsh_ref, kone_ref,
                     q_ref, kn_ref, kr_out_ref, v_ref):
    cos, sin, rmat = cos_ref[...], sin_ref[...], r_ref[...]
    scale = (MLA_NOPE + MLA_ROPE) ** -0.5 * LOG2E
    cq = cq_ref[0].astype(F32)
    cqn = (cq * lax.rsqrt(jnp.mean(cq * cq, axis=-1, keepdims=True) + EPS) * qn_g_ref[...]).astype(BF16)
    q_nope = jnp.dot(cqn, wqn_ref[...], preferred_element_type=F32)
    q_rope = jnp.dot(cqn, wqr_ref[...], preferred_element_type=F32)
    for h in range(MLA_HEADS):
        sl = slice(h * HEAD_DIM, (h + 1) * HEAD_DIM)
        qn = _head_norm_rope(q_nope[:, sl], qnn_g_ref[...], None, None, None, MLA_NOPE, scale)
        qr = _head_norm_rope(q_rope[:, sl], qrn_g_ref[...], cos, sin, rmat, MLA_ROPE, scale)
        q_ref[0, :, 2 * h * HEAD_DIM:(2 * h + 1) * HEAD_DIM] = qn.astype(BF16)
        q_ref[0, :, (2 * h + 1) * HEAD_DIM:(2 * h + 2) * HEAD_DIM] = (qr + qsh_ref[...]).astype(BF16)
    ckv = ckv_ref[0].astype(F32)
    lat = (ckv * lax.rsqrt(jnp.mean(ckv * ckv, axis=-1, keepdims=True) + EPS) * kvn_g_ref[...]).astype(BF16)
    k_nope = jnp.dot(lat, wuk_ref[...], preferred_element_type=F32)
    v_ref[0] = jnp.dot(lat, wuv_ref[...], preferred_element_type=F32).astype(BF16)
    for h in range(MLA_HEADS):
        sl = slice(h * HEAD_DIM, (h + 1) * HEAD_DIM)
        kn_ref[0, :, sl] = _head_norm_rope(k_nope[:, sl], knn_g_ref[...], None, None, None, MLA_NOPE, 1.0).astype(BF16)
    kr = kr_ref[0].astype(F32)
    kr_out_ref[0] = (_head_norm_rope(kr, krn_g_ref[...], cos, sin, rmat, MLA_ROPE, 1.0) + kone_ref[...]).astype(BF16)


def _mla_prep(proj, cos, sin, rmat, p):
    b, lt, _ = proj.shape
    tm = _tile(lt, 640, 256)
    hd = MLA_HEADS * HEAD_DIM
    full = lambda shape: pl.BlockSpec(shape, lambda bi, i: tuple(0 for _ in shape))
    return pl.pallas_call(
        _mla_prep_kernel,
        grid=(b, lt // tm),
        in_specs=[pl.BlockSpec((1, tm, MLA_Q_RANK), lambda bi, i: (bi, i, OFF_MQ // MLA_Q_RANK)),
                  pl.BlockSpec((1, tm, MLA_KV_RANK), lambda bi, i: (bi, i, OFF_MKV // MLA_KV_RANK)),
                  pl.BlockSpec((1, tm, LANE), lambda bi, i: (bi, i, OFF_MKR // LANE)),
                  pl.BlockSpec((tm, LANE), lambda bi, i: (i, 0)),
                  pl.BlockSpec((tm, LANE), lambda bi, i: (i, 0)),
                  full((LANE, LANE)),
                  full((1, MLA_Q_RANK)), full((1, MLA_KV_RANK)),
                  full((MLA_Q_RANK, hd)), full((MLA_Q_RANK, hd)),
                  full((MLA_KV_RANK, hd)), full((MLA_KV_RANK, hd)),
                  full((1, LANE)), full((1, LANE)), full((1, LANE)), full((1, LANE)),
                  full((1, LANE)), full((1, LANE))],
        out_specs=[pl.BlockSpec((1, tm, 2 * hd), lambda bi, i: (bi, i, 0)),
                   pl.BlockSpec((1, tm, hd), lambda bi, i: (bi, i, 0)),
                   pl.BlockSpec((1, tm, LANE), lambda bi, i: (bi, i, 0)),
                   pl.BlockSpec((1, tm, hd), lambda bi, i: (bi, i, 0))],
        out_shape=[jax.ShapeDtypeStruct((b, lt, 2 * hd), BF16),
                   jax.ShapeDtypeStruct((b, lt, hd), BF16),
                   jax.ShapeDtypeStruct((b, lt, LANE), BF16),
                   jax.ShapeDtypeStruct((b, lt, hd), BF16)],
        compiler_params=_cp(("parallel", "parallel")),
        name="mla_prep",
    )(proj, proj, proj, cos, sin, rmat,
      p["mla_q_norm"].reshape(1, -1), p["mla_kv_norm"].reshape(1, -1),
      p["wq_nope"], p["wq_rope"], p["w_uk"], p["w_uv"],
      p["mla_qn_norm"].reshape(1, -1), p["qr_gain"], p["mla_kn_norm"].reshape(1, -1), p["kr_gain"],
      p["q_shift_row"], p["k_one_row"])


def _stack_queries(q_ref, q_sc, grp, tq):
    for g in range(grp):
        q_sc[g * tq:(g + 1) * tq, :] = q_ref[0, :, g * FLASH_DQ:(g + 1) * FLASH_DQ]


def _write_heads(o_ref, o, grp, tq):
    for g in range(grp):
        o_ref[0, :, g * HEAD_DIM:(g + 1) * HEAD_DIM] = o[g * tq:(g + 1) * tq].astype(o_ref.dtype)


def _flash_online_kernel(q_ref, k_ref, ka_ref, v_ref, o_ref, q_sc, m_sc, l_sc, acc_sc, *, grp, tq):
    kv = pl.program_id(3)

    @pl.when(kv == 0)
    def _():
        _stack_queries(q_ref, q_sc, grp, tq)
        m_sc[...] = jnp.full_like(m_sc, -jnp.inf)
        l_sc[...] = jnp.zeros_like(l_sc)
        acc_sc[...] = jnp.zeros_like(acc_sc)

    k = jnp.concatenate([k_ref[0], ka_ref[0]], axis=-1)
    v = v_ref[0]
    n_chunks = k.shape[0] // LANE
    rb = min(FLASH_ROW_BLOCK, grp * tq)
    for r in range(grp * tq // rb):
        rows = slice(r * rb, (r + 1) * rb)
        s = lax.dot_general(q_sc[rows, :], k, (((1,), (1,)), ((), ())), preferred_element_type=F32)
        m_prev = m_sc[rows, :]
        m_new = jnp.maximum(m_prev, jnp.max(s, axis=-1, keepdims=True))
        alpha = jnp.exp2(m_prev - m_new)
        p = jnp.exp2(s - jnp.concatenate([m_new] * n_chunks, axis=-1))
        p_sum = p[:, :LANE]
        for c in range(1, n_chunks):
            p_sum = p_sum + p[:, c * LANE:(c + 1) * LANE]
        l_sc[rows, :] = alpha * l_sc[rows, :] + p_sum
        acc_sc[rows, :] = alpha * acc_sc[rows, :] + jnp.dot(p.astype(BF16), v, preferred_element_type=F32)
        m_sc[rows, :] = m_new

    @pl.when(kv == pl.num_programs(3) - 1)
    def _():
        _write_heads(o_ref, acc_sc[...] / jnp.sum(l_sc[...], axis=-1, keepdims=True), grp, tq)


def _flash_bounded_kernel(q_ref, k_ref, ka_ref, v_ref, o_ref, q_sc, acc_sc, *, grp, tq):
    kv = pl.program_id(3)

    @pl.when(kv == 0)
    def _():
        _stack_queries(q_ref, q_sc, grp, tq)
        acc_sc[...] = jnp.zeros_like(acc_sc)

    k = jnp.concatenate([k_ref[0], ka_ref[0]], axis=-1)
    v = v_ref[0]
    ones_col = (lax.broadcasted_iota(jnp.int32, v.shape, 1) == 0).astype(BF16)
    v1 = jnp.concatenate([v, ones_col], axis=-1)
    rb = min(FLASH_ROW_BLOCK, grp * tq)
    for r in range(grp * tq // rb):
        rows = slice(r * rb, (r + 1) * rb)
        s = lax.dot_general(q_sc[rows, :], k, (((1,), (1,)), ((), ())), preferred_element_type=F32)
        acc_sc[rows, :] += jnp.dot(jnp.exp2(s).astype(BF16), v1, preferred_element_type=F32)

    @pl.when(kv == pl.num_programs(3) - 1)
    def _():
        acc = acc_sc[...]
        _write_heads(o_ref, acc[:, :HEAD_DIM] / acc[:, HEAD_DIM:HEAD_DIM + 1], grp, tq)


def _flash(q, k, ka, v, *, bounded, n_kv_heads, grp, tq, tk, q_rows, q_off, kv_rows, kv_off, v_col0):
    b = q.shape[0]
    qo, ko = q_off // tq, kv_off // tk
    rows = grp * tq
    if bounded:
        body = _flash_bounded_kernel
        scratch = [pltpu.VMEM((rows, FLASH_DQ), BF16), pltpu.VMEM((rows, 2 * HEAD_DIM), F32)]
    else:
        body = _flash_online_kernel
        scratch = [pltpu.VMEM((rows, FLASH_DQ), BF16), pltpu.VMEM((rows, LANE), F32),
                   pltpu.VMEM((rows, LANE), F32), pltpu.VMEM((rows, HEAD_DIM), F32)]
    return pl.pallas_call(
        functools.partial(body, grp=grp, tq=tq),
        grid=(b, n_kv_heads, q_rows // tq, kv_rows // tk),
        in_specs=[pl.BlockSpec((1, tq, grp * FLASH_DQ), lambda bi, h, i, j: (bi, i + qo, h)),
                  pl.BlockSpec((1, tk, HEAD_DIM), lambda bi, h, i, j: (bi, j + ko, h)),
                  pl.BlockSpec((1, tk, LANE), lambda bi, h, i, j: (bi, j + ko, 0)),
                  pl.BlockSpec((1, tk, HEAD_DIM), lambda bi, h, i, j: (bi, j + ko, v_col0 + h))],
        out_specs=pl.BlockSpec((1, tq, grp * HEAD_DIM), lambda bi, h, i, j: (bi, i, h)),
        out_shape=jax.ShapeDtypeStruct((b, q_rows, n_kv_heads * grp * HEAD_DIM), BF16),
        scratch_shapes=scratch,
        compiler_params=_cp(("parallel", "parallel", "parallel", "arbitrary")),
        name="flash_bounded" if bounded else "flash_online",
    )(q, k, ka, v)


def _attention(q, k, ka, v, shift, *, n_kv_heads, grp, n_lat, v_col0, tq_lat, tk_lat):
    lt = q.shape[1]
    n_ctx = lt - n_lat
    tk = _tile(lt, tk_lat, 256)
    tq = _tile(n_lat, tq_lat, 256)

    def run(bounded):
        def go(q, k, ka, v):
            y_lat = _flash(q, k, ka, v, bounded=bounded, n_kv_heads=n_kv_heads, grp=grp, tq=tq, tk=tk,
                           q_rows=n_lat, q_off=0, kv_rows=lt, kv_off=0, v_col0=v_col0)
            y_ctx = _flash(q, k, ka, v, bounded=bounded, n_kv_heads=n_kv_heads, grp=grp, tq=n_ctx, tk=n_ctx,
                           q_rows=n_ctx, q_off=n_lat, kv_rows=n_ctx, kv_off=n_lat, v_col0=v_col0)
            return jnp.concatenate([y_lat, y_ctx], axis=1)
        return go

    return lax.cond(shift <= FLASH_MAX_SHIFT, run(True), run(False), q, k, ka, v)


def _conv_kernel(prev_ref, cur_ref, next_ref, w_ref, b_ref, o_ref, *, tr, n_lat, lt):
    rows = pl.program_id(1) * tr + lax.broadcasted_iota(jnp.int32, (tr, 1), 0)
    ext = jnp.concatenate([prev_ref[0].astype(F32), cur_ref[0].astype(F32), next_ref[0].astype(F32)], axis=0)
    w = w_ref[...]
    acc = jnp.zeros((tr, cur_ref.shape[2]), F32) + b_ref[...]
    half = MB_CONV // 2
    for kk in range(MB_CONV):
        src = rows + (kk - half)
        ok = jnp.logical_and(jnp.logical_and(src >= 0, src < lt), (src >= n_lat) == (rows >= n_lat))
        acc = acc + jnp.where(ok, ext[8 + kk - half:8 + kk - half + tr, :], 0.0) * w[kk:kk + 1, :]
    o_ref[0] = _silu(acc).astype(o_ref.dtype)


def _mamba_conv(proj, conv_w, conv_b, n_lat):
    b, lt, _ = proj.shape
    tr = _tile(lt, 1280, 256)
    cw = 512
    cb0 = OFF_XBC // cw
    r8 = tr // 8
    nblk8 = lt // 8
    w8 = jnp.zeros((8, MB_XBC), F32).at[:MB_CONV].set(conv_w.T.astype(F32))
    return pl.pallas_call(
        functools.partial(_conv_kernel, tr=tr, n_lat=n_lat, lt=lt),
        grid=(b, lt // tr, MB_XBC // cw),
        in_specs=[pl.BlockSpec((1, 8, cw), lambda bi, i, c: (bi, jnp.maximum(i * r8 - 1, 0), cb0 + c)),
                  pl.BlockSpec((1, tr, cw), lambda bi, i, c: (bi, i, cb0 + c)),
                  pl.BlockSpec((1, 8, cw), lambda bi, i, c: (bi, jnp.minimum((i + 1) * r8, nblk8 - 1), cb0 + c)),
                  pl.BlockSpec((8, cw), lambda bi, i, c: (0, c)),
                  pl.BlockSpec((1, cw), lambda bi, i, c: (0, c))],
        out_specs=pl.BlockSpec((1, tr, cw), lambda bi, i, c: (bi, i, c)),
        out_shape=jax.ShapeDtypeStruct((b, lt, MB_XBC), BF16),
        compiler_params=_cp(("parallel", "parallel", "parallel")),
        name="mamba_conv",
    )(proj, proj, proj, w8, conv_b.reshape(1, -1).astype(F32))


def _softplus(x):
    return jnp.maximum(x, 0.0) + jnp.log1p(jnp.exp(-jnp.abs(x)))


def _ssd_kernel(xs_ref, bm_ref, cm_ref, dt_ref, bias_ref, a_ref, y_ref, h_sc):
    t = SSD_CHUNK
    d = pl.program_id(2)

    @pl.when(pl.program_id(3) == 0)
    def _():
        h_sc[...] = jnp.zeros_like(h_sc)

    dt = _softplus(dt_ref[0].astype(F32) + bias_ref[0])
    a = dt * a_ref[0]
    row = lax.broadcasted_iota(jnp.int32, (t, t), 0)
    col = lax.broadcasted_iota(jnp.int32, (t, t), 1)
    sgn = 1 - 2 * d
    mask = (row - col) * sgn >= 0
    tri = mask.astype(BF16)
    a_hi = a.astype(BF16)
    a_lo = (a - a_hi.astype(F32)).astype(BF16)
    cum = jnp.dot(tri, a_hi, preferred_element_type=F32) + jnp.dot(tri, a_lo, preferred_element_type=F32)
    total = jnp.sum(a, axis=0, keepdims=True)
    cum_t = cum.T
    e_cum = jnp.exp(cum)
    e_end = jnp.exp(total - cum)
    e_tot = jnp.exp(total)

    bm = bm_ref[0]
    cm = cm_ref[0]
    scores = lax.dot_general(cm, bm, (((1,), (1,)), ((), ())), preferred_element_type=F32)
    bm_t = bm.astype(F32).T.astype(BF16)
    lane = lax.broadcasted_iota(jnp.int32, (t, LANE), 1)
    first = lane < MB_HEAD_DIM
    width = MB_HPG * MB_HEAD_DIM
    expand = (lax.broadcasted_iota(jnp.int32, (LANE, width), 1) // MB_HEAD_DIM
              == lax.broadcasted_iota(jnp.int32, (LANE, width), 0)).astype(BF16)
    stack = jnp.concatenate([dt, e_cum, e_end, jnp.broadcast_to(e_tot, (8, LANE))], axis=0)
    s_hi = stack.astype(BF16)
    s_lo = (stack - s_hi.astype(F32)).astype(BF16)
    wide = (jnp.dot(s_hi, expand, preferred_element_type=F32) + jnp.dot(s_lo, expand, preferred_element_type=F32))
    dt_w, e_cum_w, e_end_w, e_tot_w = wide[:t], wide[t:2 * t], wide[2 * t:3 * t], wide[3 * t:3 * t + 1]

    for m in range(MB_HPG // 2):
        j = 2 * m
        cols = slice(m * LANE, (m + 1) * LANE)
        xdt = xs_ref[0, :, cols].astype(F32) * dt_w[:, cols]
        h = h_sc[:, cols]
        y = jnp.dot(cm, h.astype(BF16), preferred_element_type=F32) * e_cum_w[:, cols]
        xw = (xdt * e_end_w[:, cols]).astype(BF16)
        h_sc[:, cols] = h * e_tot_w[:, cols] + jnp.dot(bm_t, xw, preferred_element_type=F32)
        xdt_b = xdt.astype(BF16)
        for q in range(2):
            diff = cum[:, j + q:j + q + 1] - cum_t[j + q:j + q + 1, :]
            decay = jnp.where(mask, jnp.exp(jnp.minimum(diff, 0.0)), 0.0)
            keep = first if q == 0 else jnp.logical_not(first)
            y = y + jnp.dot((scores * decay).astype(BF16), jnp.where(keep, xdt_b, jnp.zeros_like(xdt_b)),
                            preferred_element_type=F32)
        y_ref[0, 0, :, cols] = y


def _ssd(xbc, proj, dt_bias4, a4, n_lat):
    b, lt, _ = xbc.shape
    t = SSD_CHUNK
    n_lat_c = n_lat // t
    n_ctx_c = (lt - n_lat) // t
    nc = n_lat_c + n_ctx_c
    width = MB_HPG * MB_HEAD_DIM

    def chunk(d, i):
        fwd = jnp.where(i < n_ctx_c, n_lat_c + i, i - n_ctx_c)
        bwd = jnp.where(i < n_ctx_c, n_lat_c + n_ctx_c - 1 - i, n_lat_c - 1 - (i - n_ctx_c))
        return jnp.where(d == 0, fwd, bwd)

    return pl.pallas_call(
        _ssd_kernel,
        grid=(b, MB_GROUPS, 2, nc),
        in_specs=[pl.BlockSpec((1, t, width), lambda bi, g, d, i: (bi, chunk(d, i), g)),
                  pl.BlockSpec((1, t, MB_STATE), lambda bi, g, d, i: (bi, chunk(d, i), MB_INNER // MB_STATE + g)),
                  pl.BlockSpec((1, t, MB_STATE), lambda bi, g, d, i: (bi, chunk(d, i), MB_INNER // MB_STATE + MB_GROUPS + g)),
                  pl.BlockSpec((1, t, LANE), lambda bi, g, d, i: (bi, chunk(d, i), OFF_DT // LANE + d * MB_GROUPS + g)),
                  pl.BlockSpec((1, 1, LANE), lambda bi, g, d, i: (d * MB_GROUPS + g, 0, 0)),
                  pl.BlockSpec((1, 1, LANE), lambda bi, g, d, i: (d * MB_GROUPS + g, 0, 0))],
        out_specs=pl.BlockSpec((1, 1, t, width), lambda bi, g, d, i: (bi, d, chunk(d, i), g)),
        out_shape=jax.ShapeDtypeStruct((b, 2, lt, MB_INNER), F32),
        scratch_shapes=[pltpu.VMEM((MB_STATE, width), F32)],
        compiler_params=_cp(("parallel", "parallel", "parallel", "arbitrary")),
        name="ssd_scan",
    )(xbc, xbc, xbc, proj, dt_bias4, a4)


def _mamba_finish_kernel(yf_ref, yb_ref, xs_ref, z_ref, d_ref, g_ref, o_ref):
    y = yf_ref[0, 0] + yb_ref[0, 0] + xs_ref[0].astype(F32) * d_ref[...]
    y = y * _silu(z_ref[0].astype(F32))
    o_ref[0] = (y * lax.rsqrt(jnp.mean(y * y, axis=-1, keepdims=True) + EPS) * g_ref[...]).astype(o_ref.dtype)


def _mamba_finish(y2, xbc, proj, d_vec, gain):
    b, lt, _ = xbc.shape
    tm = _tile(lt, 640, 256)
    w = MB_INNER
    return pl.pallas_call(
        _mamba_finish_kernel,
        grid=(b, lt // tm),
        in_specs=[pl.BlockSpec((1, 1, tm, w), lambda bi, i: (bi, 0, i, 0)),
                  pl.BlockSpec((1, 1, tm, w), lambda bi, i: (bi, 1, i, 0)),
                  pl.BlockSpec((1, tm, w), lambda bi, i: (bi, i, 0)),
                  pl.BlockSpec((1, tm, w), lambda bi, i: (bi, i, OFF_MZ // w)),
                  pl.BlockSpec((1, w), lambda bi, i: (0, 0)),
                  pl.BlockSpec((1, w), lambda bi, i: (0, 0))],
        out_specs=pl.BlockSpec((1, tm, w), lambda bi, i: (bi, i, 0)),
        out_shape=jax.ShapeDtypeStruct((b, lt, w), BF16),
        compiler_params=_cp(("parallel", "parallel")),
        name="mamba_finish",
    )(y2, y2, xbc, proj, d_vec, gain.reshape(1, -1))


S5_LANE_GROUPS = LANE // S5_GROUP
S5_BLOCKS = S5_GROUPS // S5_LANE_GROUPS
S5_FLAT = S5_CHUNK * LANE
S5_STATE_W = S5_LANE_GROUPS * 2 * S5_STATE


def _s5_in_kernel(u_ref, w_ref, yi_ref, sf_ref, sfs_ref, sb_ref, sbs_ref, x_sc, *, cb):
    j = pl.program_id(2)

    @pl.when(j == 0)
    def _():
        x_sc[...] = jnp.concatenate([u_ref[pl.ds(s, cb, stride=S5_CHUNK), :] for s in range(S5_CHUNK)],
                                    axis=-1).astype(BF16)

    r = jnp.dot(x_sc[...], w_ref[0, 0], preferred_element_type=F32)

    @pl.when(j == 0)
    def _():
        yi_ref[...] = r.astype(yi_ref.dtype)

    @pl.when(j == 1)
    def _():
        sf_ref[...] = r[:, :S5_STATE_W]
        sfs_ref[...] = r[:, S5_STATE_W:]

    @pl.when(j == 2)
    def _():
        sb_ref[...] = r[:, :S5_STATE_W]
        sbs_ref[...] = r[:, S5_STATE_W:]


def _s5_in(u2, w):
    rows = u2.shape[0]
    m = rows // S5_CHUNK
    cb = _tile(m, 520, 96, 48)
    state_spec = pl.BlockSpec((cb, S5_STATE_W), lambda l, r, j: (r, l))
    state_shape = jax.ShapeDtypeStruct((m, S5_BLOCKS * S5_STATE_W), F32)
    return pl.pallas_call(
        functools.partial(_s5_in_kernel, cb=cb),
        grid=(S5_BLOCKS, m // cb, 3),
        in_specs=[pl.BlockSpec((cb * S5_CHUNK, LANE), lambda l, r, j: (r, l)),
                  pl.BlockSpec((1, 1, S5_FLAT, S5_FLAT), lambda l, r, j: (l, j, 0, 0))],
        out_specs=[pl.BlockSpec((cb, S5_FLAT), lambda l, r, j: (r, l))] + [state_spec] * 4,
        out_shape=[jax.ShapeDtypeStruct((m, S5_BLOCKS * S5_FLAT), BF16)] + [state_shape] * 4,
        scratch_shapes=[pltpu.VMEM((cb, S5_FLAT), BF16)],
        compiler_params=_cp(("parallel", "parallel", "arbitrary")),
        name="s5_intra",
    )(u2, w)


S5_SCAN_LANES = 1024


def _s5_scan_kernel(sf_ref, sfs_ref, sb_ref, sbs_ref, a_ref, hf_ref, hb_ref, st_sc, *, tc):
    @pl.when(pl.program_id(1) == 0)
    def _():
        st_sc[...] = jnp.zeros_like(st_sc)

    for lc in range(sf_ref.shape[1] // S5_SCAN_LANES):
        ln = slice(lc * S5_SCAN_LANES, (lc + 1) * S5_SCAN_LANES)
        a1f, a2f, a1b, a2b = a_ref[0:1, ln], a_ref[1:2, ln], a_ref[2:3, ln], a_ref[3:4, ln]

        def body(c, carry):
            hf, hfs, hb, hbs = carry
            hf_ref[pl.ds(c, 1), ln] = hf
            nf = a1f * hf + a2f * hfs + sf_ref[pl.ds(c, 1), ln]
            nfs = a1f * hfs - a2f * hf + sfs_ref[pl.ds(c, 1), ln]
            cb = tc - 1 - c
            hb_ref[pl.ds(cb, 1), ln] = hb
            nb = a1b * hb + a2b * hbs + sb_ref[pl.ds(cb, 1), ln]
            nbs = a1b * hbs - a2b * hb + sbs_ref[pl.ds(cb, 1), ln]
            return nf, nfs, nb, nbs

        out = lax.fori_loop(0, tc, body, tuple(st_sc[n:n + 1, ln] for n in range(4)))
        for n in range(4):
            st_sc[n:n + 1, ln] = out[n]


def _s5_scan(sf, sfs, sb, sbs, a_rows, n_batch, n_lat_chunks):
    m, lanes = sf.shape
    nc = m // n_batch
    tc = 16
    nlt = n_lat_chunks // tc
    nt = nc // tc

    def tile_f(bi, i):
        return bi * nt + jnp.where(i < nt - nlt, nlt + i, i - (nt - nlt))

    def tile_b(bi, i):
        return bi * nt + jnp.where(i < nt - nlt, nt - 1 - i, nlt - 1 - (i - (nt - nlt)))

    fwd = pl.BlockSpec((tc, lanes), lambda bi, i: (tile_f(bi, i), 0))
    bwd = pl.BlockSpec((tc, lanes), lambda bi, i: (tile_b(bi, i), 0))
    return pl.pallas_call(
        functools.partial(_s5_scan_kernel, tc=tc),
        grid=(n_batch, nt),
        in_specs=[fwd, fwd, bwd, bwd, pl.BlockSpec((8, lanes), lambda bi, i: (0, 0))],
        out_specs=[fwd, bwd],
        out_shape=[jax.ShapeDtypeStruct((m, lanes), F32), jax.ShapeDtypeStruct((m, lanes), F32)],
        scratch_shapes=[pltpu.VMEM((8, lanes), F32)],
        compiler_params=_cp(("parallel", "arbitrary")),
        name="s5_state_scan",
    )(sf, sfs, sb, sbs, a_rows)


def _s5_out_kernel(yi_ref, hf_ref, hb_ref, q_ref, o_ref, *, cb):
    h = jnp.concatenate([hf_ref[...], hb_ref[...]], axis=-1).astype(BF16)
    y = yi_ref[...].astype(F32) + jnp.dot(h, q_ref[0], preferred_element_type=F32)
    for t in range(S5_CHUNK):
        o_ref[pl.ds(t, cb, stride=S5_CHUNK), :] = y[:, t * LANE:(t + 1) * LANE]


def _s5_out(yi, hf, hb, q):
    m = yi.shape[0]
    cb = _tile(m, 520, 96, 48)
    return pl.pallas_call(
        functools.partial(_s5_out_kernel, cb=cb),
        grid=(S5_BLOCKS, m // cb),
        in_specs=[pl.BlockSpec((cb, S5_FLAT), lambda l, r: (r, l)),
                  pl.BlockSpec((cb, S5_STATE_W), lambda l, r: (r, l)),
                  pl.BlockSpec((cb, S5_STATE_W), lambda l, r: (r, l)),
                  pl.BlockSpec((1, 2 * S5_STATE_W, S5_FLAT), lambda l, r: (l, 0, 0))],
        out_specs=pl.BlockSpec((cb * S5_CHUNK, LANE), lambda l, r: (r, l)),
        out_shape=jax.ShapeDtypeStruct((m * S5_CHUNK, S5_WIDTH), F32),
        compiler_params=_cp(("parallel", "parallel")),
        name="s5_readout",
    )(yi, hf, hb, q)


def _gelu_tanh(x):
    return 0.5 * x * (1.0 + jnp.tanh(math.sqrt(2.0 / math.pi) * (x + 0.044715 * (x * x * x))))


def _s5_finish_kernel(y_ref, u_ref, d_ref, w_ref, o_ref):
    y = y_ref[0].astype(F32) + d_ref[...] * u_ref[0].astype(F32)
    g = _gelu_tanh(y)
    o_ref[0] = (g * _sigmoid(jnp.dot(g.astype(BF16), w_ref[...], preferred_element_type=F32))).astype(o_ref.dtype)


def _s5_finish(y, proj, d_vec, w_glu):
    b, lt, w = y.shape
    tm = _tile(lt, 640, 256)
    return pl.pallas_call(
        _s5_finish_kernel,
        grid=(b, lt // tm),
        in_specs=[pl.BlockSpec((1, tm, w), lambda bi, i: (bi, i, 0)),
                  pl.BlockSpec((1, tm, w), lambda bi, i: (bi, i, OFF_S5 // w)),
                  pl.BlockSpec((1, w), lambda bi, i: (0, 0)),
                  pl.BlockSpec((w, w), lambda bi, i: (0, 0))],
        out_specs=pl.BlockSpec((1, tm, w), lambda bi, i: (bi, i, 0)),
        out_shape=jax.ShapeDtypeStruct((b, lt, w), BF16),
        compiler_params=_cp(("parallel", "parallel")),
        name="s5_finish",
    )(y, proj, d_vec.reshape(1, -1), w_glu)


def _s5_operators(p):
    t = S5_CHUNK
    lam = lax.complex(p["s5_lam_re"].astype(F32), p["s5_lam_im"].astype(F32))
    step = jnp.exp(p["s5_log_step"].astype(F32))[..., None]
    lam_bar = jnp.exp(lam * step)
    b_bar = ((lam_bar - 1.0) / lam)[..., None] * lax.complex(p["s5_b_re"].astype(F32), p["s5_b_im"].astype(F32))
    c_mat = lax.complex(p["s5_c_re"].astype(F32), p["s5_c_im"].astype(F32))
    ks = jnp.arange(t + 1, dtype=F32)
    pw = jnp.exp((lam * step)[:, :, None, :] * ks[None, None, :, None])
    kern = jnp.real(jnp.einsum("dgcn,dgkn,dgni->dgkci", c_mat, pw[:, :, :t], b_bar))
    s_idx = jnp.arange(t)[:, None]
    t_idx = jnp.arange(t)[None, :]
    lag_f = jnp.clip(t_idx - s_idx, 0, t - 1)
    lag_b = jnp.clip(s_idx - t_idx, 0, t - 1)
    kf = jnp.where((t_idx >= s_idx)[None, :, :, None, None], kern[0][:, lag_f], 0.0)
    kb = jnp.where((s_idx >= t_idx)[None, :, :, None, None], kern[1][:, lag_b], 0.0)
    w_intra = jnp.transpose(kf + kb, (0, 1, 4, 2, 3)).reshape(S5_GROUPS, t * S5_GROUP, t * S5_GROUP)
    inj_f = pw[0][:, t - 1 - jnp.arange(t), :, None] * b_bar[0][:, None]
    inj_b = pw[1][:, jnp.arange(t), :, None] * b_bar[1][:, None]

    def inj_mat(z):
        z = jnp.transpose(z, (0, 1, 3, 2)).reshape(S5_GROUPS, t * S5_GROUP, S5_STATE)
        return jnp.concatenate([jnp.real(z), jnp.imag(z), jnp.imag(z), jnp.real(z)], axis=-1)

    lg, tt, gg = S5_LANE_GROUPS, t, S5_GROUP
    eye = jnp.eye(lg, dtype=BF16)

    def lift_in(mat, tail):
        m6 = mat.astype(BF16).reshape((S5_BLOCKS, lg, tt, gg) + tail)
        return m6

    wi = jnp.einsum("lgsitc,gh->lsgithc", lift_in(w_intra, (tt, gg)), eye).reshape(S5_BLOCKS, S5_FLAT, S5_FLAT)

    def lift_inj(z):
        return jnp.einsum("lgsivn,gh->lsgivhn", lift_in(z, (2, 2 * S5_STATE)), eye).reshape(
            S5_BLOCKS, S5_FLAT, 2 * S5_STATE_W)

    w_in = jnp.stack([wi, lift_inj(inj_mat(inj_f)), lift_inj(inj_mat(inj_b))], axis=1)
    m_f = c_mat[0][:, None, :, :] * pw[0][:, 1 + jnp.arange(t), None, :]
    m_b = c_mat[1][:, None, :, :] * pw[1][:, t - jnp.arange(t), None, :]

    def read_mat(z):
        z = jnp.transpose(z.reshape(S5_GROUPS, t * S5_GROUP, S5_STATE), (0, 2, 1))
        return jnp.concatenate([jnp.real(z), -jnp.imag(z)], axis=1).astype(BF16)

    def trans(z):
        re, im = jnp.real(z), jnp.imag(z)
        return [jnp.concatenate([re, re], -1).reshape(1, -1), jnp.concatenate([-im, im], -1).reshape(1, -1)]

    a_rows = jnp.concatenate(trans(pw[0][:, t]) + trans(pw[1][:, t])
                             + [jnp.zeros((4, 2 * S5_STATE * S5_GROUPS), F32)], axis=0)
    def lift_read(z):
        z6 = z.reshape(S5_BLOCKS, lg, 2 * S5_STATE, tt, gg)
        return jnp.einsum("lgntc,gh->lgnthc", z6, eye).reshape(S5_BLOCKS, S5_STATE_W, S5_FLAT)

    q = jnp.concatenate([lift_read(read_mat(m_f)), lift_read(read_mat(m_b))], axis=1)
    return dict(w_in=w_in, q=q, a_rows=a_rows)


def _s5_mixer(u32, proj, ops, d_vec, w_glu, n_lat):
    b, lt, _ = proj.shape
    yi, sf, sfs, sb, sbs = _s5_in(u32.reshape(b * lt, S5_WIDTH), ops["w_in"])
    hf, hb = _s5_scan(sf, sfs, sb, sbs, ops["a_rows"], b, n_lat // S5_CHUNK)
    y = _s5_out(yi, hf, hb, ops["q"]).reshape(b, lt, S5_WIDTH)
    return _s5_finish(y, proj, d_vec, w_glu)


def _merge_kernel(ya_ref, yb_ref, yc_ref, yd_ref, g0_ref, g1_ref, g2_ref, g3_ref,
                  wa_ref, wb_ref, wc_ref, wd_ref, o_ref):
    acc = None
    for y_ref, g_ref, w_ref in ((ya_ref, g0_ref, wa_ref), (yb_ref, g1_ref, wb_ref),
                                (yc_ref, g2_ref, wc_ref), (yd_ref, g3_ref, wd_ref)):
        term = _sigmoid(g_ref[0].astype(F32)) * jnp.dot(y_ref[0], w_ref[...], preferred_element_type=F32)
        acc = term if acc is None else acc + term
    o_ref[0] = acc.astype(o_ref.dtype)


def _merge(ys, proj, ws):
    b, lt, _ = proj.shape
    d = D_MODEL
    tm = _tile(lt, 640, 256)
    tn = 512
    nb = d // tn
    in_specs = [pl.BlockSpec((1, tm, y.shape[2]), lambda bi, i, j: (bi, i, 0)) for y in ys]
    in_specs += [pl.BlockSpec((1, tm, tn), functools.partial(lambda bi, i, j, br: (bi, i, br * nb + j), br=br))
                 for br in range(4)]
    in_specs += [pl.BlockSpec((w.shape[0], tn), lambda bi, i, j: (0, j)) for w in ws]
    return pl.pallas_call(
        _merge_kernel,
        grid=(b, lt // tm, nb),
        in_specs=in_specs,
        out_specs=pl.BlockSpec((1, tm, tn), lambda bi, i, j: (bi, i, j)),
        out_shape=jax.ShapeDtypeStruct((b, lt, d), BF16),
        compiler_params=_cp(("parallel", "parallel", "parallel")),
        name="branch_merge",
    )(*ys, proj, proj, proj, proj, *ws)


def _row_gate(ml, mc, row_tile, tm, n_lat, row):
    rows = row_tile * tm + lax.broadcasted_iota(jnp.int32, (tm, 1), 0)
    return jnp.where(rows >= n_lat, mc[row:row + 1], ml[row:row + 1])


def _outproj_kernel(m_ref, w_ref, x_ref, ml_ref, mc_ref, o_ref, *, tm, n_lat):
    gate = _row_gate(ml_ref[0], mc_ref[0], pl.program_id(1), tm, n_lat, 2)
    o_ref[0] = x_ref[0] + gate * jnp.dot(m_ref[0], w_ref[...], preferred_element_type=F32)


def _outproj(m, w_out, x, mods, n_lat):
    b, lt, d = x.shape
    tm = _tile(lt, 1280, 256)
    tn = 512
    return pl.pallas_call(
        functools.partial(_outproj_kernel, tm=tm, n_lat=n_lat),
        grid=(b, lt // tm, d // tn),
        in_specs=[pl.BlockSpec((1, tm, d), lambda bi, i, j: (bi, i, 0)),
                  pl.BlockSpec((d, tn), lambda bi, i, j: (0, j)),
                  pl.BlockSpec((1, tm, tn), lambda bi, i, j: (bi, i, j)),
                  pl.BlockSpec((1, 8, tn), lambda bi, i, j: (bi, 0, j)),
                  pl.BlockSpec((1, 8, tn), lambda bi, i, j: (b, 0, j))],
        out_specs=pl.BlockSpec((1, tm, tn), lambda bi, i, j: (bi, i, j)),
        out_shape=jax.ShapeDtypeStruct((b, lt, d), F32),
        compiler_params=_cp(("parallel", "parallel", "parallel")),
        name="out_proj",
    )(m, w_out, x, mods, mods)


def _prenorm_router_kernel(x_ref, g_ref, ml_ref, mc_ref, wr_ref, br_ref, h_ref, lg_ref, *, tm, n_lat):
    def chunk(r, carry):
        rs = pl.multiple_of(r * ROW_CHUNK, ROW_CHUNK)
        h = _norm_mod(x_ref[0, pl.ds(rs, ROW_CHUNK), :], g_ref[...], ml_ref[0], mc_ref[0],
                      pl.program_id(1) * tm + rs, n_lat, 3).astype(BF16)
        h_ref[0, pl.ds(rs, ROW_CHUNK), :] = h
        lg_ref[0, pl.ds(rs, ROW_CHUNK), :] = jnp.dot(h, wr_ref[...], preferred_element_type=F32) + br_ref[...]
        return carry

    lax.fori_loop(0, tm // ROW_CHUNK, chunk, 0)


def _prenorm_router(x, gain, mods, w_r, b_r, n_lat):
    b, lt, d = x.shape
    tm = _tile(lt, 640, 256)
    return pl.pallas_call(
        functools.partial(_prenorm_router_kernel, tm=tm, n_lat=n_lat),
        grid=(b, lt // tm),
        in_specs=[pl.BlockSpec((1, tm, d), lambda bi, i: (bi, i, 0)),
                  pl.BlockSpec((1, d), lambda bi, i: (0, 0)),
                  pl.BlockSpec((1, 8, d), lambda bi, i: (bi, 0, 0)),
                  pl.BlockSpec((1, 8, d), lambda bi, i: (b, 0, 0)),
                  pl.BlockSpec((d, LANE), lambda bi, i: (0, 0)),
                  pl.BlockSpec((1, LANE), lambda bi, i: (0, 0))],
        out_specs=[pl.BlockSpec((1, tm, d), lambda bi, i: (bi, i, 0)),
                   pl.BlockSpec((1, tm, LANE), lambda bi, i: (bi, i, 0))],
        out_shape=[jax.ShapeDtypeStruct((b, lt, d), BF16),
                   jax.ShapeDtypeStruct((b, lt, LANE), F32)],
        compiler_params=_cp(("parallel", "parallel")),
        name="moe_prenorm_router",
    )(x, gain.reshape(1, d), mods, mods, w_r, b_r)


def _moe_kernel(tile_ref, exp_ref, lo_ref, hi_ref, first_ref, x_ref, rw_ref, w1_ref, w3_ref, w2_ref, o_ref):
    k = pl.program_id(0)
    lo, hi = lo_ref[k], hi_ref[k]
    tile = x_ref.shape[0]

    @pl.when(hi > lo)
    def _():
        x = x_ref[...]
        hid = _silu(jnp.dot(x, w1_ref[0], preferred_element_type=F32)) * jnp.dot(x, w3_ref[0], preferred_element_type=F32)
        y = (jnp.dot(hid.astype(BF16), w2_ref[0], preferred_element_type=F32) * rw_ref[...]).astype(o_ref.dtype)
        rows = tile_ref[k] * tile + lax.broadcasted_iota(jnp.int32, (tile, 1), 0)
        mine = jnp.logical_and(rows >= lo, rows < hi)

        @pl.when(first_ref[k] == 1)
        def _():
            o_ref[...] = jnp.where(mine, y, jnp.zeros_like(y))

        @pl.when(first_ref[k] == 0)
        def _():
            o_ref[...] = jnp.where(mine, y, o_ref[...])


def _moe_ffn(xs, rw, items, w1, w3, w2):
    rows, d = xs.shape
    tile = MOE_TILE
    ff = w1.shape[2]
    item_tile, item_expert = items[0], items[1]
    grid_spec = pltpu.PrefetchScalarGridSpec(
        num_scalar_prefetch=5,
        grid=(item_tile.shape[0],),
        in_specs=[pl.BlockSpec((tile, d), lambda k, it, ie, lo, hi, fi: (it[k], 0)),
                  pl.BlockSpec((tile, 1), lambda k, it, ie, lo, hi, fi: (it[k], 0)),
                  pl.BlockSpec((1, d, ff), lambda k, it, ie, lo, hi, fi: (ie[k], 0, 0)),
                  pl.BlockSpec((1, d, ff), lambda k, it, ie, lo, hi, fi: (ie[k], 0, 0)),
                  pl.BlockSpec((1, ff, d), lambda k, it, ie, lo, hi, fi: (ie[k], 0, 0))],
        out_specs=pl.BlockSpec((tile, d), lambda k, it, ie, lo, hi, fi: (it[k], 0)),
    )
    return pl.pallas_call(
        _moe_kernel,
        grid_spec=grid_spec,
        out_shape=jax.ShapeDtypeStruct((rows, d), BF16),
        compiler_params=_cp(("arbitrary",)),
        name="moe_experts",
    )(*items, xs, rw, w1, w3, w2)


def _moe(h2, logits, w1, w3, w2):
    n, d = h2.shape
    tile = MOE_TILE
    g_prob = jax.nn.softmax(logits[:, :MOE_GROUPS], axis=-1)
    g_idx = jnp.argmax(g_prob, axis=-1)
    g_w = jnp.take_along_axis(g_prob, g_idx[:, None], axis=1)
    e_logits = logits[:, MOE_GROUPS:MOE_GROUPS + N_EXPERTS].reshape(n, MOE_GROUPS, MOE_PER_GROUP)
    e_logits = jnp.take_along_axis(e_logits, g_idx[:, None, None], axis=1)[:, 0]
    e_w, e_idx = lax.top_k(jax.nn.softmax(e_logits, axis=-1), 2)
    w = g_w * e_w / jnp.sum(e_w, axis=-1, keepdims=True)
    expert = (g_idx[:, None] * MOE_PER_GROUP + e_idx).astype(jnp.int32)

    n_rows = 2 * n
    assert n_rows % tile == 0
    n_tiles = n_rows // tile
    iota = jnp.arange(n_rows, dtype=jnp.int32)
    sorted_e, order, sorted_w = lax.sort((expert.reshape(-1), iota, w.reshape(-1)), num_keys=1, is_stable=True)
    _, sorted_pos = lax.sort((order, iota), num_keys=1)
    e_ids = jnp.arange(N_EXPERTS, dtype=jnp.int32)
    cnt_end = jnp.sum((sorted_e[None, :] <= e_ids[:, None]).astype(jnp.int32), axis=1)
    cnt_start = jnp.concatenate([jnp.zeros((1,), jnp.int32), cnt_end[:-1]])
    first_tile = cnt_start // tile
    last_tile = jnp.where(cnt_end > cnt_start, (cnt_end - 1) // tile, first_tile - 1)
    item_end = jnp.cumsum(last_tile - first_tile + 1)
    item_start = jnp.concatenate([jnp.zeros((1,), jnp.int32), item_end[:-1]])
    n_items = item_end[-1]
    k = jnp.arange(n_tiles + N_EXPERTS, dtype=jnp.int32)
    item_e = jnp.minimum(jnp.sum((item_end[None, :] <= k[:, None]).astype(jnp.int32), axis=1), N_EXPERTS - 1)
    item_t = first_tile[item_e] + k - item_start[item_e]
    valid = k < n_items
    item_e = jnp.where(valid, item_e, item_e[n_items - 1])
    item_t = jnp.where(valid, item_t, item_t[n_items - 1])
    item_lo = jnp.where(valid, jnp.maximum(cnt_start[item_e], item_t * tile), 0)
    item_hi = jnp.where(valid, jnp.minimum(cnt_end[item_e], (item_t + 1) * tile), 0)
    item_first = jnp.concatenate([jnp.ones((1,), jnp.int32), (item_t[1:] != item_t[:-1]).astype(jnp.int32)])
    items = tuple(a.astype(jnp.int32) for a in (item_t, item_e, item_lo, item_hi, item_first))

    xs = jnp.take(h2, order // 2, axis=0)
    ys = _moe_ffn(xs, sorted_w[:, None], items, w1, w3, w2)
    dest = sorted_pos.reshape(n, 2)
    return jnp.take(ys, dest[:, 0], axis=0).astype(F32) + jnp.take(ys, dest[:, 1], axis=0).astype(F32)


def _pack_w_in(w_in):
    depth, d, _ = w_in.shape
    cuts = {}
    off = 0
    for name, width in (("s5", S5_WIDTH), ("gq", GQA_OUT), ("gk", GQA_KV_HEADS * HEAD_DIM), ("gv", GQA_KV_HEADS * HEAD_DIM),
                        ("mq", MLA_Q_RANK), ("mkv", MLA_KV_RANK), ("mkr", MLA_ROPE), ("mz", MB_INNER), ("xbc", MB_XBC),
                        ("dt", 2 * MB_HEADS), ("gate", 4 * D_MODEL)):
        cuts[name] = w_in[:, :, off:off + width]
        off += width
    zeros = lambda n: jnp.zeros((depth, d, n), w_in.dtype)
    dt_blocks = []
    for blk in range(2 * MB_GROUPS):
        dt_blocks += [cuts["dt"][:, :, blk * MB_HPG:(blk + 1) * MB_HPG], zeros(LANE - MB_HPG)]
    packed = jnp.concatenate(
        [cuts["gate"], cuts["gq"], cuts["mz"], cuts["xbc"], cuts["mq"], cuts["s5"], cuts["gk"], cuts["gv"],
         cuts["mkv"], cuts["mkr"], zeros(LANE - MLA_ROPE)] + dt_blocks + [zeros(NW - OFF_DT - 4 * LANE)], axis=-1)
    return packed.astype(BF16)


def _rope_table(n_lat, n_ctx, dim, width):
    rows = n_lat // GRID_W
    row = jnp.repeat(jnp.arange(rows, dtype=F32), GRID_W)
    col = jnp.tile(jnp.arange(GRID_W, dtype=F32), rows)
    quarter = dim // 4
    inv_freq = ROPE_THETA ** (-jnp.arange(quarter, dtype=F32) / quarter)
    ang_r = row[:, None] * inv_freq
    ang_c = col[:, None] * inv_freq
    ang = jnp.concatenate([ang_r, ang_r, ang_c, ang_c], axis=-1)
    cos = jnp.concatenate([jnp.cos(ang), jnp.ones((n_lat, width - dim), F32)], axis=-1)
    sin = jnp.concatenate([jnp.sin(ang), jnp.zeros((n_lat, width - dim), F32)], axis=-1)
    cos = jnp.concatenate([cos, jnp.ones((n_ctx, width), F32)], axis=0)
    sin = jnp.concatenate([sin, jnp.zeros((n_ctx, width), F32)], axis=0)
    return cos, sin


def _rot_matrix(dim, width):
    quarter = dim // 4
    idx = jnp.arange(width)
    in_first = (idx % (2 * quarter)) < quarter
    valid = idx < dim
    src_for_first = idx + quarter
    src_for_second = idx - quarter
    rows = jnp.arange(width)[:, None]
    r = jnp.where(in_first[None, :] & (rows == src_for_first[None, :]), -1.0, 0.0)
    r = r + jnp.where((~in_first)[None, :] & (rows == src_for_second[None, :]), 1.0, 0.0)
    return (r * valid[None, :]).astype(BF16)


def _pad_lanes(v, width):
    return jnp.concatenate([v.astype(F32), jnp.zeros((width - v.shape[0],), F32)]).reshape(1, width)


def _lane_row(lane, value):
    return jnp.zeros((1, LANE), F32).at[0, lane].set(value)


def _score_bound(q_sq_norm, k_sq_norm, scale):
    return (jnp.sqrt(q_sq_norm * k_sq_norm) * (scale * LOG2E * 1.02) + 0.5).astype(F32)


def kernel(x, c, ctx, c_ctx, norm1, norm2, w_ada, b_ada, w_in, s5_lam_re, s5_lam_im, s5_log_step, s5_b_re, s5_b_im, s5_c_re, s5_c_im, s5_d, s5_w_glu, gqa_q_norm, gqa_k_norm, mla_q_norm, mla_kv_norm, mla_w_uq, mla_w_uk, mla_w_uv, mla_qn_norm, mla_kn_norm, mla_qr_norm, mla_kr_norm, mb_conv_w, mb_conv_b, mb_dt_bias, mb_a_log, mb_d, mb_norm, w_br_s5, w_br_gqa, w_br_mla, w_br_mb, w_out, moe_w_group, moe_b_group, moe_w_expert, moe_b_expert, moe_w1, moe_w3, moe_w2):
    b, n_lat, d = x.shape
    n_ctx = ctx.shape[1]
    lt = n_lat + n_ctx
    depth = w_in.shape[0]

    cvec = jnp.zeros((8, d), F32).at[:b].set(c.astype(F32)).at[b].set(c_ctx.astype(F32))
    mod_all = _ada(cvec, w_ada, b_ada).reshape(depth, 8, N_MOD, d)[:, :b + 1]
    mod_all = jnp.concatenate([mod_all, jnp.zeros((depth, b + 1, 8 - N_MOD, d), F32)], axis=2)

    w_in_p = _pack_w_in(w_in)
    cos_g, sin_g = _rope_table(n_lat, n_ctx, HEAD_DIM, HEAD_DIM)
    cos_m, sin_m = _rope_table(n_lat, n_ctx, MLA_ROPE, LANE)
    rot_g = _rot_matrix(HEAD_DIM, HEAD_DIM)
    rot_m = _rot_matrix(MLA_ROPE, LANE)
    ka_g = jnp.broadcast_to(_lane_row(0, 1.0).astype(BF16), (b, lt, LANE))

    uq = mla_w_uq.reshape(depth, MLA_Q_RANK, MLA_HEADS, MLA_NOPE + MLA_ROPE)
    wq_nope = uq[..., :MLA_NOPE].reshape(depth, MLA_Q_RANK, MLA_HEADS * MLA_NOPE).astype(BF16)
    wq_rope = jnp.concatenate([uq[..., MLA_NOPE:], jnp.zeros(uq.shape[:3] + (LANE - MLA_ROPE,), uq.dtype)],
                              axis=-1).reshape(depth, MLA_Q_RANK, MLA_HEADS * LANE).astype(BF16)
    a_dec = -jnp.exp(mb_a_log.astype(F32))

    xs = jnp.concatenate([x.astype(F32), ctx.astype(F32)], axis=1)

    for i in range(depth):
        mods = mod_all[i]
        proj, u32 = _inproj(xs, norm1[i], mods, w_in_p[i], n_lat)

        s5p = dict(s5_lam_re=s5_lam_re[i], s5_lam_im=s5_lam_im[i], s5_log_step=s5_log_step[i], s5_b_re=s5_b_re[i],
                   s5_b_im=s5_b_im[i], s5_c_re=s5_c_re[i], s5_c_im=s5_c_im[i])
        ya = _s5_mixer(u32, proj, _s5_operators(s5p), s5_d[i].astype(F32), s5_w_glu[i].astype(BF16), n_lat)

        shift_g = _score_bound(HEAD_DIM * jnp.max(jnp.abs(gqa_q_norm[i])) ** 2,
                               HEAD_DIM * jnp.max(jnp.abs(gqa_k_norm[i])) ** 2, HEAD_DIM ** -0.5)
        qg, kg = _gqa_prep(proj, cos_g, sin_g, rot_g, gqa_q_norm[i].astype(F32), gqa_k_norm[i].astype(F32),
                           _lane_row(0, -shift_g))
        yb = _attention(qg, kg, ka_g, proj, shift_g, n_kv_heads=GQA_KV_HEADS, grp=GQA_HEADS // GQA_KV_HEADS,
                        n_lat=n_lat, v_col0=OFF_GV // HEAD_DIM, tq_lat=1024, tk_lat=3328)

        mp = dict(mla_q_norm=mla_q_norm[i].astype(F32), mla_kv_norm=mla_kv_norm[i].astype(F32),
                  wq_nope=wq_nope[i], wq_rope=wq_rope[i], w_uk=mla_w_uk[i].astype(BF16), w_uv=mla_w_uv[i].astype(BF16),
                  mla_qn_norm=mla_qn_norm[i].astype(F32), mla_kn_norm=mla_kn_norm[i].astype(F32),
                  qr_gain=_pad_lanes(mla_qr_norm[i], LANE), kr_gain=_pad_lanes(mla_kr_norm[i], LANE))
        shift_m = _score_bound(
            MLA_NOPE * jnp.max(jnp.abs(mla_qn_norm[i])) ** 2 + MLA_ROPE * jnp.max(jnp.abs(mla_qr_norm[i])) ** 2,
            MLA_NOPE * jnp.max(jnp.abs(mla_kn_norm[i])) ** 2 + MLA_ROPE * jnp.max(jnp.abs(mla_kr_norm[i])) ** 2,
            (MLA_NOPE + MLA_ROPE) ** -0.5)
        mp["q_shift_row"] = _lane_row(MLA_ROPE, -shift_m)
        mp["k_one_row"] = _lane_row(MLA_ROPE, 1.0)
        qm, kn, kr, vm = _mla_prep(proj, cos_m, sin_m, rot_m, mp)
        yc = _attention(qm, kn, kr, vm, shift_m, n_kv_heads=MLA_HEADS, grp=1,
                        n_lat=n_lat, v_col0=0, tq_lat=4096, tk_lat=3328)

        xbc = _mamba_conv(proj, mb_conv_w[i], mb_conv_b[i], n_lat)
        dt_bias4 = jnp.concatenate([mb_dt_bias[i].astype(F32).reshape(2 * MB_GROUPS, MB_HPG),
                                    jnp.zeros((2 * MB_GROUPS, LANE - MB_HPG), F32)], axis=-1).reshape(2 * MB_GROUPS, 1, LANE)
        a4 = jnp.concatenate([a_dec[i].reshape(2 * MB_GROUPS, MB_HPG),
                              jnp.zeros((2 * MB_GROUPS, LANE - MB_HPG), F32)], axis=-1).reshape(2 * MB_GROUPS, 1, LANE)
        y2 = _ssd(xbc, proj, dt_bias4, a4, n_lat)
        d_vec = jnp.repeat(mb_d[i].astype(F32), MB_HEAD_DIM).reshape(1, MB_INNER)
        yd = _mamba_finish(y2, xbc, proj, d_vec, mb_norm[i].astype(F32))

        merged = _merge((ya, yb, yc, yd), proj,
                        (w_br_s5[i].astype(BF16), w_br_gqa[i].astype(BF16), w_br_mla[i].astype(BF16), w_br_mb[i].astype(BF16)))
        xs = _outproj(merged, w_out[i].astype(BF16), xs, mods, n_lat)

        w_r = jnp.concatenate([moe_w_group[i], moe_w_expert[i],
                               jnp.zeros((d, LANE - MOE_GROUPS - N_EXPERTS), F32)], axis=-1).astype(BF16)
        b_r = _pad_lanes(jnp.concatenate([moe_b_group[i], moe_b_expert[i]]), LANE)
        h2, logits = _prenorm_router(xs, norm2[i], mods, w_r, b_r, n_lat)
        ff = _moe(h2.reshape(b * lt, d), logits.reshape(b * lt, LANE),
                  moe_w1[i].astype(BF16), moe_w3[i].astype(BF16), moe_w2[i].astype(BF16)).reshape(b, lt, d)
        gate2 = jnp.concatenate([jnp.broadcast_to(mods[:b, 5][:, None, :], (b, n_lat, d)),
                                 jnp.broadcast_to(mods[b, 5][None, None, :], (b, n_ctx, d))], axis=1)
        xs = xs + gate2 * ff

    return xs[:, :n_lat].astype(x.dtype)
```

```python
import functools
import math

import numpy as np
import jax
import jax.numpy as jnp
from jax import lax
from jax.experimental import pallas as pl
from jax.experimental.pallas import tpu as pltpu

F32 = jnp.float32
BF16 = jnp.bfloat16
HIGHEST = lax.Precision.HIGHEST

EPS = 1e-6
ROPE_THETA = 10000.0
GRID_W = 64
D_MODEL = 2048
N_MOD = 6

S5_GROUP = 16
S5_WIDTH = 768
S5_GROUPS = S5_WIDTH // S5_GROUP
S5_STATE = 64
S5_CHUNK = 16

GQA_HEADS = 8
GQA_KV_HEADS = 2
HEAD_DIM = 128
GQA_OUT = GQA_HEADS * HEAD_DIM

MLA_HEADS = 8
MLA_Q_RANK = 512
MLA_KV_RANK = 256
MLA_NOPE = 128
MLA_ROPE = 64
MLA_OUT = MLA_HEADS * HEAD_DIM

MB_HEADS = 16
MB_HEAD_DIM = 64
MB_INNER = MB_HEADS * MB_HEAD_DIM
MB_GROUPS = 2
MB_HPG = MB_HEADS // MB_GROUPS
MB_STATE = 128
MB_CONV = 5
MB_XBC = MB_INNER + 2 * MB_GROUPS * MB_STATE
SSD_CHUNK = 128

MOE_GROUPS = 4
MOE_PER_GROUP = 8
N_EXPERTS = MOE_GROUPS * MOE_PER_GROUP
MOE_FF = 512
MOE_TILE = 512

LANE = 128
VMEM_LIMIT = 56 * 1024 * 1024
LOG2E = math.log2(math.e)
FLASH_ROW_BLOCK = 256
FLASH_DQ = 256
FLASH_MAX_SHIFT = 60.0

OFF_GATE = 0
OFF_GQ = 8192
OFF_MZ = 9216
OFF_XBC = 10240
OFF_MQ = 11776
OFF_S5 = 12288
OFF_GK = 13056
OFF_GV = 13312
OFF_MKV = 13568
OFF_MKR = 13824
OFF_DT = 13952
NW = 14592


def _tile(n, *cands):
    for c in cands:
        if n % c == 0:
            return c
    return n


def _cp(sem, vmem=VMEM_LIMIT):
    return pltpu.CompilerParams(dimension_semantics=sem, vmem_limit_bytes=vmem)


def _sigmoid(x):
    return 1.0 / (1.0 + jnp.exp(-x))


def _silu(x):
    return x * _sigmoid(x)


def _ada_kernel(c_ref, w_ref, b_ref, o_ref):
    s = _silu(c_ref[...])
    o_ref[0] = jnp.dot(s.astype(BF16), w_ref[0].astype(BF16), preferred_element_type=F32) + b_ref[0]


def _ada(cvec, w_ada, b_ada):
    depth, d, n = w_ada.shape
    tn = _tile(n, 1024, 512)
    return pl.pallas_call(
        _ada_kernel,
        grid=(depth, n // tn),
        in_specs=[pl.BlockSpec((8, d), lambda l, j: (0, 0)),
                  pl.BlockSpec((1, d, tn), lambda l, j: (l, 0, j)),
                  pl.BlockSpec((1, 1, tn), lambda l, j: (l, 0, j))],
        out_specs=pl.BlockSpec((1, 8, tn), lambda l, j: (l, 0, j)),
        out_shape=jax.ShapeDtypeStruct((depth, 8, n), F32),
        compiler_params=_cp(("parallel", "parallel")),
        name="ada_mod",
    )(cvec, w_ada, b_ada.reshape(depth, 1, n))


ROW_CHUNK = 128


def _norm_mod(x, g, ml, mc, row_start, n_lat, row0):
    r = lax.rsqrt(jnp.mean(x * x, axis=-1, keepdims=True) + EPS)
    rows = row_start + lax.broadcasted_iota(jnp.int32, (x.shape[0], 1), 0)
    is_ctx = rows >= n_lat
    shift = jnp.where(is_ctx, mc[row0:row0 + 1], ml[row0:row0 + 1])
    scale = jnp.where(is_ctx, mc[row0 + 1:row0 + 2], ml[row0 + 1:row0 + 2])
    return x * r * g * (1.0 + scale) + shift


def _inproj_kernel(x_ref, g_ref, ml_ref, mc_ref, w_ref, o_ref, u_ref, h_sc, *, n_lat, tm, s5_tile):
    @pl.when(pl.program_id(2) == 0)
    def _():
        def chunk(r, carry):
            rs = pl.multiple_of(r * ROW_CHUNK, ROW_CHUNK)
            h = _norm_mod(x_ref[0, pl.ds(rs, ROW_CHUNK), :], g_ref[...], ml_ref[0], mc_ref[0],
                          pl.program_id(1) * tm + rs, n_lat, 0)
            h_sc[pl.ds(rs, ROW_CHUNK), :] = h.astype(BF16)
            return carry

        lax.fori_loop(0, tm // ROW_CHUNK, chunk, 0)

    acc = jnp.dot(h_sc[...], w_ref[...], preferred_element_type=F32)
    o_ref[0] = acc.astype(o_ref.dtype)

    @pl.when(pl.program_id(2) == s5_tile)
    def _():
        u_ref[0] = acc


def _inproj(x, gain, mods, w, n_lat):
    b, lt, d = x.shape
    n = w.shape[1]
    tm = _tile(lt, 1280, 256)
    tn = S5_WIDTH
    assert n % tn == 0 and OFF_S5 % tn == 0
    return pl.pallas_call(
        functools.partial(_inproj_kernel, n_lat=n_lat, tm=tm, s5_tile=OFF_S5 // tn),
        grid=(b, lt // tm, n // tn),
        in_specs=[pl.BlockSpec((1, tm, d), lambda bi, i, j: (bi, i, 0)),
                  pl.BlockSpec((1, d), lambda bi, i, j: (0, 0)),
                  pl.BlockSpec((1, 8, d), lambda bi, i, j: (bi, 0, 0)),
                  pl.BlockSpec((1, 8, d), lambda bi, i, j: (b, 0, 0)),
                  pl.BlockSpec((d, tn), lambda bi, i, j: (0, j))],
        out_specs=[pl.BlockSpec((1, tm, tn), lambda bi, i, j: (bi, i, j)),
                   pl.BlockSpec((1, tm, tn), lambda bi, i, j: (bi, i, 0))],
        out_shape=[jax.ShapeDtypeStruct((b, lt, n), BF16),
                   jax.ShapeDtypeStruct((b, lt, tn), F32)],
        scratch_shapes=[pltpu.VMEM((tm, d), BF16)],
        compiler_params=_cp(("parallel", "parallel", "arbitrary")),
        name="in_proj",
    )(x, gain.reshape(1, d), mods, mods, w)


def _head_norm_rope(x, gain, cos, sin, rmat, n_valid, scale):
    ms = jnp.sum(x * x, axis=-1, keepdims=True) * (1.0 / n_valid)
    y = x * lax.rsqrt(ms + EPS) * gain
    if rmat is not None:
        rot = jnp.dot(y.astype(BF16), rmat, preferred_element_type=F32)
        y = y * cos + rot * sin
    return y * scale


def _gqa_prep_kernel(q_ref, k_ref, cos_ref, sin_ref, r_ref, qg_ref, kg_ref, sh_ref, qo_ref, ko_ref):
    cos, sin, rmat = cos_ref[...], sin_ref[...], r_ref[...]
    scale = HEAD_DIM ** -0.5 * LOG2E
    aux = jnp.broadcast_to(sh_ref[...], (q_ref.shape[1], LANE)).astype(BF16)
    for h in range(GQA_HEADS):
        x = q_ref[0, :, h * HEAD_DIM:(h + 1) * HEAD_DIM].astype(F32)
        qo_ref[0, :, h * FLASH_DQ:h * FLASH_DQ + HEAD_DIM] = _head_norm_rope(
            x, qg_ref[...], cos, sin, rmat, HEAD_DIM, scale).astype(BF16)
        qo_ref[0, :, h * FLASH_DQ + HEAD_DIM:(h + 1) * FLASH_DQ] = aux
    for h in range(GQA_KV_HEADS):
        x = k_ref[0, :, h * HEAD_DIM:(h + 1) * HEAD_DIM].astype(F32)
        ko_ref[0, :, h * HEAD_DIM:(h + 1) * HEAD_DIM] = _head_norm_rope(
            x, kg_ref[...], cos, sin, rmat, HEAD_DIM, 1.0).astype(BF16)


def _gqa_prep(proj, cos, sin, rmat, q_gain, k_gain, shift_row):
    b, lt, _ = proj.shape
    tm = _tile(lt, 640, 256)
    kw = GQA_KV_HEADS * HEAD_DIM
    qw = GQA_HEADS * FLASH_DQ
    return pl.pallas_call(
        _gqa_prep_kernel,
        grid=(b, lt // tm),
        in_specs=[pl.BlockSpec((1, tm, GQA_OUT), lambda bi, i: (bi, i, OFF_GQ // GQA_OUT)),
                  pl.BlockSpec((1, tm, kw), lambda bi, i: (bi, i, OFF_GK // kw)),
                  pl.BlockSpec((tm, HEAD_DIM), lambda bi, i: (i, 0)),
                  pl.BlockSpec((tm, HEAD_DIM), lambda bi, i: (i, 0)),
                  pl.BlockSpec((HEAD_DIM, HEAD_DIM), lambda bi, i: (0, 0)),
                  pl.BlockSpec((1, HEAD_DIM), lambda bi, i: (0, 0)),
                  pl.BlockSpec((1, HEAD_DIM), lambda bi, i: (0, 0)),
                  pl.BlockSpec((1, LANE), lambda bi, i: (0, 0))],
        out_specs=[pl.BlockSpec((1, tm, qw), lambda bi, i: (bi, i, 0)),
                   pl.BlockSpec((1, tm, kw), lambda bi, i: (bi, i, 0))],
        out_shape=[jax.ShapeDtypeStruct((b, lt, qw), BF16),
                   jax.ShapeDtypeStruct((b, lt, kw), BF16)],
        compiler_params=_cp(("parallel", "parallel")),
        name="gqa_prep",
    )(proj, proj, cos, sin, rmat, q_gain.reshape(1, -1), k_gain.reshape(1, -1), shift_row)


def _mla_prep_kernel(cq_ref, ckv_ref, kr_ref, cos_ref, sin_ref, r_ref,
                     qn_g_ref, kvn_g_ref, wqn_ref, wqr_ref, wuk_ref, wuv_ref,
                     qnn_g_ref, qrn_g_ref, knn_g_ref, krn_g_ref, qsh_ref, kone_ref,
                     q_ref, kn_ref, kr_out_ref, v_ref):
    cos, sin, rmat = cos_ref[...], sin_ref[...], r_ref[...]
    scale = (MLA_NOPE + MLA_ROPE) ** -0.5 * LOG2E
    cq = cq_ref[0].astype(F32)
    cqn = (cq * lax.rsqrt(jnp.mean(cq * cq, axis=-1, keepdims=True) + EPS) * qn_g_ref[...]).astype(BF16)
    q_nope = jnp.dot(cqn, wqn_ref[...], preferred_element_type=F32)
    q_rope = jnp.dot(cqn, wqr_ref[...], preferred_element_type=F32)
    for h in range(MLA_HEADS):
        sl = slice(h * HEAD_DIM, (h + 1) * HEAD_DIM)
        qn = _head_norm_rope(q_nope[:, sl], qnn_g_ref[...], None, None, None, MLA_NOPE, scale)
        qr = _head_norm_rope(q_rope[:, sl], qrn_g_ref[...], cos, sin, rmat, MLA_ROPE, scale)
        q_ref[0, :, 2 * h * HEAD_DIM:(2 * h + 1) * HEAD_DIM] = qn.astype(BF16)
        q_ref[0, :, (2 * h + 1) * HEAD_DIM:(2 * h + 2) * HEAD_DIM] = (qr + qsh_ref[...]).astype(BF16)
    ckv = ckv_ref[0].astype(F32)
    lat = (ckv * lax.rsqrt(jnp.mean(ckv * ckv, axis=-1, keepdims=True) + EPS) * kvn_g_ref[...]).astype(BF16)
    k_nope = jnp.dot(lat, wuk_ref[...], preferred_element_type=F32)
    v_ref[0] = jnp.dot(lat, wuv_ref[...], preferred_element_type=F32).astype(BF16)
    for h in range(MLA_HEADS):
        sl = slice(h * HEAD_DIM, (h + 1) * HEAD_DIM)
        kn_ref[0, :, sl] = _head_norm_rope(k_nope[:, sl], knn_g_ref[...], None, None, None, MLA_NOPE, 1.0).astype(BF16)
    kr = kr_ref[0].astype(F32)
    kr_out_ref[0] = (_head_norm_rope(kr, krn_g_ref[...], cos, sin, rmat, MLA_ROPE, 1.0) + kone_ref[...]).astype(BF16)


def _mla_prep(proj, cos, sin, rmat, p):
    b, lt, _ = proj.shape
    tm = _tile(lt, 640, 256)
    hd = MLA_HEADS * HEAD_DIM
    full = lambda shape: pl.BlockSpec(shape, lambda bi, i: tuple(0 for _ in shape))
    return pl.pallas_call(
        _mla_prep_kernel,
        grid=(b, lt // tm),
        in_specs=[pl.BlockSpec((1, tm, MLA_Q_RANK), lambda bi, i: (bi, i, OFF_MQ // MLA_Q_RANK)),
                  pl.BlockSpec((1, tm, MLA_KV_RANK), lambda bi, i: (bi, i, OFF_MKV // MLA_KV_RANK)),
                  pl.BlockSpec((1, tm, LANE), lambda bi, i: (bi, i, OFF_MKR // LANE)),
                  pl.BlockSpec((tm, LANE), lambda bi, i: (i, 0)),
                  pl.BlockSpec((tm, LANE), lambda bi, i: (i, 0)),
                  full((LANE, LANE)),
                  full((1, MLA_Q_RANK)), full((1, MLA_KV_RANK)),
                  full((MLA_Q_RANK, hd)), full((MLA_Q_RANK, hd)),
                  full((MLA_KV_RANK, hd)), full((MLA_KV_RANK, hd)),
                  full((1, LANE)), full((1, LANE)), full((1, LANE)), full((1, LANE)),
                  full((1, LANE)), full((1, LANE))],
        out_specs=[pl.BlockSpec((1, tm, 2 * hd), lambda bi, i: (bi, i, 0)),
                   pl.BlockSpec((1, tm, hd), lambda bi, i: (bi, i, 0)),
                   pl.BlockSpec((1, tm, LANE), lambda bi, i: (bi, i, 0)),
                   pl.BlockSpec((1, tm, hd), lambda bi, i: (bi, i, 0))],
        out_shape=[jax.ShapeDtypeStruct((b, lt, 2 * hd), BF16),
                   jax.ShapeDtypeStruct((b, lt, hd), BF16),
                   jax.ShapeDtypeStruct((b, lt, LANE), BF16),
                   jax.ShapeDtypeStruct((b, lt, hd), BF16)],
        compiler_params=_cp(("parallel", "parallel")),
        name="mla_prep",
    )(proj, proj, proj, cos, sin, rmat,
      p["mla_q_norm"].reshape(1, -1), p["mla_kv_norm"].reshape(1, -1),
      p["wq_nope"], p["wq_rope"], p["w_uk"], p["w_uv"],
      p["mla_qn_norm"].reshape(1, -1), p["qr_gain"], p["mla_kn_norm"].reshape(1, -1), p["kr_gain"],
      p["q_shift_row"], p["k_one_row"])


def _stack_queries(q_ref, q_sc, grp, tq):
    for g in range(grp):
        q_sc[g * tq:(g + 1) * tq, :] = q_ref[0, :, g * FLASH_DQ:(g + 1) * FLASH_DQ]


def _write_heads(o_ref, o, grp, tq):
    for g in range(grp):
        o_ref[0, :, g * HEAD_DIM:(g + 1) * HEAD_DIM] = o[g * tq:(g + 1) * tq].astype(o_ref.dtype)


def _flash_online_kernel(q_ref, k_ref, ka_ref, v_ref, o_ref, q_sc, m_sc, l_sc, acc_sc, *, grp, tq):
    kv = pl.program_id(3)

    @pl.when(kv == 0)
    def _():
        _stack_queries(q_ref, q_sc, grp, tq)
        m_sc[...] = jnp.full_like(m_sc, -jnp.inf)
        l_sc[...] = jnp.zeros_like(l_sc)
        acc_sc[...] = jnp.zeros_like(acc_sc)

    k = jnp.concatenate([k_ref[0], ka_ref[0]], axis=-1)
    v = v_ref[0]
    n_chunks = k.shape[0] // LANE
    rb = min(FLASH_ROW_BLOCK, grp * tq)
    for r in range(grp * tq // rb):
        rows = slice(r * rb, (r + 1) * rb)
        s = lax.dot_general(q_sc[rows, :], k, (((1,), (1,)), ((), ())), preferred_element_type=F32)
        m_prev = m_sc[rows, :]
        m_new = jnp.maximum(m_prev, jnp.max(s, axis=-1, keepdims=True))
        alpha = jnp.exp2(m_prev - m_new)
        p = jnp.exp2(s - jnp.concatenate([m_new] * n_chunks, axis=-1))
        p_sum = p[:, :LANE]
        for c in range(1, n_chunks):
            p_sum = p_sum + p[:, c * LANE:(c + 1) * LANE]
        l_sc[rows, :] = alpha * l_sc[rows, :] + p_sum
        acc_sc[rows, :] = alpha * acc_sc[rows, :] + jnp.dot(p.astype(BF16), v, preferred_element_type=F32)
        m_sc[rows, :] = m_new

    @pl.when(kv == pl.num_programs(3) - 1)
    def _():
        _write_heads(o_ref, acc_sc[...] / jnp.sum(l_sc[...], axis=-1, keepdims=True), grp, tq)


def _flash_bounded_kernel(q_ref, k_ref, ka_ref, v_ref, o_ref, q_sc, acc_sc, *, grp, tq):
    kv = pl.program_id(3)

    @pl.when(kv == 0)
    def _():
        _stack_queries(q_ref, q_sc, grp, tq)
        acc_sc[...] = jnp.zeros_like(acc_sc)

    k = jnp.concatenate([k_ref[0], ka_ref[0]], axis=-1)
    v = v_ref[0]
    ones_col = (lax.broadcasted_iota(jnp.int32, v.shape, 1) == 0).astype(BF16)
    v1 = jnp.concatenate([v, ones_col], axis=-1)
    rb = min(FLASH_ROW_BLOCK, grp * tq)
    for r in range(grp * tq // rb):
        rows = slice(r * rb, (r + 1) * rb)
        s = lax.dot_general(q_sc[rows, :], k, (((1,), (1,)), ((), ())), preferred_element_type=F32)
        acc_sc[rows, :] += jnp.dot(jnp.exp2(s).astype(BF16), v1, preferred_element_type=F32)

    @pl.when(kv == pl.num_programs(3) - 1)
    def _():
        acc = acc_sc[...]
        _write_heads(o_ref, acc[:, :HEAD_DIM] / acc[:, HEAD_DIM:HEAD_DIM + 1], grp, tq)


def _flash(q, k, ka, v, *, bounded, n_kv_heads, grp, tq, tk, q_rows, q_off, kv_rows, kv_off, v_col0):
    b = q.shape[0]
    qo, ko = q_off // tq, kv_off // tk
    rows = grp * tq
    if bounded:
        body = _flash_bounded_kernel
        scratch = [pltpu.VMEM((rows, FLASH_DQ), BF16), pltpu.VMEM((rows, 2 * HEAD_DIM), F32)]
    else:
        body = _flash_online_kernel
        scratch = [pltpu.VMEM((rows, FLASH_DQ), BF16), pltpu.VMEM((rows, LANE), F32),
                   pltpu.VMEM((rows, LANE), F32), pltpu.VMEM((rows, HEAD_DIM), F32)]
    return pl.pallas_call(
        functools.partial(body, grp=grp, tq=tq),
        grid=(b, n_kv_heads, q_rows // tq, kv_rows // tk),
        in_specs=[pl.BlockSpec((1, tq, grp * FLASH_DQ), lambda bi, h, i, j: (bi, i + qo, h)),
                  pl.BlockSpec((1, tk, HEAD_DIM), lambda bi, h, i, j: (bi, j + ko, h)),
                  pl.BlockSpec((1, tk, LANE), lambda bi, h, i, j: (bi, j + ko, 0)),
                  pl.BlockSpec((1, tk, HEAD_DIM), lambda bi, h, i, j: (bi, j + ko, v_col0 + h))],
        out_specs=pl.BlockSpec((1, tq, grp * HEAD_DIM), lambda bi, h, i, j: (bi, i, h)),
        out_shape=jax.ShapeDtypeStruct((b, q_rows, n_kv_heads * grp * HEAD_DIM), BF16),
        scratch_shapes=scratch,
        compiler_params=_cp(("parallel", "parallel", "parallel", "arbitrary")),
        name="flash_bounded" if bounded else "flash_online",
    )(q, k, ka, v)


def _attention(q, k, ka, v, shift, *, n_kv_heads, grp, n_lat, v_col0, tq_lat, tk_lat):
    lt = q.shape[1]
    n_ctx = lt - n_lat
    tk = _tile(lt, tk_lat, 256)
    tq = _tile(n_lat, tq_lat, 256)

    def run(bounded):
        def go(q, k, ka, v):
            y_lat = _flash(q, k, ka, v, bounded=bounded, n_kv_heads=n_kv_heads, grp=grp, tq=tq, tk=tk,
                           q_rows=n_lat, q_off=0, kv_rows=lt, kv_off=0, v_col0=v_col0)
            y_ctx = _flash(q, k, ka, v, bounded=bounded, n_kv_heads=n_kv_heads, grp=grp, tq=n_ctx, tk=n_ctx,
                           q_rows=n_ctx, q_off=n_lat, kv_rows=n_ctx, kv_off=n_lat, v_col0=v_col0)
            return jnp.concatenate([y_lat, y_ctx], axis=1)
        return go

    return lax.cond(shift <= FLASH_MAX_SHIFT, run(True), run(False), q, k, ka, v)


def _conv_kernel(prev_ref, cur_ref, next_ref, w_ref, b_ref, o_ref, *, tr, n_lat, lt):
    rows = pl.program_id(1) * tr + lax.broadcasted_iota(jnp.int32, (tr, 1), 0)
    ext = jnp.concatenate([prev_ref[0].astype(F32), cur_ref[0].astype(F32), next_ref[0].astype(F32)], axis=0)
    w = w_ref[...]
    acc = jnp.zeros((tr, cur_ref.shape[2]), F32) + b_ref[...]
    half = MB_CONV // 2
    for kk in range(MB_CONV):
        src = rows + (kk - half)
        ok = jnp.logical_and(jnp.logical_and(src >= 0, src < lt), (src >= n_lat) == (rows >= n_lat))
        acc = acc + jnp.where(ok, ext[8 + kk - half:8 + kk - half + tr, :], 0.0) * w[kk:kk + 1, :]
    o_ref[0] = _silu(acc).astype(o_ref.dtype)


def _mamba_conv(proj, conv_w, conv_b, n_lat):
    b, lt, _ = proj.shape
    tr = _tile(lt, 1280, 256)
    cw = 512
    cb0 = OFF_XBC // cw
    r8 = tr // 8
    nblk8 = lt // 8
    w8 = jnp.zeros((8, MB_XBC), F32).at[:MB_CONV].set(conv_w.T.astype(F32))
    return pl.pallas_call(
        functools.partial(_conv_kernel, tr=tr, n_lat=n_lat, lt=lt),
        grid=(b, lt // tr, MB_XBC // cw),
        in_specs=[pl.BlockSpec((1, 8, cw), lambda bi, i, c: (bi, jnp.maximum(i * r8 - 1, 0), cb0 + c)),
                  pl.BlockSpec((1, tr, cw), lambda bi, i, c: (bi, i, cb0 + c)),
                  pl.BlockSpec((1, 8, cw), lambda bi, i, c: (bi, jnp.minimum((i + 1) * r8, nblk8 - 1), cb0 + c)),
                  pl.BlockSpec((8, cw), lambda bi, i, c: (0, c)),
                  pl.BlockSpec((1, cw), lambda bi, i, c: (0, c))],
        out_specs=pl.BlockSpec((1, tr, cw), lambda bi, i, c: (bi, i, c)),
        out_shape=jax.ShapeDtypeStruct((b, lt, MB_XBC), BF16),
        compiler_params=_cp(("parallel", "parallel", "parallel")),
        name="mamba_conv",
    )(proj, proj, proj, w8, conv_b.reshape(1, -1).astype(F32))


def _softplus(x):
    return jnp.maximum(x, 0.0) + jnp.log1p(jnp.exp(-jnp.abs(x)))


def _ssd_kernel(xs_ref, bm_ref, cm_ref, dt_ref, bias_ref, a_ref, y_ref, h_sc):
    t = SSD_CHUNK
    d = pl.program_id(2)

    @pl.when(pl.program_id(3) == 0)
    def _():
        h_sc[...] = jnp.zeros_like(h_sc)

    dt = _softplus(dt_ref[0].astype(F32) + bias_ref[0])
    a = dt * a_ref[0]
    row = lax.broadcasted_iota(jnp.int32, (t, t), 0)
    col = lax.broadcasted_iota(jnp.int32, (t, t), 1)
    sgn = 1 - 2 * d
    mask = (row - col) * sgn >= 0
    tri = mask.astype(BF16)
    a_hi = a.astype(BF16)
    a_lo = (a - a_hi.astype(F32)).astype(BF16)
    cum = jnp.dot(tri, a_hi, preferred_element_type=F32) + jnp.dot(tri, a_lo, preferred_element_type=F32)
    total = jnp.sum(a, axis=0, keepdims=True)
    cum_t = cum.T
    e_cum = jnp.exp(cum)
    e_end = jnp.exp(total - cum)
    e_tot = jnp.exp(total)

    bm = bm_ref[0]
    cm = cm_ref[0]
    scores = lax.dot_general(cm, bm, (((1,), (1,)), ((), ())), preferred_element_type=F32)
    bm_t = bm.astype(F32).T.astype(BF16)
    lane = lax.broadcasted_iota(jnp.int32, (t, LANE), 1)
    first = lane < MB_HEAD_DIM
    width = MB_HPG * MB_HEAD_DIM
    expand = (lax.broadcasted_iota(jnp.int32, (LANE, width), 1) // MB_HEAD_DIM
              == lax.broadcasted_iota(jnp.int32, (LANE, width), 0)).astype(BF16)
    stack = jnp.concatenate([dt, e_cum, e_end, jnp.broadcast_to(e_tot, (8, LANE))], axis=0)
    s_hi = stack.astype(BF16)
    s_lo = (stack - s_hi.astype(F32)).astype(BF16)
    wide = (jnp.dot(s_hi, expand, preferred_element_type=F32) + jnp.dot(s_lo, expand, preferred_element_type=F32))
    dt_w, e_cum_w, e_end_w, e_tot_w = wide[:t], wide[t:2 * t], wide[2 * t:3 * t], wide[3 * t:3 * t + 1]

    for m in range(MB_HPG // 2):
        j = 2 * m
        cols = slice(m * LANE, (m + 1) * LANE)
        xdt = xs_ref[0, :, cols].astype(F32) * dt_w[:, cols]
        h = h_sc[:, cols]
        y = jnp.dot(cm, h.astype(BF16), preferred_element_type=F32) * e_cum_w[:, cols]
        xw = (xdt * e_end_w[:, cols]).astype(BF16)
        h_sc[:, cols] = h * e_tot_w[:, cols] + jnp.dot(bm_t, xw, preferred_element_type=F32)
        xdt_b = xdt.astype(BF16)
        for q in range(2):
            diff = cum[:, j + q:j + q + 1] - cum_t[j + q:j + q + 1, :]
            decay = jnp.where(mask, jnp.exp(jnp.minimum(diff, 0.0)), 0.0)
            keep = first if q == 0 else jnp.logical_not(first)
            y = y + jnp.dot((scores * decay).astype(BF16), jnp.where(keep, xdt_b, jnp.zeros_like(xdt_b)),
                            preferred_element_type=F32)
        y_ref[0, 0, :, cols] = y


def _ssd(xbc, proj, dt_bias4, a4, n_lat):
    b, lt, _ = xbc.shape
    t = SSD_CHUNK
    n_lat_c = n_lat // t
    n_ctx_c = (lt - n_lat) // t
    nc = n_lat_c + n_ctx_c
    width = MB_HPG * MB_HEAD_DIM

    def chunk(d, i):
        fwd = jnp.where(i < n_ctx_c, n_lat_c + i, i - n_ctx_c)
        bwd = jnp.where(i < n_ctx_c, n_lat_c + n_ctx_c - 1 - i, n_lat_c - 1 - (i - n_ctx_c))
        return jnp.where(d == 0, fwd, bwd)

    return pl.pallas_call(
        _ssd_kernel,
        grid=(b, MB_GROUPS, 2, nc),
        in_specs=[pl.BlockSpec((1, t, width), lambda bi, g, d, i: (bi, chunk(d, i), g)),
                  pl.BlockSpec((1, t, MB_STATE), lambda bi, g, d, i: (bi, chunk(d, i), MB_INNER // MB_STATE + g)),
                  pl.BlockSpec((1, t, MB_STATE), lambda bi, g, d, i: (bi, chunk(d, i), MB_INNER // MB_STATE + MB_GROUPS + g)),
                  pl.BlockSpec((1, t, LANE), lambda bi, g, d, i: (bi, chunk(d, i), OFF_DT // LANE + d * MB_GROUPS + g)),
                  pl.BlockSpec((1, 1, LANE), lambda bi, g, d, i: (d * MB_GROUPS + g, 0, 0)),
                  pl.BlockSpec((1, 1, LANE), lambda bi, g, d, i: (d * MB_GROUPS + g, 0, 0))],
        out_specs=pl.BlockSpec((1, 1, t, width), lambda bi, g, d, i: (bi, d, chunk(d, i), g)),
        out_shape=jax.ShapeDtypeStruct((b, 2, lt, MB_INNER), F32),
        scratch_shapes=[pltpu.VMEM((MB_STATE, width), F32)],
        compiler_params=_cp(("parallel", "parallel", "parallel", "arbitrary")),
        name="ssd_scan",
    )(xbc, xbc, xbc, proj, dt_bias4, a4)


def _mamba_finish_kernel(yf_ref, yb_ref, xs_ref, z_ref, d_ref, g_ref, o_ref):
    y = yf_ref[0, 0] + yb_ref[0, 0] + xs_ref[0].astype(F32) * d_ref[...]
    y = y * _silu(z_ref[0].astype(F32))
    o_ref[0] = (y * lax.rsqrt(jnp.mean(y * y, axis=-1, keepdims=True) + EPS) * g_ref[...]).astype(o_ref.dtype)


def _mamba_finish(y2, xbc, proj, d_vec, gain):
    b, lt, _ = xbc.shape
    tm = _tile(lt, 640, 256)
    w = MB_INNER
    return pl.pallas_call(
        _mamba_finish_kernel,
        grid=(b, lt // tm),
        in_specs=[pl.BlockSpec((1, 1, tm, w), lambda bi, i: (bi, 0, i, 0)),
                  pl.BlockSpec((1, 1, tm, w), lambda bi, i: (bi, 1, i, 0)),
                  pl.BlockSpec((1, tm, w), lambda bi, i: (bi, i, 0)),
                  pl.BlockSpec((1, tm, w), lambda bi, i: (bi, i, OFF_MZ // w)),
                  pl.BlockSpec((1, w), lambda bi, i: (0, 0)),
                  pl.BlockSpec((1, w), lambda bi, i: (0, 0))],
        out_specs=pl.BlockSpec((1, tm, w), lambda bi, i: (bi, i, 0)),
        out_shape=jax.ShapeDtypeStruct((b, lt, w), BF16),
        compiler_params=_cp(("parallel", "parallel")),
        name="mamba_finish",
    )(y2, y2, xbc, proj, d_vec, gain.reshape(1, -1))


S5_LANE_GROUPS = LANE // S5_GROUP
S5_BLOCKS = S5_GROUPS // S5_LANE_GROUPS
S5_FLAT = S5_CHUNK * LANE
S5_STATE_W = S5_LANE_GROUPS * 2 * S5_STATE


def _s5_in_kernel(u_ref, w_ref, yi_ref, sf_ref, sfs_ref, sb_ref, sbs_ref, x_sc, *, cb):
    j = pl.program_id(2)

    @pl.when(j == 0)
    def _():
        x_sc[...] = jnp.concatenate([u_ref[pl.ds(s, cb, stride=S5_CHUNK), :] for s in range(S5_CHUNK)],
                                    axis=-1).astype(BF16)

    r = jnp.dot(x_sc[...], w_ref[0, 0], preferred_element_type=F32)

    @pl.when(j == 0)
    def _():
        yi_ref[...] = r.astype(yi_ref.dtype)

    @pl.when(j == 1)
    def _():
        sf_ref[...] = r[:, :S5_STATE_W]
        sfs_ref[...] = r[:, S5_STATE_W:]

    @pl.when(j == 2)
    def _():
        sb_ref[...] = r[:, :S5_STATE_W]
        sbs_ref[...] = r[:, S5_STATE_W:]


def _s5_in(u2, w):
    rows = u2.shape[0]
    m = rows // S5_CHUNK
    cb = _tile(m, 520, 96, 48)
    state_spec = pl.BlockSpec((cb, S5_STATE_W), lambda l, r, j: (r, l))
    state_shape = jax.ShapeDtypeStruct((m, S5_BLOCKS * S5_STATE_W), F32)
    return pl.pallas_call(
        functools.partial(_s5_in_kernel, cb=cb),
        grid=(S5_BLOCKS, m // cb, 3),
        in_specs=[pl.BlockSpec((cb * S5_CHUNK, LANE), lambda l, r, j: (r, l)),
                  pl.BlockSpec((1, 1, S5_FLAT, S5_FLAT), lambda l, r, j: (l, j, 0, 0))],
        out_specs=[pl.BlockSpec((cb, S5_FLAT), lambda l, r, j: (r, l))] + [state_spec] * 4,
        out_shape=[jax.ShapeDtypeStruct((m, S5_BLOCKS * S5_FLAT), BF16)] + [state_shape] * 4,
        scratch_shapes=[pltpu.VMEM((cb, S5_FLAT), BF16)],
        compiler_params=_cp(("parallel", "parallel", "arbitrary")),
        name="s5_intra",
    )(u2, w)


S5_SCAN_LANES = 1024


def _s5_scan_kernel(sf_ref, sfs_ref, sb_ref, sbs_ref, a_ref, hf_ref, hb_ref, st_sc, *, tc):
    @pl.when(pl.program_id(1) == 0)
    def _():
        st_sc[...] = jnp.zeros_like(st_sc)

    for lc in range(sf_ref.shape[1] // S5_SCAN_LANES):
        ln = slice(lc * S5_SCAN_LANES, (lc + 1) * S5_SCAN_LANES)
        a1f, a2f, a1b, a2b = a_ref[0:1, ln], a_ref[1:2, ln], a_ref[2:3, ln], a_ref[3:4, ln]

        def body(c, carry):
            hf, hfs, hb, hbs = carry
            hf_ref[pl.ds(c, 1), ln] = hf
            nf = a1f * hf + a2f * hfs + sf_ref[pl.ds(c, 1), ln]
            nfs = a1f * hfs - a2f * hf + sfs_ref[pl.ds(c, 1), ln]
            cb = tc - 1 - c
            hb_ref[pl.ds(cb, 1), ln] = hb
            nb = a1b * hb + a2b * hbs + sb_ref[pl.ds(cb, 1), ln]
            nbs = a1b * hbs - a2b * hb + sbs_ref[pl.ds(cb, 1), ln]
            return nf, nfs, nb, nbs

        out = lax.fori_loop(0, tc, body, tuple(st_sc[n:n + 1, ln] for n in range(4)))
        for n in range(4):
            st_sc[n:n + 1, ln] = out[n]


def _s5_scan(sf, sfs, sb, sbs, a_rows, n_batch, n_lat_chunks):
    m, lanes = sf.shape
    nc = m // n_batch
    tc = 16
    nlt = n_lat_chunks // tc
    nt = nc // tc

    def tile_f(bi, i):
        return bi * nt + jnp.where(i < nt - nlt, nlt + i, i - (nt - nlt))

    def tile_b(bi, i):
        return bi * nt + jnp.where(i < nt - nlt, nt - 1 - i, nlt - 1 - (i - (nt - nlt)))

    fwd = pl.BlockSpec((tc, lanes), lambda bi, i: (tile_f(bi, i), 0))
    bwd = pl.BlockSpec((tc, lanes), lambda bi, i: (tile_b(bi, i), 0))
    return pl.pallas_call(
        functools.partial(_s5_scan_kernel, tc=tc),
        grid=(n_batch, nt),
        in_specs=[fwd, fwd, bwd, bwd, pl.BlockSpec((8, lanes), lambda bi, i: (0, 0))],
        out_specs=[fwd, bwd],
        out_shape=[jax.ShapeDtypeStruct((m, lanes), F32), jax.ShapeDtypeStruct((m, lanes), F32)],
        scratch_shapes=[pltpu.VMEM((8, lanes), F32)],
        compiler_params=_cp(("parallel", "arbitrary")),
        name="s5_state_scan",
    )(sf, sfs, sb, sbs, a_rows)


def _s5_out_kernel(yi_ref, hf_ref, hb_ref, q_ref, o_ref, *, cb):
    h = jnp.concatenate([hf_ref[...], hb_ref[...]], axis=-1).astype(BF16)
    y = yi_ref[...].astype(F32) + jnp.dot(h, q_ref[0], preferred_element_type=F32)
    for t in range(S5_CHUNK):
        o_ref[pl.ds(t, cb, stride=S5_CHUNK), :] = y[:, t * LANE:(t + 1) * LANE]


def _s5_out(yi, hf, hb, q):
    m = yi.shape[0]
    cb = _tile(m, 520, 96, 48)
    return pl.pallas_call(
        functools.partial(_s5_out_kernel, cb=cb),
        grid=(S5_BLOCKS, m // cb),
        in_specs=[pl.BlockSpec((cb, S5_FLAT), lambda l, r: (r, l)),
                  pl.BlockSpec((cb, S5_STATE_W), lambda l, r: (r, l)),
                  pl.BlockSpec((cb, S5_STATE_W), lambda l, r: (r, l)),
                  pl.BlockSpec((1, 2 * S5_STATE_W, S5_FLAT), lambda l, r: (l, 0, 0))],
        out_specs=pl.BlockSpec((cb * S5_CHUNK, LANE), lambda l, r: (r, l)),
        out_shape=jax.ShapeDtypeStruct((m * S5_CHUNK, S5_WIDTH), F32),
        compiler_params=_cp(("parallel", "parallel")),
        name="s5_readout",
    )(yi, hf, hb, q)


def _gelu_tanh(x):
    return 0.5 * x * (1.0 + jnp.tanh(math.sqrt(2.0 / math.pi) * (x + 0.044715 * (x * x * x))))


def _s5_finish_kernel(y_ref, u_ref, d_ref, w_ref, o_ref):
    y = y_ref[0].astype(F32) + d_ref[...] * u_ref[0].astype(F32)
    g = _gelu_tanh(y)
    o_ref[0] = (g * _sigmoid(jnp.dot(g.astype(BF16), w_ref[...], preferred_element_type=F32))).astype(o_ref.dtype)


def _s5_finish(y, proj, d_vec, w_glu):
    b, lt, w = y.shape
    tm = _tile(lt, 640, 256)
    return pl.pallas_call(
        _s5_finish_kernel,
        grid=(b, lt // tm),
        in_specs=[pl.BlockSpec((1, tm, w), lambda bi, i: (bi, i, 0)),
                  pl.BlockSpec((1, tm, w), lambda bi, i: (bi, i, OFF_S5 // w)),
                  pl.BlockSpec((1, w), lambda bi, i: (0, 0)),
                  pl.BlockSpec((w, w), lambda bi, i: (0, 0))],
        out_specs=pl.BlockSpec((1, tm, w), lambda bi, i: (bi, i, 0)),
        out_shape=jax.ShapeDtypeStruct((b, lt, w), BF16),
        compiler_params=_cp(("parallel", "parallel")),
        name="s5_finish",
    )(y, proj, d_vec.reshape(1, -1), w_glu)


def _lift_cols(rows2d, rows_per_group, inner):
    r, k = rows2d.shape
    j = np.arange(k * S5_LANE_GROUPS)
    src = (j // (S5_LANE_GROUPS * inner)) * inner + j % inner
    sel = jnp.asarray(np.arange(k)[:, None] == src[None, :], dtype=BF16)
    tiled = jnp.dot(rows2d.astype(BF16), sel, preferred_element_type=F32)
    col_group = jnp.asarray((j // inner) % S5_LANE_GROUPS, dtype=jnp.int32)
    row_group = (jnp.arange(r, dtype=jnp.int32) // rows_per_group) % S5_LANE_GROUPS
    return jnp.where(row_group[:, None] == col_group[None, :], tiled, 0.0).astype(BF16)


def _s5_operators(p):
    t = S5_CHUNK
    lam = lax.complex(p["s5_lam_re"].astype(F32), p["s5_lam_im"].astype(F32))
    step = jnp.exp(p["s5_log_step"].astype(F32))[..., None]
    lam_bar = jnp.exp(lam * step)
    b_bar = ((lam_bar - 1.0) / lam)[..., None] * lax.complex(p["s5_b_re"].astype(F32), p["s5_b_im"].astype(F32))
    c_mat = lax.complex(p["s5_c_re"].astype(F32), p["s5_c_im"].astype(F32))
    ks = jnp.arange(t + 1, dtype=F32)
    pw = jnp.exp((lam * step)[:, :, None, :] * ks[None, None, :, None])
    kern = jnp.real(jnp.einsum("dgcn,dgkn,dgni->dgkci", c_mat, pw[:, :, :t], b_bar))
    s_idx = jnp.arange(t)[:, None]
    t_idx = jnp.arange(t)[None, :]
    lag_f = jnp.clip(t_idx - s_idx, 0, t - 1)
    lag_b = jnp.clip(s_idx - t_idx, 0, t - 1)
    kf = jnp.where((t_idx >= s_idx)[None, :, :, None, None], kern[0][:, lag_f], 0.0)
    kb = jnp.where((s_idx >= t_idx)[None, :, :, None, None], kern[1][:, lag_b], 0.0)
    w_intra = jnp.transpose(kf + kb, (0, 1, 4, 2, 3)).reshape(S5_GROUPS, t * S5_GROUP, t * S5_GROUP)
    inj_f = pw[0][:, t - 1 - jnp.arange(t), :, None] * b_bar[0][:, None]
    inj_b = pw[1][:, jnp.arange(t), :, None] * b_bar[1][:, None]

    def inj_mat(z):
        z = jnp.transpose(z, (0, 1, 3, 2)).reshape(S5_GROUPS, t * S5_GROUP, S5_STATE)
        return jnp.concatenate([jnp.real(z), jnp.imag(z), jnp.imag(z), jnp.real(z)], axis=-1)

    lg = S5_LANE_GROUPS

    def lift_in(mat, inner):
        rows = jnp.transpose(mat.reshape(S5_BLOCKS, lg, t, S5_GROUP, mat.shape[-1]), (0, 2, 1, 3, 4))
        return _lift_cols(rows.reshape(S5_BLOCKS * S5_FLAT, mat.shape[-1]), S5_GROUP, inner).reshape(
            S5_BLOCKS, S5_FLAT, S5_FLAT)

    w_in = jnp.stack([lift_in(w_intra, S5_GROUP), lift_in(inj_mat(inj_f), 2 * S5_STATE),
                      lift_in(inj_mat(inj_b), 2 * S5_STATE)], axis=1)
    m_f = c_mat[0][:, None, :, :] * pw[0][:, 1 + jnp.arange(t), None, :]
    m_b = c_mat[1][:, None, :, :] * pw[1][:, t - jnp.arange(t), None, :]

    def read_mat(z):
        z = jnp.transpose(z.reshape(S5_GROUPS, t * S5_GROUP, S5_STATE), (0, 2, 1))
        return jnp.concatenate([jnp.real(z), -jnp.imag(z)], axis=1).astype(BF16)

    def trans(z):
        re, im = jnp.real(z), jnp.imag(z)
        return [jnp.concatenate([re, re], -1).reshape(1, -1), jnp.concatenate([-im, im], -1).reshape(1, -1)]

    a_rows = jnp.concatenate(trans(pw[0][:, t]) + trans(pw[1][:, t])
                             + [jnp.zeros((4, 2 * S5_STATE * S5_GROUPS), F32)], axis=0)
    def lift_read(z):
        return _lift_cols(z.reshape(S5_GROUPS * 2 * S5_STATE, z.shape[-1]), 2 * S5_STATE, S5_GROUP).reshape(
            S5_BLOCKS, S5_STATE_W, S5_FLAT)

    q = jnp.concatenate([lift_read(read_mat(m_f)), lift_read(read_mat(m_b))], axis=1)
    return dict(w_in=w_in, q=q, a_rows=a_rows)


def _s5_mixer(u32, proj, ops, d_vec, w_glu, n_lat):
    b, lt, _ = proj.shape
    yi, sf, sfs, sb, sbs = _s5_in(u32.reshape(b * lt, S5_WIDTH), ops["w_in"])
    hf, hb = _s5_scan(sf, sfs, sb, sbs, ops["a_rows"], b, n_lat // S5_CHUNK)
    y = _s5_out(yi, hf, hb, ops["q"]).reshape(b, lt, S5_WIDTH)
    return _s5_finish(y, proj, d_vec, w_glu)


def _merge_kernel(ya_ref, yb_ref, yc_ref, yd_ref, g0_ref, g1_ref, g2_ref, g3_ref,
                  wa_ref, wb_ref, wc_ref, wd_ref, o_ref):
    acc = None
    for y_ref, g_ref, w_ref in ((ya_ref, g0_ref, wa_ref), (yb_ref, g1_ref, wb_ref),
                                (yc_ref, g2_ref, wc_ref), (yd_ref, g3_ref, wd_ref)):
        term = _sigmoid(g_ref[0].astype(F32)) * jnp.dot(y_ref[0], w_ref[...], preferred_element_type=F32)
        acc = term if acc is None else acc + term
    o_ref[0] = acc.astype(o_ref.dtype)


def _merge(ys, proj, ws):
    b, lt, _ = proj.shape
    d = D_MODEL
    tm = _tile(lt, 1280, 256)
    tn = 512
    nb = d // tn
    in_specs = [pl.BlockSpec((1, tm, y.shape[2]), lambda bi, i, j: (bi, i, 0)) for y in ys]
    in_specs += [pl.BlockSpec((1, tm, tn), functools.partial(lambda bi, i, j, br: (bi, i, br * nb + j), br=br))
                 for br in range(4)]
    in_specs += [pl.BlockSpec((w.shape[0], tn), lambda bi, i, j: (0, j)) for w in ws]
    return pl.pallas_call(
        _merge_kernel,
        grid=(b, lt // tm, nb),
        in_specs=in_specs,
        out_specs=pl.BlockSpec((1, tm, tn), lambda bi, i, j: (bi, i, j)),
        out_shape=jax.ShapeDtypeStruct((b, lt, d), BF16),
        compiler_params=_cp(("parallel", "parallel", "parallel")),
        name="branch_merge",
    )(*ys, proj, proj, proj, proj, *ws)


def _row_gate(ml, mc, row_tile, tm, n_lat, row):
    rows = row_tile * tm + lax.broadcasted_iota(jnp.int32, (tm, 1), 0)
    return jnp.where(rows >= n_lat, mc[row:row + 1], ml[row:row + 1])


def _outproj_kernel(m_ref, w_ref, x_ref, ml_ref, mc_ref, o_ref, *, tm, n_lat):
    gate = _row_gate(ml_ref[0], mc_ref[0], pl.program_id(1), tm, n_lat, 2)
    o_ref[0] = x_ref[0] + gate * jnp.dot(m_ref[0], w_ref[...], preferred_element_type=F32)


def _outproj(m, w_out, x, mods, n_lat):
    b, lt, d = x.shape
    tm = _tile(lt, 1280, 256)
    tn = 512
    return pl.pallas_call(
        functools.partial(_outproj_kernel, tm=tm, n_lat=n_lat),
        grid=(b, lt // tm, d // tn),
        in_specs=[pl.BlockSpec((1, tm, d), lambda bi, i, j: (bi, i, 0)),
                  pl.BlockSpec((d, tn), lambda bi, i, j: (0, j)),
                  pl.BlockSpec((1, tm, tn), lambda bi, i, j: (bi, i, j)),
                  pl.BlockSpec((1, 8, tn), lambda bi, i, j: (bi, 0, j)),
                  pl.BlockSpec((1, 8, tn), lambda bi, i, j: (b, 0, j))],
        out_specs=pl.BlockSpec((1, tm, tn), lambda bi, i, j: (bi, i, j)),
        out_shape=jax.ShapeDtypeStruct((b, lt, d), F32),
        compiler_params=_cp(("parallel", "parallel", "parallel")),
        name="out_proj",
    )(m, w_out, x, mods, mods)


def _prenorm_router_kernel(x_ref, g_ref, ml_ref, mc_ref, wr_ref, br_ref, h_ref, lg_ref, *, tm, n_lat):
    def chunk(r, carry):
        rs = pl.multiple_of(r * ROW_CHUNK, ROW_CHUNK)
        h = _norm_mod(x_ref[0, pl.ds(rs, ROW_CHUNK), :], g_ref[...], ml_ref[0], mc_ref[0],
                      pl.program_id(1) * tm + rs, n_lat, 3).astype(BF16)
        h_ref[0, pl.ds(rs, ROW_CHUNK), :] = h
        lg_ref[0, pl.ds(rs, ROW_CHUNK), :] = jnp.dot(h, wr_ref[...], preferred_element_type=F32) + br_ref[...]
        return carry

    lax.fori_loop(0, tm // ROW_CHUNK, chunk, 0)


def _prenorm_router(x, gain, mods, w_r, b_r, n_lat):
    b, lt, d = x.shape
    tm = _tile(lt, 640, 256)
    return pl.pallas_call(
        functools.partial(_prenorm_router_kernel, tm=tm, n_lat=n_lat),
        grid=(b, lt // tm),
        in_specs=[pl.BlockSpec((1, tm, d), lambda bi, i: (bi, i, 0)),
                  pl.BlockSpec((1, d), lambda bi, i: (0, 0)),
                  pl.BlockSpec((1, 8, d), lambda bi, i: (bi, 0, 0)),
                  pl.BlockSpec((1, 8, d), lambda bi, i: (b, 0, 0)),
                  pl.BlockSpec((d, LANE), lambda bi, i: (0, 0)),
                  pl.BlockSpec((1, LANE), lambda bi, i: (0, 0))],
        out_specs=[pl.BlockSpec((1, tm, d), lambda bi, i: (bi, i, 0)),
                   pl.BlockSpec((1, tm, LANE), lambda bi, i: (bi, i, 0))],
        out_shape=[jax.ShapeDtypeStruct((b, lt, d), BF16),
                   jax.ShapeDtypeStruct((b, lt, LANE), F32)],
        compiler_params=_cp(("parallel", "parallel")),
        name="moe_prenorm_router",
    )(x, gain.reshape(1, d), mods, mods, w_r, b_r)


def _moe_kernel(tile_ref, exp_ref, lo_ref, hi_ref, first_ref, x_ref, rw_ref, w1_ref, w3_ref, w2_ref, o_ref):
    k = pl.program_id(0)
    lo, hi = lo_ref[k], hi_ref[k]
    tile = x_ref.shape[0]

    @pl.when(hi > lo)
    def _():
        x = x_ref[...]
        w1, w3, w2 = w1_ref[0].astype(BF16), w3_ref[0].astype(BF16), w2_ref[0].astype(BF16)
        hid = _silu(jnp.dot(x, w1, preferred_element_type=F32)) * jnp.dot(x, w3, preferred_element_type=F32)
        y = (jnp.dot(hid.astype(BF16), w2, preferred_element_type=F32) * rw_ref[...]).astype(o_ref.dtype)
        rows = tile_ref[k] * tile + lax.broadcasted_iota(jnp.int32, (tile, 1), 0)
        mine = jnp.logical_and(rows >= lo, rows < hi)

        @pl.when(first_ref[k] == 1)
        def _():
            o_ref[...] = jnp.where(mine, y, jnp.zeros_like(y))

        @pl.when(first_ref[k] == 0)
        def _():
            o_ref[...] = jnp.where(mine, y, o_ref[...])


def _moe_ffn(xs, rw, items, w1, w3, w2):
    rows, d = xs.shape
    tile = MOE_TILE
    ff = w1.shape[2]
    item_tile, item_expert = items[0], items[1]
    grid_spec = pltpu.PrefetchScalarGridSpec(
        num_scalar_prefetch=5,
        grid=(item_tile.shape[0],),
        in_specs=[pl.BlockSpec((tile, d), lambda k, it, ie, lo, hi, fi: (it[k], 0)),
                  pl.BlockSpec((tile, 1), lambda k, it, ie, lo, hi, fi: (it[k], 0)),
                  pl.BlockSpec((1, d, ff), lambda k, it, ie, lo, hi, fi: (ie[k], 0, 0)),
                  pl.BlockSpec((1, d, ff), lambda k, it, ie, lo, hi, fi: (ie[k], 0, 0)),
                  pl.BlockSpec((1, ff, d), lambda k, it, ie, lo, hi, fi: (ie[k], 0, 0))],
        out_specs=pl.BlockSpec((tile, d), lambda k, it, ie, lo, hi, fi: (it[k], 0)),
    )
    return pl.pallas_call(
        _moe_kernel,
        grid_spec=grid_spec,
        out_shape=jax.ShapeDtypeStruct((rows, d), BF16),
        compiler_params=_cp(("arbitrary",)),
        name="moe_experts",
    )(*items, xs, rw, w1, w3, w2)


def _moe(h2, logits, w1, w3, w2):
    n, d = h2.shape
    tile = MOE_TILE
    g_prob = jax.nn.softmax(logits[:, :MOE_GROUPS], axis=-1)
    g_idx = jnp.argmax(g_prob, axis=-1)
    g_w = jnp.take_along_axis(g_prob, g_idx[:, None], axis=1)
    e_logits = logits[:, MOE_GROUPS:MOE_GROUPS + N_EXPERTS].reshape(n, MOE_GROUPS, MOE_PER_GROUP)
    e_logits = jnp.take_along_axis(e_logits, g_idx[:, None, None], axis=1)[:, 0]
    e_w, e_idx = lax.top_k(jax.nn.softmax(e_logits, axis=-1), 2)
    w = g_w * e_w / jnp.sum(e_w, axis=-1, keepdims=True)
    expert = (g_idx[:, None] * MOE_PER_GROUP + e_idx).astype(jnp.int32)

    n_rows = 2 * n
    assert n_rows % tile == 0
    n_tiles = n_rows // tile
    iota = jnp.arange(n_rows, dtype=jnp.int32)
    sorted_e, order, sorted_w = lax.sort((expert.reshape(-1), iota, w.reshape(-1)), num_keys=1, is_stable=True)
    _, sorted_pos = lax.sort((order, iota), num_keys=1)
    e_ids = jnp.arange(N_EXPERTS, dtype=jnp.int32)
    cnt_end = jnp.sum((sorted_e[None, :] <= e_ids[:, None]).astype(jnp.int32), axis=1)
    cnt_start = jnp.concatenate([jnp.zeros((1,), jnp.int32), cnt_end[:-1]])
    first_tile = cnt_start // tile
    last_tile = jnp.where(cnt_end > cnt_start, (cnt_end - 1) // tile, first_tile - 1)
    item_end = jnp.cumsum(last_tile - first_tile + 1)
    item_start = jnp.concatenate([jnp.zeros((1,), jnp.int32), item_end[:-1]])
    n_items = item_end[-1]
    k = jnp.arange(n_tiles + N_EXPERTS, dtype=jnp.int32)
    item_e = jnp.minimum(jnp.sum((item_end[None, :] <= k[:, None]).astype(jnp.int32), axis=1), N_EXPERTS - 1)
    item_t = first_tile[item_e] + k - item_start[item_e]
    valid = k < n_items
    item_e = jnp.where(valid, item_e, item_e[n_items - 1])
    item_t = jnp.where(valid, item_t, item_t[n_items - 1])
    item_lo = jnp.where(valid, jnp.maximum(cnt_start[item_e], item_t * tile), 0)
    item_hi = jnp.where(valid, jnp.minimum(cnt_end[item_e], (item_t + 1) * tile), 0)
    item_first = jnp.concatenate([jnp.ones((1,), jnp.int32), (item_t[1:] != item_t[:-1]).astype(jnp.int32)])
    items = tuple(a.astype(jnp.int32) for a in (item_t, item_e, item_lo, item_hi, item_first))

    xs = jnp.take(h2, order // 2, axis=0)
    ys = _moe_ffn(xs, sorted_w[:, None], items, w1, w3, w2)
    dest = sorted_pos.reshape(n, 2)
    return jnp.take(ys, dest[:, 0], axis=0).astype(F32) + jnp.take(ys, dest[:, 1], axis=0).astype(F32)


def _pack_w_in(w_in):
    depth, d, _ = w_in.shape
    cuts = {}
    off = 0
    for name, width in (("s5", S5_WIDTH), ("gq", GQA_OUT), ("gk", GQA_KV_HEADS * HEAD_DIM), ("gv", GQA_KV_HEADS * HEAD_DIM),
                        ("mq", MLA_Q_RANK), ("mkv", MLA_KV_RANK), ("mkr", MLA_ROPE), ("mz", MB_INNER), ("xbc", MB_XBC),
                        ("dt", 2 * MB_HEADS), ("gate", 4 * D_MODEL)):
        cuts[name] = w_in[:, :, off:off + width]
        off += width
    zeros = lambda n: jnp.zeros((depth, d, n), w_in.dtype)
    dt_blocks = []
    for blk in range(2 * MB_GROUPS):
        dt_blocks += [cuts["dt"][:, :, blk * MB_HPG:(blk + 1) * MB_HPG], zeros(LANE - MB_HPG)]
    packed = jnp.concatenate(
        [cuts["gate"], cuts["gq"], cuts["mz"], cuts["xbc"], cuts["mq"], cuts["s5"], cuts["gk"], cuts["gv"],
         cuts["mkv"], cuts["mkr"], zeros(LANE - MLA_ROPE)] + dt_blocks + [zeros(NW - OFF_DT - 4 * LANE)], axis=-1)
    return packed.astype(BF16)


def _rope_table(n_lat, n_ctx, dim, width):
    rows = n_lat // GRID_W
    row = jnp.repeat(jnp.arange(rows, dtype=F32), GRID_W)
    col = jnp.tile(jnp.arange(GRID_W, dtype=F32), rows)
    quarter = dim // 4
    inv_freq = ROPE_THETA ** (-jnp.arange(quarter, dtype=F32) / quarter)
    ang_r = row[:, None] * inv_freq
    ang_c = col[:, None] * inv_freq
    ang = jnp.concatenate([ang_r, ang_r, ang_c, ang_c], axis=-1)
    cos = jnp.concatenate([jnp.cos(ang), jnp.ones((n_lat, width - dim), F32)], axis=-1)
    sin = jnp.concatenate([jnp.sin(ang), jnp.zeros((n_lat, width - dim), F32)], axis=-1)
    cos = jnp.concatenate([cos, jnp.ones((n_ctx, width), F32)], axis=0)
    sin = jnp.concatenate([sin, jnp.zeros((n_ctx, width), F32)], axis=0)
    return cos, sin


def _rot_matrix(dim, width):
    quarter = dim // 4
    idx = jnp.arange(width)
    in_first = (idx % (2 * quarter)) < quarter
    valid = idx < dim
    src_for_first = idx + quarter
    src_for_second = idx - quarter
    rows = jnp.arange(width)[:, None]
    r = jnp.where(in_first[None, :] & (rows == src_for_first[None, :]), -1.0, 0.0)
    r = r + jnp.where((~in_first)[None, :] & (rows == src_for_second[None, :]), 1.0, 0.0)
    return (r * valid[None, :]).astype(BF16)


def _pad_lanes(v, width):
    return jnp.concatenate([v.astype(F32), jnp.zeros((width - v.shape[0],), F32)]).reshape(1, width)


def _lane_row(lane, value):
    return jnp.zeros((1, LANE), F32).at[0, lane].set(value)


def _score_bound(q_sq_norm, k_sq_norm, scale):
    return (jnp.sqrt(q_sq_norm * k_sq_norm) * (scale * LOG2E * 1.02) + 0.5).astype(F32)


def kernel(x, c, ctx, c_ctx, norm1, norm2, w_ada, b_ada, w_in, s5_lam_re, s5_lam_im, s5_log_step, s5_b_re, s5_b_im, s5_c_re, s5_c_im, s5_d, s5_w_glu, gqa_q_norm, gqa_k_norm, mla_q_norm, mla_kv_norm, mla_w_uq, mla_w_uk, mla_w_uv, mla_qn_norm, mla_kn_norm, mla_qr_norm, mla_kr_norm, mb_conv_w, mb_conv_b, mb_dt_bias, mb_a_log, mb_d, mb_norm, w_br_s5, w_br_gqa, w_br_mla, w_br_mb, w_out, moe_w_group, moe_b_group, moe_w_expert, moe_b_expert, moe_w1, moe_w3, moe_w2):
    b, n_lat, d = x.shape
    n_ctx = ctx.shape[1]
    lt = n_lat + n_ctx
    depth = w_in.shape[0]

    cvec = jnp.zeros((8, d), F32).at[:b].set(c.astype(F32)).at[b].set(c_ctx.astype(F32))
    mod_all = _ada(cvec, w_ada, b_ada).reshape(depth, 8, N_MOD, d)[:, :b + 1]
    mod_all = jnp.concatenate([mod_all, jnp.zeros((depth, b + 1, 8 - N_MOD, d), F32)], axis=2)

    w_in_p = _pack_w_in(w_in)
    cos_g, sin_g = _rope_table(n_lat, n_ctx, HEAD_DIM, HEAD_DIM)
    cos_m, sin_m = _rope_table(n_lat, n_ctx, MLA_ROPE, LANE)
    rot_g = _rot_matrix(HEAD_DIM, HEAD_DIM)
    rot_m = _rot_matrix(MLA_ROPE, LANE)
    ka_g = jnp.broadcast_to(_lane_row(0, 1.0).astype(BF16), (b, lt, LANE))

    uq = mla_w_uq.reshape(depth, MLA_Q_RANK, MLA_HEADS, MLA_NOPE + MLA_ROPE)
    wq_nope = uq[..., :MLA_NOPE].reshape(depth, MLA_Q_RANK, MLA_HEADS * MLA_NOPE).astype(BF16)
    wq_rope = jnp.concatenate([uq[..., MLA_NOPE:], jnp.zeros(uq.shape[:3] + (LANE - MLA_ROPE,), uq.dtype)],
                              axis=-1).reshape(depth, MLA_Q_RANK, MLA_HEADS * LANE).astype(BF16)
    a_dec = -jnp.exp(mb_a_log.astype(F32))

    xs = jnp.concatenate([x.astype(F32), ctx.astype(F32)], axis=1)

    for i in range(depth):
        mods = mod_all[i]
        proj, u32 = _inproj(xs, norm1[i], mods, w_in_p[i], n_lat)

        s5p = dict(s5_lam_re=s5_lam_re[i], s5_lam_im=s5_lam_im[i], s5_log_step=s5_log_step[i], s5_b_re=s5_b_re[i],
                   s5_b_im=s5_b_im[i], s5_c_re=s5_c_re[i], s5_c_im=s5_c_im[i])
        ya = _s5_mixer(u32, proj, _s5_operators(s5p), s5_d[i].astype(F32), s5_w_glu[i].astype(BF16), n_lat)

        shift_g = _score_bound(HEAD_DIM * jnp.max(jnp.abs(gqa_q_norm[i])) ** 2,
                               HEAD_DIM * jnp.max(jnp.abs(gqa_k_norm[i])) ** 2, HEAD_DIM ** -0.5)
        qg, kg = _gqa_prep(proj, cos_g, sin_g, rot_g, gqa_q_norm[i].astype(F32), gqa_k_norm[i].astype(F32),
                           _lane_row(0, -shift_g))
        yb = _attention(qg, kg, ka_g, proj, shift_g, n_kv_heads=GQA_KV_HEADS, grp=GQA_HEADS // GQA_KV_HEADS,
                        n_lat=n_lat, v_col0=OFF_GV // HEAD_DIM, tq_lat=1024, tk_lat=3328)

        mp = dict(mla_q_norm=mla_q_norm[i].astype(F32), mla_kv_norm=mla_kv_norm[i].astype(F32),
                  wq_nope=wq_nope[i], wq_rope=wq_rope[i], w_uk=mla_w_uk[i].astype(BF16), w_uv=mla_w_uv[i].astype(BF16),
                  mla_qn_norm=mla_qn_norm[i].astype(F32), mla_kn_norm=mla_kn_norm[i].astype(F32),
                  qr_gain=_pad_lanes(mla_qr_norm[i], LANE), kr_gain=_pad_lanes(mla_kr_norm[i], LANE))
        shift_m = _score_bound(
            MLA_NOPE * jnp.max(jnp.abs(mla_qn_norm[i])) ** 2 + MLA_ROPE * jnp.max(jnp.abs(mla_qr_norm[i])) ** 2,
            MLA_NOPE * jnp.max(jnp.abs(mla_kn_norm[i])) ** 2 + MLA_ROPE * jnp.max(jnp.abs(mla_kr_norm[i])) ** 2,
            (MLA_NOPE + MLA_ROPE) ** -0.5)
        mp["q_shift_row"] = _lane_row(MLA_ROPE, -shift_m)
        mp["k_one_row"] = _lane_row(MLA_ROPE, 1.0)
        qm, kn, kr, vm = _mla_prep(proj, cos_m, sin_m, rot_m, mp)
        yc = _attention(qm, kn, kr, vm, shift_m, n_kv_heads=MLA_HEADS, grp=1,
                        n_lat=n_lat, v_col0=0, tq_lat=4096, tk_lat=3328)

        xbc = _mamba_conv(proj, mb_conv_w[i], mb_conv_b[i], n_lat)
        dt_bias4 = jnp.concatenate([mb_dt_bias[i].astype(F32).reshape(2 * MB_GROUPS, MB_HPG),
                                    jnp.zeros((2 * MB_GROUPS, LANE - MB_HPG), F32)], axis=-1).reshape(2 * MB_GROUPS, 1, LANE)
        a4 = jnp.concatenate([a_dec[i].reshape(2 * MB_GROUPS, MB_HPG),
                              jnp.zeros((2 * MB_GROUPS, LANE - MB_HPG), F32)], axis=-1).reshape(2 * MB_GROUPS, 1, LANE)
        y2 = _ssd(xbc, proj, dt_bias4, a4, n_lat)
        d_vec = jnp.repeat(mb_d[i].astype(F32), MB_HEAD_DIM).reshape(1, MB_INNER)
        yd = _mamba_finish(y2, xbc, proj, d_vec, mb_norm[i].astype(F32))

        merged = _merge((ya, yb, yc, yd), proj,
                        (w_br_s5[i].astype(BF16), w_br_gqa[i].astype(BF16), w_br_mla[i].astype(BF16), w_br_mb[i].astype(BF16)))
        xs = _outproj(merged, w_out[i].astype(BF16), xs, mods, n_lat)

        w_r = jnp.concatenate([moe_w_group[i], moe_w_expert[i],
                               jnp.zeros((d, LANE - MOE_GROUPS - N_EXPERTS), F32)], axis=-1).astype(BF16)
        b_r = _pad_lanes(jnp.concatenate([moe_b_group[i], moe_b_expert[i]]), LANE)
        h2, logits = _prenorm_router(xs, norm2[i], mods, w_r, b_r, n_lat)
        ff = _moe(h2.reshape(b * lt, d), logits.reshape(b * lt, LANE),
                  moe_w1[i], moe_w3[i], moe_w2[i]).reshape(b, lt, d)
        if i == depth - 1:
            return (xs[:, :n_lat] + mods[:b, 5][:, None, :] * ff[:, :n_lat]).astype(x.dtype)
        gate2 = jnp.concatenate([jnp.broadcast_to(mods[:b, 5][:, None, :], (b, n_lat, d)),
                                 jnp.broadcast_to(mods[b, 5][None, None, :], (b, n_ctx, d))], axis=1)
        xs = xs + gate2 * ff
```

```python
import functools
import math

import numpy as np
import jax
import jax.numpy as jnp
from jax import lax
from jax.experimental import pallas as pl
from jax.experimental.pallas import tpu as pltpu

F32 = jnp.float32
BF16 = jnp.bfloat16
HIGHEST = lax.Precision.HIGHEST

EPS = 1e-6
ROPE_THETA = 10000.0
GRID_W = 64
D_MODEL = 2048
N_MOD = 6

S5_GROUP = 16
S5_WIDTH = 768
S5_GROUPS = S5_WIDTH // S5_GROUP
S5_STATE = 64
S5_CHUNK = 16

GQA_HEADS = 8
GQA_KV_HEADS = 2
HEAD_DIM = 128
GQA_OUT = GQA_HEADS * HEAD_DIM

MLA_HEADS = 8
MLA_Q_RANK = 512
MLA_KV_RANK = 256
MLA_NOPE = 128
MLA_ROPE = 64
MLA_OUT = MLA_HEADS * HEAD_DIM

MB_HEADS = 16
MB_HEAD_DIM = 64
MB_INNER = MB_HEADS * MB_HEAD_DIM
MB_GROUPS = 2
MB_HPG = MB_HEADS // MB_GROUPS
MB_STATE = 128
MB_CONV = 5
MB_XBC = MB_INNER + 2 * MB_GROUPS * MB_STATE
SSD_CHUNK = 128

MOE_GROUPS = 4
MOE_PER_GROUP = 8
N_EXPERTS = MOE_GROUPS * MOE_PER_GROUP
MOE_FF = 512
MOE_TILE = 512

LANE = 128
VMEM_LIMIT = 56 * 1024 * 1024
LOG2E = math.log2(math.e)
FLASH_ROW_BLOCK = 256
FLASH_DQ = 256
FLASH_MAX_SHIFT = 60.0

OFF_GATE = 0
OFF_GQ = 8192
OFF_MZ = 9216
OFF_XBC = 10240
OFF_MQ = 11776
OFF_S5 = 12288
OFF_GK = 13056
OFF_GV = 13312
OFF_MKV = 13568
OFF_MKR = 13824
OFF_DT = 13952
NW = 14592


def _tile(n, *cands):
    for c in cands:
        if n % c == 0:
            return c
    return n


def _cp(sem, vmem=VMEM_LIMIT):
    return pltpu.CompilerParams(dimension_semantics=sem, vmem_limit_bytes=vmem)


def _sigmoid(x):
    return 1.0 / (1.0 + jnp.exp(-x))


def _silu(x):
    return x * _sigmoid(x)


def _ada_kernel(c_ref, w_ref, b_ref, o_ref):
    s = _silu(c_ref[...])
    o_ref[0] = jnp.dot(s.astype(BF16), w_ref[0].astype(BF16), preferred_element_type=F32) + b_ref[0]


def _ada(cvec, w_ada, b_ada):
    depth, d, n = w_ada.shape
    tn = _tile(n, 1024, 512)
    return pl.pallas_call(
        _ada_kernel,
        grid=(depth, n // tn),
        in_specs=[pl.BlockSpec((8, d), lambda l, j: (0, 0)),
                  pl.BlockSpec((1, d, tn), lambda l, j: (l, 0, j)),
                  pl.BlockSpec((1, 1, tn), lambda l, j: (l, 0, j))],
        out_specs=pl.BlockSpec((1, 8, tn), lambda l, j: (l, 0, j)),
        out_shape=jax.ShapeDtypeStruct((depth, 8, n), F32),
        compiler_params=_cp(("parallel", "parallel")),
        name="ada_mod",
    )(cvec, w_ada, b_ada.reshape(depth, 1, n))


ROW_CHUNK = 128


def _norm_mod(x, g, ml, mc, row_start, n_lat, row0):
    r = lax.rsqrt(jnp.mean(x * x, axis=-1, keepdims=True) + EPS)
    rows = row_start + lax.broadcasted_iota(jnp.int32, (x.shape[0], 1), 0)
    is_ctx = rows >= n_lat
    shift = jnp.where(is_ctx, mc[row0:row0 + 1], ml[row0:row0 + 1])
    scale = jnp.where(is_ctx, mc[row0 + 1:row0 + 2], ml[row0 + 1:row0 + 2])
    return x * r * g * (1.0 + scale) + shift


def _inproj_kernel(x_ref, g_ref, ml_ref, mc_ref, w_ref, o_ref, u_ref, h_sc, *, n_lat, tm, s5_tile):
    @pl.when(pl.program_id(2) == 0)
    def _():
        def chunk(r, carry):
            rs = pl.multiple_of(r * ROW_CHUNK, ROW_CHUNK)
            h = _norm_mod(x_ref[0, pl.ds(rs, ROW_CHUNK), :], g_ref[...], ml_ref[0], mc_ref[0],
                          pl.program_id(1) * tm + rs, n_lat, 0)
            h_sc[pl.ds(rs, ROW_CHUNK), :] = h.astype(BF16)
            return carry

        lax.fori_loop(0, tm // ROW_CHUNK, chunk, 0)

    acc = jnp.dot(h_sc[...], w_ref[...], preferred_element_type=F32)
    o_ref[0] = acc.astype(o_ref.dtype)

    @pl.when(pl.program_id(2) == s5_tile)
    def _():
        u_ref[0] = acc


def _inproj(x, gain, mods, w, n_lat):
    b, lt, d = x.shape
    n = w.shape[1]
    tm = _tile(lt, 1280, 256)
    tn = S5_WIDTH
    assert n % tn == 0 and OFF_S5 % tn == 0
    return pl.pallas_call(
        functools.partial(_inproj_kernel, n_lat=n_lat, tm=tm, s5_tile=OFF_S5 // tn),
        grid=(b, lt // tm, n // tn),
        in_specs=[pl.BlockSpec((1, tm, d), lambda bi, i, j: (bi, i, 0)),
                  pl.BlockSpec((1, d), lambda bi, i, j: (0, 0)),
                  pl.BlockSpec((1, 8, d), lambda bi, i, j: (bi, 0, 0)),
                  pl.BlockSpec((1, 8, d), lambda bi, i, j: (b, 0, 0)),
                  pl.BlockSpec((d, tn), lambda bi, i, j: (0, j))],
        out_specs=[pl.BlockSpec((1, tm, tn), lambda bi, i, j: (bi, i, j)),
                   pl.BlockSpec((1, tm, tn), lambda bi, i, j: (bi, i, 0))],
        out_shape=[jax.ShapeDtypeStruct((b, lt, n), BF16),
                   jax.ShapeDtypeStruct((b, lt, tn), F32)],
        scratch_shapes=[pltpu.VMEM((tm, d), BF16)],
        compiler_params=_cp(("parallel", "parallel", "arbitrary")),
        name="in_proj",
    )(x, gain.reshape(1, d), mods, mods, w)


def _head_norm_rope(x, gain, cos, sin, rmat, n_valid, scale):
    ms = jnp.sum(x * x, axis=-1, keepdims=True) * (1.0 / n_valid)
    y = x * lax.rsqrt(ms + EPS) * gain
    if rmat is not None:
        rot = jnp.dot(y.astype(BF16), rmat, preferred_element_type=F32)
        y = y * cos + rot * sin
    return y * scale


def _gqa_prep_kernel(q_ref, k_ref, cos_ref, sin_ref, r_ref, qg_ref, kg_ref, sh_ref, qo_ref, ko_ref):
    cos, sin, rmat = cos_ref[...], sin_ref[...], r_ref[...]
    scale = HEAD_DIM ** -0.5 * LOG2E
    aux = jnp.broadcast_to(sh_ref[...], (q_ref.shape[1], LANE)).astype(BF16)
    for h in range(GQA_HEADS):
        x = q_ref[0, :, h * HEAD_DIM:(h + 1) * HEAD_DIM].astype(F32)
        qo_ref[0, :, h * FLASH_DQ:h * FLASH_DQ + HEAD_DIM] = _head_norm_rope(
            x, qg_ref[...], cos, sin, rmat, HEAD_DIM, scale).astype(BF16)
        qo_ref[0, :, h * FLASH_DQ + HEAD_DIM:(h + 1) * FLASH_DQ] = aux
    for h in range(GQA_KV_HEADS):
        x = k_ref[0, :, h * HEAD_DIM:(h + 1) * HEAD_DIM].astype(F32)
        ko_ref[0, :, h * HEAD_DIM:(h + 1) * HEAD_DIM] = _head_norm_rope(
            x, kg_ref[...], cos, sin, rmat, HEAD_DIM, 1.0).astype(BF16)


def _gqa_prep(proj, cos, sin, rmat, q_gain, k_gain, shift_row):
    b, lt, _ = proj.shape
    tm = _tile(lt, 640, 256)
    kw = GQA_KV_HEADS * HEAD_DIM
    qw = GQA_HEADS * FLASH_DQ
    return pl.pallas_call(
        _gqa_prep_kernel,
        grid=(b, lt // tm),
        in_specs=[pl.BlockSpec((1, tm, GQA_OUT), lambda bi, i: (bi, i, OFF_GQ // GQA_OUT)),
                  pl.BlockSpec((1, tm, kw), lambda bi, i: (bi, i, OFF_GK // kw)),
                  pl.BlockSpec((tm, HEAD_DIM), lambda bi, i: (i, 0)),
                  pl.BlockSpec((tm, HEAD_DIM), lambda bi, i: (i, 0)),
                  pl.BlockSpec((HEAD_DIM, HEAD_DIM), lambda bi, i: (0, 0)),
                  pl.BlockSpec((1, HEAD_DIM), lambda bi, i: (0, 0)),
                  pl.BlockSpec((1, HEAD_DIM), lambda bi, i: (0, 0)),
                  pl.BlockSpec((1, LANE), lambda bi, i: (0, 0))],
        out_specs=[pl.BlockSpec((1, tm, qw), lambda bi, i: (bi, i, 0)),
                   pl.BlockSpec((1, tm, kw), lambda bi, i: (bi, i, 0))],
        out_shape=[jax.ShapeDtypeStruct((b, lt, qw), BF16),
                   jax.ShapeDtypeStruct((b, lt, kw), BF16)],
        compiler_params=_cp(("parallel", "parallel")),
        name="gqa_prep",
    )(proj, proj, cos, sin, rmat, q_gain.reshape(1, -1), k_gain.reshape(1, -1), shift_row)


def _mla_prep_kernel(cq_ref, ckv_ref, kr_ref, cos_ref, sin_ref, r_ref,
                     qn_g_ref, kvn_g_ref, wqn_ref, wqr_ref, wuk_ref, wuv_ref,
                     qnn_g_ref, qrn_g_ref, knn_g_ref, krn_g_ref, qsh_ref, kone_ref,
                     q_ref, kn_ref, kr_out_ref, v_ref):
    cos, sin, rmat = cos_ref[...], sin_ref[...], r_ref[...]
    scale = (MLA_NOPE + MLA_ROPE) ** -0.5 * LOG2E
    cq = cq_ref[0].astype(F32)
    cqn = (cq * lax.rsqrt(jnp.mean(cq * cq, axis=-1, keepdims=True) + EPS) * qn_g_ref[...]).astype(BF16)
    q_nope = jnp.dot(cqn, wqn_ref[...], preferred_element_type=F32)
    q_rope = jnp.dot(cqn, wqr_ref[...], preferred_element_type=F32)
    for h in range(MLA_HEADS):
        sl = slice(h * HEAD_DIM, (h + 1) * HEAD_DIM)
        qn = _head_norm_rope(q_nope[:, sl], qnn_g_ref[...], None, None, None, MLA_NOPE, scale)
        qr = _head_norm_rope(q_rope[:, sl], qrn_g_ref[...], cos, sin, rmat, MLA_ROPE, scale)
        q_ref[0, :, 2 * h * HEAD_DIM:(2 * h + 1) * HEAD_DIM] = qn.astype(BF16)
        q_ref[0, :, (2 * h + 1) * HEAD_DIM:(2 * h + 2) * HEAD_DIM] = (qr + qsh_ref[...]).astype(BF16)
    ckv = ckv_ref[0].astype(F32)
    lat = (ckv * lax.rsqrt(jnp.mean(ckv * ckv, axis=-1, keepdims=True) + EPS) * kvn_g_ref[...]).astype(BF16)
    k_nope = jnp.dot(lat, wuk_ref[...], preferred_element_type=F32)
    v_ref[0] = jnp.dot(lat, wuv_ref[...], preferred_element_type=F32).astype(BF16)
    for h in range(MLA_HEADS):
        sl = slice(h * HEAD_DIM, (h + 1) * HEAD_DIM)
        kn_ref[0, :, sl] = _head_norm_rope(k_nope[:, sl], knn_g_ref[...], None, None, None, MLA_NOPE, 1.0).astype(BF16)
    kr = kr_ref[0].astype(F32)
    kr_out_ref[0] = (_head_norm_rope(kr, krn_g_ref[...], cos, sin, rmat, MLA_ROPE, 1.0) + kone_ref[...]).astype(BF16)


def _mla_prep(proj, cos, sin, rmat, p):
    b, lt, _ = proj.shape
    tm = _tile(lt, 640, 256)
    hd = MLA_HEADS * HEAD_DIM
    full = lambda shape: pl.BlockSpec(shape, lambda bi, i: tuple(0 for _ in shape))
    return pl.pallas_call(
        _mla_prep_kernel,
        grid=(b, lt // tm),
        in_specs=[pl.BlockSpec((1, tm, MLA_Q_RANK), lambda bi, i: (bi, i, OFF_MQ // MLA_Q_RANK)),
                  pl.BlockSpec((1, tm, MLA_KV_RANK), lambda bi, i: (bi, i, OFF_MKV // MLA_KV_RANK)),
                  pl.BlockSpec((1, tm, LANE), lambda bi, i: (bi, i, OFF_MKR // LANE)),
                  pl.BlockSpec((tm, LANE), lambda bi, i: (i, 0)),
                  pl.BlockSpec((tm, LANE), lambda bi, i: (i, 0)),
                  full((LANE, LANE)),
                  full((1, MLA_Q_RANK)), full((1, MLA_KV_RANK)),
                  full((MLA_Q_RANK, hd)), full((MLA_Q_RANK, hd)),
                  full((MLA_KV_RANK, hd)), full((MLA_KV_RANK, hd)),
                  full((1, LANE)), full((1, LANE)), full((1, LANE)), full((1, LANE)),
                  full((1, LANE)), full((1, LANE))],
        out_specs=[pl.BlockSpec((1, tm, 2 * hd), lambda bi, i: (bi, i, 0)),
                   pl.BlockSpec((1, tm, hd), lambda bi, i: (bi, i, 0)),
                   pl.BlockSpec((1, tm, LANE), lambda bi, i: (bi, i, 0)),
                   pl.BlockSpec((1, tm, hd), lambda bi, i: (bi, i, 0))],
        out_shape=[jax.ShapeDtypeStruct((b, lt, 2 * hd), BF16),
                   jax.ShapeDtypeStruct((b, lt, hd), BF16),
                   jax.ShapeDtypeStruct((b, lt, LANE), BF16),
                   jax.ShapeDtypeStruct((b, lt, hd), BF16)],
        compiler_params=_cp(("parallel", "parallel")),
        name="mla_prep",
    )(proj, proj, proj, cos, sin, rmat,
      p["mla_q_norm"].reshape(1, -1), p["mla_kv_norm"].reshape(1, -1),
      p["wq_nope"], p["wq_rope"], p["w_uk"], p["w_uv"],
      p["mla_qn_norm"].reshape(1, -1), p["qr_gain"], p["mla_kn_norm"].reshape(1, -1), p["kr_gain"],
      p["q_shift_row"], p["k_one_row"])


def _stack_queries(q_ref, q_sc, grp, tq):
    for g in range(grp):
        q_sc[g * tq:(g + 1) * tq, :] = q_ref[0, :, g * FLASH_DQ:(g + 1) * FLASH_DQ]


def _write_heads(o_ref, o, grp, tq):
    for g in range(grp):
        o_ref[0, :, g * HEAD_DIM:(g + 1) * HEAD_DIM] = o[g * tq:(g + 1) * tq].astype(o_ref.dtype)


def _flash_online_kernel(q_ref, k_ref, ka_ref, v_ref, o_ref, q_sc, m_sc, l_sc, acc_sc, *, grp, tq):
    kv = pl.program_id(3)

    @pl.when(kv == 0)
    def _():
        _stack_queries(q_ref, q_sc, grp, tq)
        m_sc[...] = jnp.full_like(m_sc, -jnp.inf)
        l_sc[...] = jnp.zeros_like(l_sc)
        acc_sc[...] = jnp.zeros_like(acc_sc)

    k = jnp.concatenate([k_ref[0], ka_ref[0]], axis=-1)
    v = v_ref[0]
    n_chunks = k.shape[0] // LANE
    rb = min(FLASH_ROW_BLOCK, grp * tq)
    for r in range(grp * tq // rb):
        rows = slice(r * rb, (r + 1) * rb)
        s = lax.dot_general(q_sc[rows, :], k, (((1,), (1,)), ((), ())), preferred_element_type=F32)
        m_prev = m_sc[rows, :]
        m_new = jnp.maximum(m_prev, jnp.max(s, axis=-1, keepdims=True))
        alpha = jnp.exp2(m_prev - m_new)
        p = jnp.exp2(s - jnp.concatenate([m_new] * n_chunks, axis=-1))
        p_sum = p[:, :LANE]
        for c in range(1, n_chunks):
            p_sum = p_sum + p[:, c * LANE:(c + 1) * LANE]
        l_sc[rows, :] = alpha * l_sc[rows, :] + p_sum
        acc_sc[rows, :] = alpha * acc_sc[rows, :] + jnp.dot(p.astype(BF16), v, preferred_element_type=F32)
        m_sc[rows, :] = m_new

    @pl.when(kv == pl.num_programs(3) - 1)
    def _():
        _write_heads(o_ref, acc_sc[...] / jnp.sum(l_sc[...], axis=-1, keepdims=True), grp, tq)


def _flash_bounded_kernel(q_ref, k_ref, ka_ref, v_ref, o_ref, q_sc, acc_sc, *, grp, tq):
    kv = pl.program_id(3)

    @pl.when(kv == 0)
    def _():
        _stack_queries(q_ref, q_sc, grp, tq)
        acc_sc[...] = jnp.zeros_like(acc_sc)

    k = jnp.concatenate([k_ref[0], ka_ref[0]], axis=-1)
    v = v_ref[0]
    ones_col = (lax.broadcasted_iota(jnp.int32, v.shape, 1) == 0).astype(BF16)
    v1 = jnp.concatenate([v, ones_col], axis=-1)
    rb = min(FLASH_ROW_BLOCK, grp * tq)
    for r in range(grp * tq // rb):
        rows = slice(r * rb, (r + 1) * rb)
        s = lax.dot_general(q_sc[rows, :], k, (((1,), (1,)), ((), ())), preferred_element_type=F32)
        acc_sc[rows, :] += jnp.dot(jnp.exp2(s).astype(BF16), v1, preferred_element_type=F32)

    @pl.when(kv == pl.num_programs(3) - 1)
    def _():
        acc = acc_sc[...]
        _write_heads(o_ref, acc[:, :HEAD_DIM] / acc[:, HEAD_DIM:HEAD_DIM + 1], grp, tq)


def _flash(q, k, ka, v, *, bounded, n_kv_heads, grp, tq, tk, q_rows, q_off, kv_rows, kv_off, v_col0):
    b = q.shape[0]
    qo, ko = q_off // tq, kv_off // tk
    rows = grp * tq
    if bounded:
        body = _flash_bounded_kernel
        scratch = [pltpu.VMEM((rows, FLASH_DQ), BF16), pltpu.VMEM((rows, 2 * HEAD_DIM), F32)]
    else:
        body = _flash_online_kernel
        scratch = [pltpu.VMEM((rows, FLASH_DQ), BF16), pltpu.VMEM((rows, LANE), F32),
                   pltpu.VMEM((rows, LANE), F32), pltpu.VMEM((rows, HEAD_DIM), F32)]
    return pl.pallas_call(
        functools.partial(body, grp=grp, tq=tq),
        grid=(b, n_kv_heads, q_rows // tq, kv_rows // tk),
        in_specs=[pl.BlockSpec((1, tq, grp * FLASH_DQ), lambda bi, h, i, j: (bi, i + qo, h)),
                  pl.BlockSpec((1, tk, HEAD_DIM), lambda bi, h, i, j: (bi, j + ko, h)),
                  pl.BlockSpec((1, tk, LANE), lambda bi, h, i, j: (bi, j + ko, 0)),
                  pl.BlockSpec((1, tk, HEAD_DIM), lambda bi, h, i, j: (bi, j + ko, v_col0 + h))],
        out_specs=pl.BlockSpec((1, tq, grp * HEAD_DIM), lambda bi, h, i, j: (bi, i, h)),
        out_shape=jax.ShapeDtypeStruct((b, q_rows, n_kv_heads * grp * HEAD_DIM), BF16),
        scratch_shapes=scratch,
        compiler_params=_cp(("parallel", "parallel", "parallel", "arbitrary")),
        name="flash_bounded" if bounded else "flash_online",
    )(q, k, ka, v)


def _attention(q, k, ka, v, shift, *, n_kv_heads, grp, n_lat, v_col0, tq_lat, tk_lat):
    lt = q.shape[1]
    n_ctx = lt - n_lat
    tk = _tile(lt, tk_lat, 256)
    tq = _tile(n_lat, tq_lat, 256)

    def run(bounded):
        def go(q, k, ka, v):
            y_lat = _flash(q, k, ka, v, bounded=bounded, n_kv_heads=n_kv_heads, grp=grp, tq=tq, tk=tk,
                           q_rows=n_lat, q_off=0, kv_rows=lt, kv_off=0, v_col0=v_col0)
            y_ctx = _flash(q, k, ka, v, bounded=bounded, n_kv_heads=n_kv_heads, grp=grp, tq=n_ctx, tk=n_ctx,
                           q_rows=n_ctx, q_off=n_lat, kv_rows=n_ctx, kv_off=n_lat, v_col0=v_col0)
            return jnp.concatenate([y_lat, y_ctx], axis=1)
        return go

    return lax.cond(shift <= FLASH_MAX_SHIFT, run(True), run(False), q, k, ka, v)


def _conv_kernel(prev_ref, cur_ref, next_ref, w_ref, b_ref, o_ref, *, tr, n_lat, lt):
    rows = pl.program_id(1) * tr + lax.broadcasted_iota(jnp.int32, (tr, 1), 0)
    ext = jnp.concatenate([prev_ref[0].astype(F32), cur_ref[0].astype(F32), next_ref[0].astype(F32)], axis=0)
    w = w_ref[...]
    acc = jnp.zeros((tr, cur_ref.shape[2]), F32) + b_ref[...]
    half = MB_CONV // 2
    for kk in range(MB_CONV):
        src = rows + (kk - half)
        ok = jnp.logical_and(jnp.logical_and(src >= 0, src < lt), (src >= n_lat) == (rows >= n_lat))
        acc = acc + jnp.where(ok, ext[8 + kk - half:8 + kk - half + tr, :], 0.0) * w[kk:kk + 1, :]
    o_ref[0] = _silu(acc).astype(o_ref.dtype)


def _mamba_conv(proj, conv_w, conv_b, n_lat):
    b, lt, _ = proj.shape
    tr = _tile(lt, 1280, 256)
    cw = 512
    cb0 = OFF_XBC // cw
    r8 = tr // 8
    nblk8 = lt // 8
    w8 = jnp.zeros((8, MB_XBC), F32).at[:MB_CONV].set(conv_w.T.astype(F32))
    return pl.pallas_call(
        functools.partial(_conv_kernel, tr=tr, n_lat=n_lat, lt=lt),
        grid=(b, lt // tr, MB_XBC // cw),
        in_specs=[pl.BlockSpec((1, 8, cw), lambda bi, i, c: (bi, jnp.maximum(i * r8 - 1, 0), cb0 + c)),
                  pl.BlockSpec((1, tr, cw), lambda bi, i, c: (bi, i, cb0 + c)),
                  pl.BlockSpec((1, 8, cw), lambda bi, i, c: (bi, jnp.minimum((i + 1) * r8, nblk8 - 1), cb0 + c)),
                  pl.BlockSpec((8, cw), lambda bi, i, c: (0, c)),
                  pl.BlockSpec((1, cw), lambda bi, i, c: (0, c))],
        out_specs=pl.BlockSpec((1, tr, cw), lambda bi, i, c: (bi, i, c)),
        out_shape=jax.ShapeDtypeStruct((b, lt, MB_XBC), BF16),
        compiler_params=_cp(("parallel", "parallel", "parallel")),
        name="mamba_conv",
    )(proj, proj, proj, w8, conv_b.reshape(1, -1).astype(F32))


def _softplus(x):
    return jnp.maximum(x, 0.0) + jnp.log1p(jnp.exp(-jnp.abs(x)))


def _ssd_kernel(xs_ref, bm_ref, cm_ref, dt_ref, bias_ref, a_ref, y_ref, h_sc):
    t = SSD_CHUNK
    d = pl.program_id(2)

    @pl.when(pl.program_id(3) == 0)
    def _():
        h_sc[...] = jnp.zeros_like(h_sc)

    dt = _softplus(dt_ref[0].astype(F32) + bias_ref[0])
    a = dt * a_ref[0]
    row = lax.broadcasted_iota(jnp.int32, (t, t), 0)
    col = lax.broadcasted_iota(jnp.int32, (t, t), 1)
    sgn = 1 - 2 * d
    mask = (row - col) * sgn >= 0
    tri = mask.astype(BF16)
    a_hi = a.astype(BF16)
    a_lo = (a - a_hi.astype(F32)).astype(BF16)
    cum = jnp.dot(tri, a_hi, preferred_element_type=F32) + jnp.dot(tri, a_lo, preferred_element_type=F32)
    total = jnp.sum(a, axis=0, keepdims=True)
    cum_t = cum.T
    e_cum = jnp.exp(cum)
    e_end = jnp.exp(total - cum)
    e_tot = jnp.exp(total)

    bm = bm_ref[0]
    cm = cm_ref[0]
    scores = lax.dot_general(cm, bm, (((1,), (1,)), ((), ())), preferred_element_type=F32)
    bm_t = bm.astype(F32).T.astype(BF16)
    lane = lax.broadcasted_iota(jnp.int32, (t, LANE), 1)
    first = lane < MB_HEAD_DIM
    width = MB_HPG * MB_HEAD_DIM
    expand = (lax.broadcasted_iota(jnp.int32, (LANE, width), 1) // MB_HEAD_DIM
              == lax.broadcasted_iota(jnp.int32, (LANE, width), 0)).astype(BF16)
    stack = jnp.concatenate([dt, e_cum, e_end, jnp.broadcast_to(e_tot, (8, LANE))], axis=0)
    s_hi = stack.astype(BF16)
    s_lo = (stack - s_hi.astype(F32)).astype(BF16)
    wide = (jnp.dot(s_hi, expand, preferred_element_type=F32) + jnp.dot(s_lo, expand, preferred_element_type=F32))
    dt_w, e_cum_w, e_end_w, e_tot_w = wide[:t], wide[t:2 * t], wide[2 * t:3 * t], wide[3 * t:3 * t + 1]

    for m in range(MB_HPG // 2):
        j = 2 * m
        cols = slice(m * LANE, (m + 1) * LANE)
        xdt = xs_ref[0, :, cols].astype(F32) * dt_w[:, cols]
        h = h_sc[:, cols]
        y = jnp.dot(cm, h.astype(BF16), preferred_element_type=F32) * e_cum_w[:, cols]
        xw = (xdt * e_end_w[:, cols]).astype(BF16)
        h_sc[:, cols] = h * e_tot_w[:, cols] + jnp.dot(bm_t, xw, preferred_element_type=F32)
        xdt_b = xdt.astype(BF16)
        for q in range(2):
            diff = cum[:, j + q:j + q + 1] - cum_t[j + q:j + q + 1, :]
            decay = jnp.where(mask, jnp.exp(jnp.minimum(diff, 0.0)), 0.0)
            keep = first if q == 0 else jnp.logical_not(first)
            y = y + jnp.dot((scores * decay).astype(BF16), jnp.where(keep, xdt_b, jnp.zeros_like(xdt_b)),
                            preferred_element_type=F32)
        y_ref[0, 0, :, cols] = y


def _ssd(xbc, proj, dt_bias4, a4, n_lat):
    b, lt, _ = xbc.shape
    t = SSD_CHUNK
    n_lat_c = n_lat // t
    n_ctx_c = (lt - n_lat) // t
    nc = n_lat_c + n_ctx_c
    width = MB_HPG * MB_HEAD_DIM

    def chunk(d, i):
        fwd = jnp.where(i < n_ctx_c, n_lat_c + i, i - n_ctx_c)
        bwd = jnp.where(i < n_ctx_c, n_lat_c + n_ctx_c - 1 - i, n_lat_c - 1 - (i - n_ctx_c))
        return jnp.where(d == 0, fwd, bwd)

    return pl.pallas_call(
        _ssd_kernel,
        grid=(b, MB_GROUPS, 2, nc),
        in_specs=[pl.BlockSpec((1, t, width), lambda bi, g, d, i: (bi, chunk(d, i), g)),
                  pl.BlockSpec((1, t, MB_STATE), lambda bi, g, d, i: (bi, chunk(d, i), MB_INNER // MB_STATE + g)),
                  pl.BlockSpec((1, t, MB_STATE), lambda bi, g, d, i: (bi, chunk(d, i), MB_INNER // MB_STATE + MB_GROUPS + g)),
                  pl.BlockSpec((1, t, LANE), lambda bi, g, d, i: (bi, chunk(d, i), OFF_DT // LANE + d * MB_GROUPS + g)),
                  pl.BlockSpec((1, 1, LANE), lambda bi, g, d, i: (d * MB_GROUPS + g, 0, 0)),
                  pl.BlockSpec((1, 1, LANE), lambda bi, g, d, i: (d * MB_GROUPS + g, 0, 0))],
        out_specs=pl.BlockSpec((1, 1, t, width), lambda bi, g, d, i: (bi, d, chunk(d, i), g)),
        out_shape=jax.ShapeDtypeStruct((b, 2, lt, MB_INNER), F32),
        scratch_shapes=[pltpu.VMEM((MB_STATE, width), F32)],
        compiler_params=_cp(("parallel", "parallel", "parallel", "arbitrary")),
        name="ssd_scan",
    )(xbc, xbc, xbc, proj, dt_bias4, a4)


def _mamba_finish_kernel(yf_ref, yb_ref, xs_ref, z_ref, d_ref, g_ref, o_ref):
    y = yf_ref[0, 0] + yb_ref[0, 0] + xs_ref[0].astype(F32) * d_ref[...]
    y = y * _silu(z_ref[0].astype(F32))
    o_ref[0] = (y * lax.rsqrt(jnp.mean(y * y, axis=-1, keepdims=True) + EPS) * g_ref[...]).astype(o_ref.dtype)


def _mamba_finish(y2, xbc, proj, d_vec, gain):
    b, lt, _ = xbc.shape
    tm = _tile(lt, 640, 256)
    w = MB_INNER
    return pl.pallas_call(
        _mamba_finish_kernel,
        grid=(b, lt // tm),
        in_specs=[pl.BlockSpec((1, 1, tm, w), lambda bi, i: (bi, 0, i, 0)),
                  pl.BlockSpec((1, 1, tm, w), lambda bi, i: (bi, 1, i, 0)),
                  pl.BlockSpec((1, tm, w), lambda bi, i: (bi, i, 0)),
                  pl.BlockSpec((1, tm, w), lambda bi, i: (bi, i, OFF_MZ // w)),
                  pl.BlockSpec((1, w), lambda bi, i: (0, 0)),
                  pl.BlockSpec((1, w), lambda bi, i: (0, 0))],
        out_specs=pl.BlockSpec((1, tm, w), lambda bi, i: (bi, i, 0)),
        out_shape=jax.ShapeDtypeStruct((b, lt, w), BF16),
        compiler_params=_cp(("parallel", "parallel")),
        name="mamba_finish",
    )(y2, y2, xbc, proj, d_vec, gain.reshape(1, -1))


S5_LANE_GROUPS = LANE // S5_GROUP
S5_BLOCKS = S5_GROUPS // S5_LANE_GROUPS
S5_FLAT = S5_CHUNK * LANE
S5_STATE_W = S5_LANE_GROUPS * 2 * S5_STATE


def _s5_in_kernel(u_ref, w_ref, yi_ref, sf_ref, sfs_ref, sb_ref, sbs_ref, x_sc, *, cb):
    j = pl.program_id(2)

    @pl.when(j == 0)
    def _():
        x_sc[...] = jnp.concatenate([u_ref[pl.ds(s, cb, stride=S5_CHUNK), :] for s in range(S5_CHUNK)],
                                    axis=-1).astype(BF16)

    r = jnp.dot(x_sc[...], w_ref[0, 0], preferred_element_type=F32)

    @pl.when(j == 0)
    def _():
        yi_ref[...] = r.astype(yi_ref.dtype)

    @pl.when(j == 1)
    def _():
        sf_ref[...] = r[:, :S5_STATE_W]
        sfs_ref[...] = r[:, S5_STATE_W:]

    @pl.when(j == 2)
    def _():
        sb_ref[...] = r[:, :S5_STATE_W]
        sbs_ref[...] = r[:, S5_STATE_W:]


def _s5_in(u2, w):
    rows = u2.shape[0]
    m = rows // S5_CHUNK
    cb = _tile(m, 520, 96, 48)
    state_spec = pl.BlockSpec((cb, S5_STATE_W), lambda l, r, j: (r, l))
    state_shape = jax.ShapeDtypeStruct((m, S5_BLOCKS * S5_STATE_W), F32)
    return pl.pallas_call(
        functools.partial(_s5_in_kernel, cb=cb),
        grid=(S5_BLOCKS, m // cb, 3),
        in_specs=[pl.BlockSpec((cb * S5_CHUNK, LANE), lambda l, r, j: (r, l)),
                  pl.BlockSpec((1, 1, S5_FLAT, S5_FLAT), lambda l, r, j: (l, j, 0, 0))],
        out_specs=[pl.BlockSpec((cb, S5_FLAT), lambda l, r, j: (r, l))] + [state_spec] * 4,
        out_shape=[jax.ShapeDtypeStruct((m, S5_BLOCKS * S5_FLAT), BF16)] + [state_shape] * 4,
        scratch_shapes=[pltpu.VMEM((cb, S5_FLAT), BF16)],
        compiler_params=_cp(("parallel", "parallel", "arbitrary")),
        name="s5_intra",
    )(u2, w)


S5_SCAN_LANES = 1024


def _s5_scan_kernel(sf_ref, sfs_ref, sb_ref, sbs_ref, a_ref, hf_ref, hb_ref, st_sc, *, tc):
    @pl.when(pl.program_id(1) == 0)
    def _():
        st_sc[...] = jnp.zeros_like(st_sc)

    for lc in range(sf_ref.shape[1] // S5_SCAN_LANES):
        ln = slice(lc * S5_SCAN_LANES, (lc + 1) * S5_SCAN_LANES)
        a1f, a2f, a1b, a2b = a_ref[0:1, ln], a_ref[1:2, ln], a_ref[2:3, ln], a_ref[3:4, ln]

        def body(c, carry):
            hf, hfs, hb, hbs = carry
            hf_ref[pl.ds(c, 1), ln] = hf
            nf = a1f * hf + a2f * hfs + sf_ref[pl.ds(c, 1), ln]
            nfs = a1f * hfs - a2f * hf + sfs_ref[pl.ds(c, 1), ln]
            cb = tc - 1 - c
            hb_ref[pl.ds(cb, 1), ln] = hb
            nb = a1b * hb + a2b * hbs + sb_ref[pl.ds(cb, 1), ln]
            nbs = a1b * hbs - a2b * hb + sbs_ref[pl.ds(cb, 1), ln]
            return nf, nfs, nb, nbs

        out = lax.fori_loop(0, tc, body, tuple(st_sc[n:n + 1, ln] for n in range(4)))
        for n in range(4):
            st_sc[n:n + 1, ln] = out[n]


def _s5_scan(sf, sfs, sb, sbs, a_rows, n_batch, n_lat_chunks):
    m, lanes = sf.shape
    nc = m // n_batch
    tc = 16
    nlt = n_lat_chunks // tc
    nt = nc // tc

    def tile_f(bi, i):
        return bi * nt + jnp.where(i < nt - nlt, nlt + i, i - (nt - nlt))

    def tile_b(bi, i):
        return bi * nt + jnp.where(i < nt - nlt, nt - 1 - i, nlt - 1 - (i - (nt - nlt)))

    fwd = pl.BlockSpec((tc, lanes), lambda bi, i: (tile_f(bi, i), 0))
    bwd = pl.BlockSpec((tc, lanes), lambda bi, i: (tile_b(bi, i), 0))
    return pl.pallas_call(
        functools.partial(_s5_scan_kernel, tc=tc),
        grid=(n_batch, nt),
        in_specs=[fwd, fwd, bwd, bwd, pl.BlockSpec((8, lanes), lambda bi, i: (0, 0))],
        out_specs=[fwd, bwd],
        out_shape=[jax.ShapeDtypeStruct((m, lanes), F32), jax.ShapeDtypeStruct((m, lanes), F32)],
        scratch_shapes=[pltpu.VMEM((8, lanes), F32)],
        compiler_params=_cp(("parallel", "arbitrary")),
        name="s5_state_scan",
    )(sf, sfs, sb, sbs, a_rows)


def _s5_out_kernel(yi_ref, hf_ref, hb_ref, q_ref, o_ref, *, cb):
    h = jnp.concatenate([hf_ref[...], hb_ref[...]], axis=-1).astype(BF16)
    y = yi_ref[...].astype(F32) + jnp.dot(h, q_ref[0], preferred_element_type=F32)
    for t in range(S5_CHUNK):
        o_ref[pl.ds(t, cb, stride=S5_CHUNK), :] = y[:, t * LANE:(t + 1) * LANE]


def _s5_out(yi, hf, hb, q):
    m = yi.shape[0]
    cb = _tile(m, 520, 96, 48)
    return pl.pallas_call(
        functools.partial(_s5_out_kernel, cb=cb),
        grid=(S5_BLOCKS, m // cb),
        in_specs=[pl.BlockSpec((cb, S5_FLAT), lambda l, r: (r, l)),
                  pl.BlockSpec((cb, S5_STATE_W), lambda l, r: (r, l)),
                  pl.BlockSpec((cb, S5_STATE_W), lambda l, r: (r, l)),
                  pl.BlockSpec((1, 2 * S5_STATE_W, S5_FLAT), lambda l, r: (l, 0, 0))],
        out_specs=pl.BlockSpec((cb * S5_CHUNK, LANE), lambda l, r: (r, l)),
        out_shape=jax.ShapeDtypeStruct((m * S5_CHUNK, S5_WIDTH), F32),
        compiler_params=_cp(("parallel", "parallel")),
        name="s5_readout",
    )(yi, hf, hb, q)


def _gelu_tanh(x):
    return 0.5 * x * (1.0 + jnp.tanh(math.sqrt(2.0 / math.pi) * (x + 0.044715 * (x * x * x))))


def _s5_finish_kernel(y_ref, u_ref, d_ref, w_ref, o_ref):
    y = y_ref[0].astype(F32) + d_ref[...] * u_ref[0].astype(F32)
    g = _gelu_tanh(y)
    o_ref[0] = (g * _sigmoid(jnp.dot(g.astype(BF16), w_ref[...], preferred_element_type=F32))).astype(o_ref.dtype)


def _s5_finish(y, proj, d_vec, w_glu):
    b, lt, w = y.shape
    tm = _tile(lt, 640, 256)
    return pl.pallas_call(
        _s5_finish_kernel,
        grid=(b, lt // tm),
        in_specs=[pl.BlockSpec((1, tm, w), lambda bi, i: (bi, i, 0)),
                  pl.BlockSpec((1, tm, w), lambda bi, i: (bi, i, OFF_S5 // w)),
                  pl.BlockSpec((1, w), lambda bi, i: (0, 0)),
                  pl.BlockSpec((w, w), lambda bi, i: (0, 0))],
        out_specs=pl.BlockSpec((1, tm, w), lambda bi, i: (bi, i, 0)),
        out_shape=jax.ShapeDtypeStruct((b, lt, w), BF16),
        compiler_params=_cp(("parallel", "parallel")),
        name="s5_finish",
    )(y, proj, d_vec.reshape(1, -1), w_glu)


def _lift_cols(rows2d, rows_per_group, inner):
    r, k = rows2d.shape
    j = np.arange(k * S5_LANE_GROUPS)
    src = (j // (S5_LANE_GROUPS * inner)) * inner + j % inner
    sel = jnp.asarray(np.arange(k)[:, None] == src[None, :], dtype=BF16)
    tiled = jnp.dot(rows2d.astype(BF16), sel, preferred_element_type=F32)
    col_group = jnp.asarray((j // inner) % S5_LANE_GROUPS, dtype=jnp.int32)
    row_group = (jnp.arange(r, dtype=jnp.int32) // rows_per_group) % S5_LANE_GROUPS
    return jnp.where(row_group[:, None] == col_group[None, :], tiled, 0.0).astype(BF16)


def _s5_operators(p):
    t = S5_CHUNK
    lam = lax.complex(p["s5_lam_re"].astype(F32), p["s5_lam_im"].astype(F32))
    step = jnp.exp(p["s5_log_step"].astype(F32))[..., None]
    lam_bar = jnp.exp(lam * step)
    b_bar = ((lam_bar - 1.0) / lam)[..., None] * lax.complex(p["s5_b_re"].astype(F32), p["s5_b_im"].astype(F32))
    c_mat = lax.complex(p["s5_c_re"].astype(F32), p["s5_c_im"].astype(F32))
    ks = jnp.arange(t + 1, dtype=F32)
    pw = jnp.exp((lam * step)[:, :, None, :] * ks[None, None, :, None])
    kern = jnp.real(jnp.einsum("dgcn,dgkn,dgni->dgkci", c_mat, pw[:, :, :t], b_bar))
    s_idx = jnp.arange(t)[:, None]
    t_idx = jnp.arange(t)[None, :]
    lag_f = jnp.clip(t_idx - s_idx, 0, t - 1)
    lag_b = jnp.clip(s_idx - t_idx, 0, t - 1)
    kf = jnp.where((t_idx >= s_idx)[None, :, :, None, None], kern[0][:, lag_f], 0.0)
    kb = jnp.where((s_idx >= t_idx)[None, :, :, None, None], kern[1][:, lag_b], 0.0)
    w_intra = jnp.transpose(kf + kb, (0, 1, 4, 2, 3)).reshape(S5_GROUPS, t * S5_GROUP, t * S5_GROUP)
    inj_f = pw[0][:, t - 1 - jnp.arange(t), :, None] * b_bar[0][:, None]
    inj_b = pw[1][:, jnp.arange(t), :, None] * b_bar[1][:, None]

    def inj_mat(z):
        z = jnp.transpose(z, (0, 1, 3, 2)).reshape(S5_GROUPS, t * S5_GROUP, S5_STATE)
        return jnp.concatenate([jnp.real(z), jnp.imag(z), jnp.imag(z), jnp.real(z)], axis=-1)

    lg = S5_LANE_GROUPS

    def lift_in(mat, inner):
        rows = jnp.transpose(mat.reshape(S5_BLOCKS, lg, t, S5_GROUP, mat.shape[-1]), (0, 2, 1, 3, 4))
        return _lift_cols(rows.reshape(S5_BLOCKS * S5_FLAT, mat.shape[-1]), S5_GROUP, inner).reshape(
            S5_BLOCKS, S5_FLAT, S5_FLAT)

    w_in = jnp.stack([lift_in(w_intra, S5_GROUP), lift_in(inj_mat(inj_f), 2 * S5_STATE),
                      lift_in(inj_mat(inj_b), 2 * S5_STATE)], axis=1)
    m_f = c_mat[0][:, None, :, :] * pw[0][:, 1 + jnp.arange(t), None, :]
    m_b = c_mat[1][:, None, :, :] * pw[1][:, t - jnp.arange(t), None, :]

    def read_mat(z):
        z = jnp.transpose(z.reshape(S5_GROUPS, t * S5_GROUP, S5_STATE), (0, 2, 1))
        return jnp.concatenate([jnp.real(z), -jnp.imag(z)], axis=1).astype(BF16)

    def trans(z):
        re, im = jnp.real(z), jnp.imag(z)
        return [jnp.concatenate([re, re], -1).reshape(1, -1), jnp.concatenate([-im, im], -1).reshape(1, -1)]

    a_rows = jnp.concatenate(trans(pw[0][:, t]) + trans(pw[1][:, t])
                             + [jnp.zeros((4, 2 * S5_STATE * S5_GROUPS), F32)], axis=0)
    def lift_read(z):
        return _lift_cols(z.reshape(S5_GROUPS * 2 * S5_STATE, z.shape[-1]), 2 * S5_STATE, S5_GROUP).reshape(
            S5_BLOCKS, S5_STATE_W, S5_FLAT)

    q = jnp.concatenate([lift_read(read_mat(m_f)), lift_read(read_mat(m_b))], axis=1)
    return dict(w_in=w_in, q=q, a_rows=a_rows)


def _s5_mixer(u32, proj, ops, d_vec, w_glu, n_lat):
    b, lt, _ = proj.shape
    yi, sf, sfs, sb, sbs = _s5_in(u32.reshape(b * lt, S5_WIDTH), ops["w_in"])
    hf, hb = _s5_scan(sf, sfs, sb, sbs, ops["a_rows"], b, n_lat // S5_CHUNK)
    y = _s5_out(yi, hf, hb, ops["q"]).reshape(b, lt, S5_WIDTH)
    return _s5_finish(y, proj, d_vec, w_glu)


def _merge_kernel(ya_ref, yb_ref, yc_ref, yd_ref, g0_ref, g1_ref, g2_ref, g3_ref,
                  wa_ref, wb_ref, wc_ref, wd_ref, o_ref):
    acc = None
    for y_ref, g_ref, w_ref in ((ya_ref, g0_ref, wa_ref), (yb_ref, g1_ref, wb_ref),
                                (yc_ref, g2_ref, wc_ref), (yd_ref, g3_ref, wd_ref)):
        term = _sigmoid(g_ref[0].astype(F32)) * jnp.dot(y_ref[0], w_ref[...], preferred_element_type=F32)
        acc = term if acc is None else acc + term
    o_ref[0] = acc.astype(o_ref.dtype)


def _merge(ys, proj, ws):
    b, lt, _ = proj.shape
    d = D_MODEL
    tm = _tile(lt, 1280, 256)
    tn = 512
    nb = d // tn
    in_specs = [pl.BlockSpec((1, tm, y.shape[2]), lambda bi, i, j: (bi, i, 0)) for y in ys]
    in_specs += [pl.BlockSpec((1, tm, tn), functools.partial(lambda bi, i, j, br: (bi, i, br * nb + j), br=br))
                 for br in range(4)]
    in_specs += [pl.BlockSpec((w.shape[0], tn), lambda bi, i, j: (0, j)) for w in ws]
    return pl.pallas_call(
        _merge_kernel,
        grid=(b, lt // tm, nb),
        in_specs=in_specs,
        out_specs=pl.BlockSpec((1, tm, tn), lambda bi, i, j: (bi, i, j)),
        out_shape=jax.ShapeDtypeStruct((b, lt, d), BF16),
        compiler_params=_cp(("parallel", "parallel", "parallel")),
        name="branch_merge",
    )(*ys, proj, proj, proj, proj, *ws)


def _row_gate(ml, mc, row_tile, tm, n_lat, row):
    rows = row_tile * tm + lax.broadcasted_iota(jnp.int32, (tm, 1), 0)
    return jnp.where(rows >= n_lat, mc[row:row + 1], ml[row:row + 1])


def _outproj_kernel(m_ref, w_ref, x_ref, ml_ref, mc_ref, o_ref, *, tm, n_lat):
    gate = _row_gate(ml_ref[0], mc_ref[0], pl.program_id(1), tm, n_lat, 2)
    o_ref[0] = x_ref[0] + gate * jnp.dot(m_ref[0], w_ref[...], preferred_element_type=F32)


def _outproj(m, w_out, x, mods, n_lat):
    b, lt, d = x.shape
    tm = _tile(lt, 1280, 256)
    tn = 512
    return pl.pallas_call(
        functools.partial(_outproj_kernel, tm=tm, n_lat=n_lat),
        grid=(b, lt // tm, d // tn),
        in_specs=[pl.BlockSpec((1, tm, d), lambda bi, i, j: (bi, i, 0)),
                  pl.BlockSpec((d, tn), lambda bi, i, j: (0, j)),
                  pl.BlockSpec((1, tm, tn), lambda bi, i, j: (bi, i, j)),
                  pl.BlockSpec((1, 8, tn), lambda bi, i, j: (bi, 0, j)),
                  pl.BlockSpec((1, 8, tn), lambda bi, i, j: (b, 0, j))],
        out_specs=pl.BlockSpec((1, tm, tn), lambda bi, i, j: (bi, i, j)),
        out_shape=jax.ShapeDtypeStruct((b, lt, d), F32),
        compiler_params=_cp(("parallel", "parallel", "parallel")),
        name="out_proj",
    )(m, w_out, x, mods, mods)


def _prenorm_router_kernel(x_ref, g_ref, ml_ref, mc_ref, wr_ref, br_ref, h_ref, lg_ref, *, tm, n_lat):
    def chunk(r, carry):
        rs = pl.multiple_of(r * ROW_CHUNK, ROW_CHUNK)
        h = _norm_mod(x_ref[0, pl.ds(rs, ROW_CHUNK), :], g_ref[...], ml_ref[0], mc_ref[0],
                      pl.program_id(1) * tm + rs, n_lat, 3).astype(BF16)
        h_ref[0, pl.ds(rs, ROW_CHUNK), :] = h
        lg_ref[0, pl.ds(rs, ROW_CHUNK), :] = jnp.dot(h, wr_ref[...], preferred_element_type=F32) + br_ref[...]
        return carry

    lax.fori_loop(0, tm // ROW_CHUNK, chunk, 0)


def _prenorm_router(x, gain, mods, w_r, b_r, n_lat):
    b, lt, d = x.shape
    tm = _tile(lt, 640, 256)
    return pl.pallas_call(
        functools.partial(_prenorm_router_kernel, tm=tm, n_lat=n_lat),
        grid=(b, lt // tm),
        in_specs=[pl.BlockSpec((1, tm, d), lambda bi, i: (bi, i, 0)),
                  pl.BlockSpec((1, d), lambda bi, i: (0, 0)),
                  pl.BlockSpec((1, 8, d), lambda bi, i: (bi, 0, 0)),
                  pl.BlockSpec((1, 8, d), lambda bi, i: (b, 0, 0)),
                  pl.BlockSpec((d, LANE), lambda bi, i: (0, 0)),
                  pl.BlockSpec((1, LANE), lambda bi, i: (0, 0))],
        out_specs=[pl.BlockSpec((1, tm, d), lambda bi, i: (bi, i, 0)),
                   pl.BlockSpec((1, tm, LANE), lambda bi, i: (bi, i, 0))],
        out_shape=[jax.ShapeDtypeStruct((b, lt, d), BF16),
                   jax.ShapeDtypeStruct((b, lt, LANE), F32)],
        compiler_params=_cp(("parallel", "parallel")),
        name="moe_prenorm_router",
    )(x, gain.reshape(1, d), mods, mods, w_r, b_r)


def _moe_kernel(tile_ref, exp_ref, lo_ref, hi_ref, first_ref, x_ref, rw_ref, w1_ref, w3_ref, w2_ref, o_ref):
    k = pl.program_id(0)
    lo, hi = lo_ref[k], hi_ref[k]
    tile = x_ref.shape[0]

    @pl.when(hi > lo)
    def _():
        x = x_ref[...]
        w1, w3, w2 = w1_ref[0].astype(BF16), w3_ref[0].astype(BF16), w2_ref[0].astype(BF16)
        hid = _silu(jnp.dot(x, w1, preferred_element_type=F32)) * jnp.dot(x, w3, preferred_element_type=F32)
        y = (jnp.dot(hid.astype(BF16), w2, preferred_element_type=F32) * rw_ref[...]).astype(o_ref.dtype)
        rows = tile_ref[k] * tile + lax.broadcasted_iota(jnp.int32, (tile, 1), 0)
        mine = jnp.logical_and(rows >= lo, rows < hi)

        @pl.when(first_ref[k] == 1)
        def _():
            o_ref[...] = jnp.where(mine, y, jnp.zeros_like(y))

        @pl.when(first_ref[k] == 0)
        def _():
            o_ref[...] = jnp.where(mine, y, o_ref[...])


def _moe_ffn(xs, rw, items, w1, w3, w2):
    rows, d = xs.shape
    tile = MOE_TILE
    ff = w1.shape[2]
    item_tile, item_expert = items[0], items[1]
    grid_spec = pltpu.PrefetchScalarGridSpec(
        num_scalar_prefetch=5,
        grid=(item_tile.shape[0],),
        in_specs=[pl.BlockSpec((tile, d), lambda k, it, ie, lo, hi, fi: (it[k], 0)),
                  pl.BlockSpec((tile, 1), lambda k, it, ie, lo, hi, fi: (it[k], 0)),
                  pl.BlockSpec((1, d, ff), lambda k, it, ie, lo, hi, fi: (ie[k], 0, 0)),
                  pl.BlockSpec((1, d, ff), lambda k, it, ie, lo, hi, fi: (ie[k], 0, 0)),
                  pl.BlockSpec((1, ff, d), lambda k, it, ie, lo, hi, fi: (ie[k], 0, 0))],
        out_specs=pl.BlockSpec((tile, d), lambda k, it, ie, lo, hi, fi: (it[k], 0)),
    )
    return pl.pallas_call(
        _moe_kernel,
        grid_spec=grid_spec,
        out_shape=jax.ShapeDtypeStruct((rows, d), BF16),
        compiler_params=_cp(("arbitrary",)),
        name="moe_experts",
    )(*items, xs, rw, w1, w3, w2)


def _moe(h2, logits, w1, w3, w2):
    n, d = h2.shape
    tile = MOE_TILE
    g_prob = jax.nn.softmax(logits[:, :MOE_GROUPS], axis=-1)
    g_idx = jnp.argmax(g_prob, axis=-1)
    g_w = jnp.take_along_axis(g_prob, g_idx[:, None], axis=1)
    e_logits = logits[:, MOE_GROUPS:MOE_GROUPS + N_EXPERTS].reshape(n, MOE_GROUPS, MOE_PER_GROUP)
    e_logits = jnp.take_along_axis(e_logits, g_idx[:, None, None], axis=1)[:, 0]
    e_w, e_idx = lax.top_k(jax.nn.softmax(e_logits, axis=-1), 2)
    w = g_w * e_w / jnp.sum(e_w, axis=-1, keepdims=True)
    expert = (g_idx[:, None] * MOE_PER_GROUP + e_idx).astype(jnp.int32)

    n_rows = 2 * n
    assert n_rows % tile == 0
    n_tiles = n_rows // tile
    iota = jnp.arange(n_rows, dtype=jnp.int32)
    sorted_e, order, sorted_w = lax.sort((expert.reshape(-1), iota, w.reshape(-1)), num_keys=1, is_stable=True)
    _, sorted_pos = lax.sort((order, iota), num_keys=1)
    e_ids = jnp.arange(N_EXPERTS, dtype=jnp.int32)
    cnt_end = jnp.sum((sorted_e[None, :] <= e_ids[:, None]).astype(jnp.int32), axis=1)
    cnt_start = jnp.concatenate([jnp.zeros((1,), jnp.int32), cnt_end[:-1]])
    first_tile = cnt_start // tile
    last_tile = jnp.where(cnt_end > cnt_start, (cnt_end - 1) // tile, first_tile - 1)
    item_end = jnp.cumsum(last_tile - first_tile + 1)
    item_start = jnp.concatenate([jnp.zeros((1,), jnp.int32), item_end[:-1]])
    n_items = item_end[-1]
    k = jnp.arange(n_tiles + N_EXPERTS, dtype=jnp.int32)
    item_e = jnp.minimum(jnp.sum((item_end[None, :] <= k[:, None]).astype(jnp.int32), axis=1), N_EXPERTS - 1)
    item_t = first_tile[item_e] + k - item_start[item_e]
    valid = k < n_items
    item_e = jnp.where(valid, item_e, item_e[n_items - 1])
    item_t = jnp.where(valid, item_t, item_t[n_items - 1])
    item_lo = jnp.where(valid, jnp.maximum(cnt_start[item_e], item_t * tile), 0)
    item_hi = jnp.where(valid, jnp.minimum(cnt_end[item_e], (item_t + 1) * tile), 0)
    item_first = jnp.concatenate([jnp.ones((1,), jnp.int32), (item_t[1:] != item_t[:-1]).astype(jnp.int32)])
    items = tuple(a.astype(jnp.int32) for a in (item_t, item_e, item_lo, item_hi, item_first))

    xs = jnp.take(h2, order // 2, axis=0)
    ys = _moe_ffn(xs, sorted_w[:, None], items, w1, w3, w2)
    dest = sorted_pos.reshape(n, 2)
    return jnp.take(ys, dest[:, 0], axis=0).astype(F32) + jnp.take(ys, dest[:, 1], axis=0).astype(F32)


def _pack_w_in(w_in):
    depth, d, _ = w_in.shape
    cuts = {}
    off = 0
    for name, width in (("s5", S5_WIDTH), ("gq", GQA_OUT), ("gk", GQA_KV_HEADS * HEAD_DIM), ("gv", GQA_KV_HEADS * HEAD_DIM),
                        ("mq", MLA_Q_RANK), ("mkv", MLA_KV_RANK), ("mkr", MLA_ROPE), ("mz", MB_INNER), ("xbc", MB_XBC),
                        ("dt", 2 * MB_HEADS), ("gate", 4 * D_MODEL)):
        cuts[name] = w_in[:, :, off:off + width]
        off += width
    zeros = lambda n: jnp.zeros((depth, d, n), w_in.dtype)
    dt_blocks = []
    for blk in range(2 * MB_GROUPS):
        dt_blocks += [cuts["dt"][:, :, blk * MB_HPG:(blk + 1) * MB_HPG], zeros(LANE - MB_HPG)]
    packed = jnp.concatenate(
        [cuts["gate"], cuts["gq"], cuts["mz"], cuts["xbc"], cuts["mq"], cuts["s5"], cuts["gk"], cuts["gv"],
         cuts["mkv"], cuts["mkr"], zeros(LANE - MLA_ROPE)] + dt_blocks + [zeros(NW - OFF_DT - 4 * LANE)], axis=-1)
    return packed.astype(BF16)


def _rope_table(n_lat, n_ctx, dim, width):
    rows = n_lat // GRID_W
    row = jnp.repeat(jnp.arange(rows, dtype=F32), GRID_W)
    col = jnp.tile(jnp.arange(GRID_W, dtype=F32), rows)
    quarter = dim // 4
    inv_freq = ROPE_THETA ** (-jnp.arange(quarter, dtype=F32) / quarter)
    ang_r = row[:, None] * inv_freq
    ang_c = col[:, None] * inv_freq
    ang = jnp.concatenate([ang_r, ang_r, ang_c, ang_c], axis=-1)
    cos = jnp.concatenate([jnp.cos(ang), jnp.ones((n_lat, width - dim), F32)], axis=-1)
    sin = jnp.concatenate([jnp.sin(ang), jnp.zeros((n_lat, width - dim), F32)], axis=-1)
    cos = jnp.concatenate([cos, jnp.ones((n_ctx, width), F32)], axis=0)
    sin = jnp.concatenate([sin, jnp.zeros((n_ctx, width), F32)], axis=0)
    return cos, sin


def _rot_matrix(dim, width):
    quarter = dim // 4
    idx = jnp.arange(width)
    in_first = (idx % (2 * quarter)) < quarter
    valid = idx < dim
    src_for_first = idx + quarter
    src_for_second = idx - quarter
    rows = jnp.arange(width)[:, None]
    r = jnp.where(in_first[None, :] & (rows == src_for_first[None, :]), -1.0, 0.0)
    r = r + jnp.where((~in_first)[None, :] & (rows == src_for_second[None, :]), 1.0, 0.0)
    return (r * valid[None, :]).astype(BF16)


def _pad_lanes(v, width):
    return jnp.concatenate([v.astype(F32), jnp.zeros((width - v.shape[0],), F32)]).reshape(1, width)


def _lane_row(lane, value):
    return jnp.zeros((1, LANE), F32).at[0, lane].set(value)


def _score_bound(q_sq_norm, k_sq_norm, scale):
    return (jnp.sqrt(q_sq_norm * k_sq_norm) * (scale * LOG2E * 1.02) + 0.5).astype(F32)


def kernel(x, c, ctx, c_ctx, norm1, norm2, w_ada, b_ada, w_in, s5_lam_re, s5_lam_im, s5_log_step, s5_b_re, s5_b_im, s5_c_re, s5_c_im, s5_d, s5_w_glu, gqa_q_norm, gqa_k_norm, mla_q_norm, mla_kv_norm, mla_w_uq, mla_w_uk, mla_w_uv, mla_qn_norm, mla_kn_norm, mla_qr_norm, mla_kr_norm, mb_conv_w, mb_conv_b, mb_dt_bias, mb_a_log, mb_d, mb_norm, w_br_s5, w_br_gqa, w_br_mla, w_br_mb, w_out, moe_w_group, moe_b_group, moe_w_expert, moe_b_expert, moe_w1, moe_w3, moe_w2):
    b, n_lat, d = x.shape
    n_ctx = ctx.shape[1]
    lt = n_lat + n_ctx
    depth = w_in.shape[0]

    cvec = jnp.zeros((8, d), F32).at[:b].set(c.astype(F32)).at[b].set(c_ctx.astype(F32))
    mod_all = _ada(cvec, w_ada, b_ada).reshape(depth, 8, N_MOD, d)[:, :b + 1]
    mod_all = jnp.concatenate([mod_all, jnp.zeros((depth, b + 1, 8 - N_MOD, d), F32)], axis=2)

    w_in_p = _pack_w_in(w_in.astype(BF16))
    cos_g, sin_g = _rope_table(n_lat, n_ctx, HEAD_DIM, HEAD_DIM)
    cos_m, sin_m = _rope_table(n_lat, n_ctx, MLA_ROPE, LANE)
    rot_g = _rot_matrix(HEAD_DIM, HEAD_DIM)
    rot_m = _rot_matrix(MLA_ROPE, LANE)
    ka_g = jnp.broadcast_to(_lane_row(0, 1.0).astype(BF16), (b, lt, LANE))

    uq = mla_w_uq.reshape(depth, MLA_Q_RANK, MLA_HEADS, MLA_NOPE + MLA_ROPE)
    wq_nope = uq[..., :MLA_NOPE].reshape(depth, MLA_Q_RANK, MLA_HEADS * MLA_NOPE).astype(BF16)
    wq_rope = jnp.concatenate([uq[..., MLA_NOPE:], jnp.zeros(uq.shape[:3] + (LANE - MLA_ROPE,), uq.dtype)],
                              axis=-1).reshape(depth, MLA_Q_RANK, MLA_HEADS * LANE).astype(BF16)
    a_dec = -jnp.exp(mb_a_log.astype(F32))

    xs = jnp.concatenate([x.astype(F32), ctx.astype(F32)], axis=1)

    for i in range(depth):
        mods = mod_all[i]
        proj, u32 = _inproj(xs, norm1[i], mods, w_in_p[i], n_lat)

        s5p = dict(s5_lam_re=s5_lam_re[i], s5_lam_im=s5_lam_im[i], s5_log_step=s5_log_step[i], s5_b_re=s5_b_re[i],
                   s5_b_im=s5_b_im[i], s5_c_re=s5_c_re[i], s5_c_im=s5_c_im[i])
        ya = _s5_mixer(u32, proj, _s5_operators(s5p), s5_d[i].astype(F32), s5_w_glu[i].astype(BF16), n_lat)

        shift_g = _score_bound(HEAD_DIM * jnp.max(jnp.abs(gqa_q_norm[i])) ** 2,
                               HEAD_DIM * jnp.max(jnp.abs(gqa_k_norm[i])) ** 2, HEAD_DIM ** -0.5)
        qg, kg = _gqa_prep(proj, cos_g, sin_g, rot_g, gqa_q_norm[i].astype(F32), gqa_k_norm[i].astype(F32),
                           _lane_row(0, -shift_g))
        yb = _attention(qg, kg, ka_g, proj, shift_g, n_kv_heads=GQA_KV_HEADS, grp=GQA_HEADS // GQA_KV_HEADS,
                        n_lat=n_lat, v_col0=OFF_GV // HEAD_DIM, tq_lat=1024, tk_lat=3328)

        mp = dict(mla_q_norm=mla_q_norm[i].astype(F32), mla_kv_norm=mla_kv_norm[i].astype(F32),
                  wq_nope=wq_nope[i], wq_rope=wq_rope[i], w_uk=mla_w_uk[i].astype(BF16), w_uv=mla_w_uv[i].astype(BF16),
                  mla_qn_norm=mla_qn_norm[i].astype(F32), mla_kn_norm=mla_kn_norm[i].astype(F32),
                  qr_gain=_pad_lanes(mla_qr_norm[i], LANE), kr_gain=_pad_lanes(mla_kr_norm[i], LANE))
        shift_m = _score_bound(
            MLA_NOPE * jnp.max(jnp.abs(mla_qn_norm[i])) ** 2 + MLA_ROPE * jnp.max(jnp.abs(mla_qr_norm[i])) ** 2,
            MLA_NOPE * jnp.max(jnp.abs(mla_kn_norm[i])) ** 2 + MLA_ROPE * jnp.max(jnp.abs(mla_kr_norm[i])) ** 2,
            (MLA_NOPE + MLA_ROPE) ** -0.5)
        mp["q_shift_row"] = _lane_row(MLA_ROPE, -shift_m)
        mp["k_one_row"] = _lane_row(MLA_ROPE, 1.0)
        qm, kn, kr, vm = _mla_prep(proj, cos_m, sin_m, rot_m, mp)
        yc = _attention(qm, kn, kr, vm, shift_m, n_kv_heads=MLA_HEADS, grp=1,
                        n_lat=n_lat, v_col0=0, tq_lat=4096, tk_lat=3328)

        xbc = _mamba_conv(proj, mb_conv_w[i], mb_conv_b[i], n_lat)
        dt_bias4 = jnp.concatenate([mb_dt_bias[i].astype(F32).reshape(2 * MB_GROUPS, MB_HPG),
                                    jnp.zeros((2 * MB_GROUPS, LANE - MB_HPG), F32)], axis=-1).reshape(2 * MB_GROUPS, 1, LANE)
        a4 = jnp.concatenate([a_dec[i].reshape(2 * MB_GROUPS, MB_HPG),
                              jnp.zeros((2 * MB_GROUPS, LANE - MB_HPG), F32)], axis=-1).reshape(2 * MB_GROUPS, 1, LANE)
        y2 = _ssd(xbc, proj, dt_bias4, a4, n_lat)
        d_vec = jnp.repeat(mb_d[i].astype(F32), MB_HEAD_DIM).reshape(1, MB_INNER)
        yd = _mamba_finish(y2, xbc, proj, d_vec, mb_norm[i].astype(F32))

        merged = _merge((ya, yb, yc, yd), proj,
                        (w_br_s5[i].astype(BF16), w_br_gqa[i].astype(BF16), w_br_mla[i].astype(BF16), w_br_mb[i].astype(BF16)))
        xs = _outproj(merged, w_out[i].astype(BF16), xs, mods, n_lat)

        w_r = jnp.concatenate([moe_w_group[i], moe_w_expert[i],
                               jnp.zeros((d, LANE - MOE_GROUPS - N_EXPERTS), F32)], axis=-1).astype(BF16)
        b_r = _pad_lanes(jnp.concatenate([moe_b_group[i], moe_b_expert[i]]), LANE)
        h2, logits = _prenorm_router(xs, norm2[i], mods, w_r, b_r, n_lat)
        ff = _moe(h2.reshape(b * lt, d), logits.reshape(b * lt, LANE),
                  moe_w1[i], moe_w3[i], moe_w2[i]).reshape(b, lt, d)
        if i == depth - 1:
            return (xs[:, :n_lat] + mods[:b, 5][:, None, :] * ff[:, :n_lat]).astype(x.dtype)
        gate2 = jnp.concatenate([jnp.broadcast_to(mods[:b, 5][:, None, :], (b, n_lat, d)),
                                 jnp.broadcast_to(mods[b, 5][None, None, :], (b, n_ctx, d))], axis=1)
        xs = xs + gate2 * ff
```
